```python
import jax
import jax.numpy as jnp
from jax import lax
import numpy as np

D_MODEL = 1024
BATCH = 1
SEQ = 16384
DEPTH = 4

N_MIXERS = 4
NORM_EPS = 1e-6
GLA_HEADS = 4
GLA_DK = D_MODEL // 2
GLA_DV = D_MODEL
GLA_RANK = 16
GLA_TAU = 16.0
GLA_CHUNK = 64
GLA_IN = 2 * GLA_DK + 2 * GLA_DV + GLA_RANK
RWKV_HEAD = 64
RWKV_HEADS = D_MODEL // RWKV_HEAD
RWKV_DECAY_RANK = 64
RWKV_A_RANK = 64
RWKV_GATE_RANK = 128
RWKV_GN_EPS = 64e-5
RWKV_SHIFT_MIXES = 6
SB_HEADS = 16
SB_HEAD_DIM = D_MODEL // SB_HEADS
SB_BLOCK = 128
ML_HEADS = 4
ML_DK = D_MODEL // 2
ML_DV = D_MODEL
ML_CHUNK = 64
ML_IN = 2 * ML_DK + 2 * ML_DV + 2 * ML_HEADS
FFN_DIM = 2816
CONV_WIDTH = 3

kernel_name = "hybrid_gla_rwkv7_stickbreak_mlstm_trunk"


def rms_norm(x, g, eps=NORM_EPS):
    xf = x.astype(jnp.float32)
    y = xf * lax.rsqrt(jnp.mean(xf * xf, axis=-1, keepdims=True) + eps)
    return (y * g.astype(jnp.float32)).astype(x.dtype)


def _to_chunks(t, chunk):
    b, s, h, d = t.shape
    return t.reshape(b, s // chunk, chunk, h, d).transpose(1, 0, 3, 2, 4)


def _from_chunks(t):
    n, b, h, c, d = t.shape
    return t.transpose(1, 0, 3, 2, 4).reshape(b, n * c, h, d)


def _gates_to_chunks(t, chunk):
    b, s, h = t.shape
    return t.reshape(b, s // chunk, chunk, h).transpose(1, 0, 3, 2)


def gla_mixer(x, w_in, w_alpha_up, b_alpha, out_norm, w_out):
    f32 = jnp.float32
    b, s, _ = x.shape
    h, dk, dv, c = GLA_HEADS, GLA_DK // GLA_HEADS, GLA_DV // GLA_HEADS, GLA_CHUNK
    proj = x @ w_in
    q, k, v, r, a_low = jnp.split(
        proj, [GLA_DK, 2 * GLA_DK, 2 * GLA_DK + GLA_DV, 2 * GLA_DK + 2 * GLA_DV], axis=-1)
    log_alpha = jax.nn.log_sigmoid((a_low @ w_alpha_up + b_alpha).astype(f32)) / GLA_TAU
    q = q.astype(f32).reshape(b, s, h, dk) * dk ** -0.5
    k = k.astype(f32).reshape(b, s, h, dk)
    v = v.astype(f32).reshape(b, s, h, dv)
    qc, kc, vc = _to_chunks(q, c), _to_chunks(k, c), _to_chunks(v, c)
    cum = jnp.cumsum(_to_chunks(log_alpha.reshape(b, s, h, dk), c), axis=3)
    causal = jnp.tril(jnp.ones((c, c), dtype=bool))

    def step(state, inp):
        qi, ki, vi, bi = inp
        q_dec = qi * jnp.exp(bi)
        k_inv = ki * jnp.exp(-bi)
        scores = jnp.where(causal, jnp.einsum('bhtd,bhsd->bhts', q_dec, k_inv), 0.0)
        o = (jnp.einsum('bhts,bhsv->bhtv', scores, vi)
             + jnp.einsum('bhtd,bhdv->bhtv', q_dec, state))
        b_last = bi[:, :, -1:, :]
        k_end = ki * jnp.exp(b_last - bi)
        state = (state * jnp.exp(b_last)[:, :, 0, :, None]
                 + jnp.einsum('bhsd,bhsv->bhdv', k_end, vi))
        return state, o

    state0 = jnp.zeros((b, h, dk, dv), f32)
    _, o = lax.scan(step, state0, (qc, kc, vc, cum))
    o = _from_chunks(o)
    o = rms_norm(o, out_norm.reshape(h, dv)).reshape(b, s, GLA_DV)
    return (o.astype(x.dtype) * jax.nn.silu(r)) @ w_out


def rwkv7_mixer(x, mu, w_rkv, w0, w1, w2, a0, a1, a2, g1, g2, k_k, k_a, r_k, gn_g, gn_b, w_out):
    f32 = jnp.float32
    b, s, d = x.shape
    h, n = RWKV_HEADS, RWKV_HEAD
    x_prev = jnp.pad(x, ((0, 0), (1, 0), (0, 0)))[:, :-1]
    xx = x_prev - x
    xr, xw, xk, xv, xa, xg = (x + xx * mu[j] for j in range(RWKV_SHIFT_MIXES))
    r = xr @ w_rkv[0]
    k = xk @ w_rkv[1]
    v = xv @ w_rkv[2]
    w_log = -jax.nn.softplus(-(w0 + jnp.tanh(xw @ w1) @ w2).astype(f32)) - 0.5
    decay = jnp.exp(-jnp.exp(w_log))
    a = jax.nn.sigmoid((a0 + (xa @ a1) @ a2).astype(f32))
    g = jax.nn.sigmoid(xg @ g1) @ g2
    kk = (k * k_k).astype(f32).reshape(b, s, h, n)
    kk = kk / jnp.maximum(jnp.sqrt(jnp.sum(kk * kk, axis=-1, keepdims=True)), 1e-12)
    k = k.astype(f32) * (1.0 + (a - 1.0) * k_a.astype(f32))
    r = r.astype(f32).reshape(b, s, h, n)
    k = k.reshape(b, s, h, n)
    v = v.astype(f32).reshape(b, s, h, n)
    a = a.reshape(b, s, h, n)
    decay = decay.reshape(b, s, h, n)

    def step(state, inp):
        r_t, w_t, k_t, v_t, kk_t, a_t = inp
        sa = jnp.einsum('bhvk,bhk->bhv', state, -kk_t)
        state = (state * w_t[:, :, None, :]
                 + sa[..., None] * (kk_t * a_t)[:, :, None, :]
                 + v_t[..., None] * k_t[:, :, None, :])
        return state, jnp.einsum('bhvk,bhk->bhv', state, r_t)

    tm = lambda t: jnp.moveaxis(t, 1, 0)
    state0 = jnp.zeros((b, h, n, n), f32)
    _, y = lax.scan(step, state0, (tm(r), tm(decay), tm(k), tm(v), tm(kk), tm(a)))
    y = jnp.moveaxis(y, 0, 1)
    mean = jnp.mean(y, axis=-1, keepdims=True)
    var = jnp.mean(jnp.square(y - mean), axis=-1, keepdims=True)
    yn = ((y - mean) * lax.rsqrt(var + RWKV_GN_EPS)).reshape(b, s, d) * gn_g.astype(f32) + gn_b.astype(f32)
    bonus = jnp.sum(r * k * r_k.astype(f32).reshape(h, n), axis=-1, keepdims=True) * v
    out = (yn + bonus.reshape(b, s, d)).astype(x.dtype) * g
    return out @ w_out


def stick_breaking_mixer(x, w_qkv, w_out):
    f32 = jnp.float32
    b, s, d = x.shape
    h, dh, qb_len = SB_HEADS, SB_HEAD_DIM, SB_BLOCK
    q, k, v = jnp.split(x @ w_qkv, 3, axis=-1)
    q = q.reshape(b, s, h, dh) * dh ** -0.5
    k = k.reshape(b, s, h, dh)
    v = v.reshape(b, s, h, dh)
    n_blocks = s // qb_len
    q_blocks = q.reshape(b, n_blocks, qb_len, h, dh).transpose(1, 0, 3, 2, 4)
    starts = jnp.arange(n_blocks, dtype=jnp.int32) * qb_len
    key_pos = jnp.arange(s, dtype=jnp.int32)

    def block(args):
        q_blk, start = args
        z = jnp.einsum('bhqd,bkhd->bhqk', q_blk, k).astype(f32)
        q_pos = start + jnp.arange(qb_len, dtype=jnp.int32)
        before = key_pos[None, :] < q_pos[:, None]
        log_keep = jnp.where(before, jax.nn.log_sigmoid(-z), 0.0)
        suffix = lax.cumsum(log_keep, axis=3, reverse=True)
        log_w = jax.nn.log_sigmoid(z) + suffix - log_keep
        w = jnp.where(before, jnp.exp(log_w), 0.0)
        return jnp.einsum('bhqk,bkhd->bhqd', w.astype(v.dtype), v)

    o = lax.map(block, (q_blocks, starts))
    o = o.transpose(1, 0, 3, 2, 4).reshape(b, s, d)
    return o @ w_out


def mlstm_mixer(x, w_in, b_if, out_norm, w_out):
    f32 = jnp.float32
    b, s, _ = x.shape
    h, dk, dv, c = ML_HEADS, ML_DK // ML_HEADS, ML_DV // ML_HEADS, ML_CHUNK
    proj = x @ w_in
    q, k, v, o_pre, if_pre = jnp.split(
        proj, [ML_DK, 2 * ML_DK, 2 * ML_DK + ML_DV, 2 * ML_DK + 2 * ML_DV], axis=-1)
    if_pre = (if_pre + b_if).astype(f32)
    i_pre = if_pre[..., :h]
    log_f = jax.nn.log_sigmoid(if_pre[..., h:])
    qc = _to_chunks(q.astype(f32).reshape(b, s, h, dk), c)
    kc = _to_chunks(k.astype(f32).reshape(b, s, h, dk) * dk ** -0.5, c)
    vc = _to_chunks(v.astype(f32).reshape(b, s, h, dv), c)
    ic = _gates_to_chunks(i_pre, c)
    cum_f = jnp.cumsum(_gates_to_chunks(log_f, c), axis=-1)
    causal = jnp.tril(jnp.ones((c, c), dtype=bool))

    def step(carry, inp):
        c_st, n_st, m_prev = carry
        qi, ki, vi, ii, bi = inp
        d_log = jnp.where(causal, bi[..., :, None] - bi[..., None, :] + ii[..., None, :], -jnp.inf)
        inter = bi + m_prev[..., None]
        m_t = jnp.maximum(inter, jnp.max(d_log, axis=-1))
        w_intra = jnp.exp(d_log - m_t[..., None])
        scale = jnp.exp(inter - m_t)
        qk = jnp.einsum('bhtd,bhsd->bhts', qi, ki) * w_intra
        num = (jnp.einsum('bhts,bhsv->bhtv', qk, vi)
               + scale[..., None] * jnp.einsum('bhtd,bhdv->bhtv', qi, c_st))
        den = jnp.sum(qk, axis=-1) + scale * jnp.einsum('bhtd,bhd->bht', qi, n_st)
        h_out = num / jnp.maximum(jnp.abs(den), jnp.exp(-m_t))[..., None]
        b_last = bi[..., -1]
        g_end = b_last[..., None] - bi + ii
        m_new = jnp.maximum(b_last + m_prev, jnp.max(g_end, axis=-1))
        w_end = jnp.exp(g_end - m_new[..., None])
        carry_scale = jnp.exp(b_last + m_prev - m_new)
        c_st = carry_scale[..., None, None] * c_st + jnp.einsum('bhs,bhsd,bhsv->bhdv', w_end, ki, vi)
        n_st = carry_scale[..., None] * n_st + jnp.einsum('bhs,bhsd->bhd', w_end, ki)
        return (c_st, n_st, m_new), h_out

    carry0 = (jnp.zeros((b, h, dk, dv), f32), jnp.zeros((b, h, dk), f32), jnp.zeros((b, h), f32))
    _, hs = lax.scan(step, carry0, (qc, kc, vc, ic, cum_f))
    hs = _from_chunks(hs)
    hs = rms_norm(hs, out_norm.reshape(h, dv)).reshape(b, s, ML_DV)
    return (hs.astype(x.dtype) * jax.nn.sigmoid(o_pre)) @ w_out


def conv_ffn(x, w_up, conv_w, conv_b, w_down):
    s = x.shape[1]
    u = x @ w_up
    u_pad = jnp.pad(u, ((0, 0), (CONV_WIDTH - 1, 0), (0, 0)))
    u = sum(u_pad[:, j:j + s] * conv_w[j] for j in range(CONV_WIDTH)) + conv_b
    gate, up = jnp.split(u, 2, axis=-1)
    return (jax.nn.silu(gate) * up) @ w_down


def setup_inputs(seed: int = 0) -> dict:
    key = jax.random.key(seed)
    ks = iter(jax.random.split(key, 96))
    f32 = jnp.float32
    D, F2 = D_MODEL, 2 * FFN_DIM

    def nrm(shape, scale):
        return scale * jax.random.normal(next(ks), shape, f32)

    def dense(fan_in, shape):
        return nrm(shape, fan_in ** -0.5)

    def gain(n):
        return 1.0 + nrm((n,), 0.02)

    def unif(shape, lo, hi):
        return jax.random.uniform(next(ks), shape, f32, lo, hi)

    p = {}

    def ffn_params(i):
        p[f"l{i}_norm2"] = gain(D)
        p[f"l{i}_ffn_w_up"] = dense(D, (D, F2))
        p[f"l{i}_ffn_conv_w"] = dense(CONV_WIDTH, (CONV_WIDTH, F2))
        p[f"l{i}_ffn_conv_b"] = nrm((F2,), 0.02)
        p[f"l{i}_ffn_w_down"] = dense(FFN_DIM, (FFN_DIM, D))

    p["x"] = nrm((BATCH, SEQ, D), 1.0)
    p["l0_norm1"] = gain(D)
    p["l0_gla_w_in"] = dense(D, (D, GLA_IN))
    p["l0_gla_w_alpha_up"] = dense(GLA_RANK, (GLA_RANK, GLA_DK))
    p["l0_gla_b_alpha"] = nrm((GLA_DK,), 0.1)
    p["l0_gla_out_norm"] = gain(GLA_DV)
    p["l0_gla_w_out"] = dense(GLA_DV, (GLA_DV, D))
    ffn_params(0)
    p["l1_norm1"] = gain(D)
    p["l1_rwkv_mu"] = unif((RWKV_SHIFT_MIXES, D), 0.0, 1.0)
    p["l1_rwkv_w_rkv"] = dense(D, (3, D, D))
    p["l1_rwkv_w0"] = unif((D,), -6.0, -1.0)
    p["l1_rwkv_w1"] = dense(D, (D, RWKV_DECAY_RANK))
    p["l1_rwkv_w2"] = nrm((RWKV_DECAY_RANK, D), 0.1 * RWKV_DECAY_RANK ** -0.5)
    p["l1_rwkv_a0"] = nrm((D,), 0.1)
    p["l1_rwkv_a1"] = dense(D, (D, RWKV_A_RANK))
    p["l1_rwkv_a2"] = dense(RWKV_A_RANK, (RWKV_A_RANK, D))
    p["l1_rwkv_g1"] = dense(D, (D, RWKV_GATE_RANK))
    p["l1_rwkv_g2"] = dense(RWKV_GATE_RANK, (RWKV_GATE_RANK, D))
    p["l1_rwkv_k_k"] = 0.85 + nrm((D,), 0.05)
    p["l1_rwkv_k_a"] = 1.0 + nrm((D,), 0.05)
    p["l1_rwkv_r_k"] = nrm((D,), 0.1)
    p["l1_rwkv_gn_g"] = gain(D)
    p["l1_rwkv_gn_b"] = nrm((D,), 0.02)
    p["l1_rwkv_w_out"] = dense(D, (D, D))
    ffn_params(1)
    p["l2_norm1"] = gain(D)
    p["l2_sb_w_qkv"] = dense(D, (D, 3 * D))
    p["l2_sb_w_out"] = dense(D, (D, D))
    ffn_params(2)
    p["l3_norm1"] = gain(D)
    p["l3_ml_w_in"] = dense(D, (D, ML_IN))
    p["l3_ml_b_if"] = jnp.concatenate([nrm((ML_HEADS,), 0.1), unif((ML_HEADS,), 3.0, 6.0)])
    p["l3_ml_out_norm"] = gain(ML_DV)
    p["l3_ml_w_out"] = dense(ML_DV, (ML_DV, D))
    ffn_params(3)
    p["final_norm"] = gain(D)
    return p


def reference(x,
              l0_norm1, l0_gla_w_in, l0_gla_w_alpha_up, l0_gla_b_alpha, l0_gla_out_norm, l0_gla_w_out,
              l0_norm2, l0_ffn_w_up, l0_ffn_conv_w, l0_ffn_conv_b, l0_ffn_w_down,
              l1_norm1, l1_rwkv_mu, l1_rwkv_w_rkv, l1_rwkv_w0, l1_rwkv_w1, l1_rwkv_w2,
              l1_rwkv_a0, l1_rwkv_a1, l1_rwkv_a2, l1_rwkv_g1, l1_rwkv_g2,
              l1_rwkv_k_k, l1_rwkv_k_a, l1_rwkv_r_k, l1_rwkv_gn_g, l1_rwkv_gn_b, l1_rwkv_w_out,
              l1_norm2, l1_ffn_w_up, l1_ffn_conv_w, l1_ffn_conv_b, l1_ffn_w_down,
              l2_norm1, l2_sb_w_qkv, l2_sb_w_out,
              l2_norm2, l2_ffn_w_up, l2_ffn_conv_w, l2_ffn_conv_b, l2_ffn_w_down,
              l3_norm1, l3_ml_w_in, l3_ml_b_if, l3_ml_out_norm, l3_ml_w_out,
              l3_norm2, l3_ffn_w_up, l3_ffn_conv_w, l3_ffn_conv_b, l3_ffn_w_down,
              final_norm):
    mixers = (
        lambda h: gla_mixer(h, l0_gla_w_in, l0_gla_w_alpha_up, l0_gla_b_alpha, l0_gla_out_norm, l0_gla_w_out),
        lambda h: rwkv7_mixer(h, l1_rwkv_mu, l1_rwkv_w_rkv, l1_rwkv_w0, l1_rwkv_w1, l1_rwkv_w2,
                              l1_rwkv_a0, l1_rwkv_a1, l1_rwkv_a2, l1_rwkv_g1, l1_rwkv_g2,
                              l1_rwkv_k_k, l1_rwkv_k_a, l1_rwkv_r_k, l1_rwkv_gn_g, l1_rwkv_gn_b,
                              l1_rwkv_w_out),
        lambda h: stick_breaking_mixer(h, l2_sb_w_qkv, l2_sb_w_out),
        lambda h: mlstm_mixer(h, l3_ml_w_in, l3_ml_b_if, l3_ml_out_norm, l3_ml_w_out),
    )
    norm1 = (l0_norm1, l1_norm1, l2_norm1, l3_norm1)
    norm2 = (l0_norm2, l1_norm2, l2_norm2, l3_norm2)
    ffns = (
        (l0_ffn_w_up, l0_ffn_conv_w, l0_ffn_conv_b, l0_ffn_w_down),
        (l1_ffn_w_up, l1_ffn_conv_w, l1_ffn_conv_b, l1_ffn_w_down),
        (l2_ffn_w_up, l2_ffn_conv_w, l2_ffn_conv_b, l2_ffn_w_down),
        (l3_ffn_w_up, l3_ffn_conv_w, l3_ffn_conv_b, l3_ffn_w_down),
    )
    for layer in range(DEPTH):
        x = x + mixers[layer % N_MIXERS](rms_norm(x, norm1[layer]))
        x = x + conv_ffn(rms_norm(x, norm2[layer]), *ffns[layer])
    return rms_norm(x, final_norm)
```

```python
import functools

import jax
import jax.numpy as jnp
from jax import lax
from jax.experimental import pallas as pl
from jax.experimental.pallas import tpu as pltpu

F32 = jnp.float32
BF16 = jnp.bfloat16
HIGHEST = lax.Precision.HIGHEST

NORM_EPS = 1e-6
CHUNK = 64
GLA_HEADS, GLA_DK, GLA_DV, GLA_TAU = 4, 128, 256, 16.0
ML_HEADS, ML_DK, ML_DV = 4, 128, 256
RWKV_HEAD = 64
RWKV_GN_EPS = 64e-5
SB_HEADS, SB_HEAD_DIM = 16, 64
CONV_WIDTH = 3
LANES = 128
SB_ZERO_LOG = -104.0


def _rms(x, g, eps=NORM_EPS):
    return x * lax.rsqrt(jnp.mean(x * x, axis=-1, keepdims=True) + eps) * g


def _softplus(x):
    return jnp.maximum(x, 0.0) + jnp.log1p(jnp.exp(-jnp.abs(x)))


def _log_sigmoid(x):
    return -_softplus(-x)


def _sigmoid(x):
    return 1.0 / (1.0 + jnp.exp(-x))


def _dot(a, b, **kw):
    return jnp.dot(a, b, preferred_element_type=F32, **kw)


def _dot_nt(a, b, **kw):
    return lax.dot_general(a, b, (((1,), (1,)), ((), ())), preferred_element_type=F32, **kw)


def _dot_tn(a, b, **kw):
    return lax.dot_general(a, b, (((0,), (0,)), ((), ())), preferred_element_type=F32, **kw)


def _bdot(a, b):
    return _dot(a.astype(BF16), b.astype(BF16))


def _bdot_nt(a, b):
    return _dot_nt(a.astype(BF16), b.astype(BF16))


def _bdot_tn(a, b):
    return _dot_tn(a.astype(BF16), b.astype(BF16))


def _tri_masks(n):
    row = lax.broadcasted_iota(jnp.int32, (n, n), 0)
    col = lax.broadcasted_iota(jnp.int32, (n, n), 1)
    return row, col


def _full(shape):
    return pl.BlockSpec(shape, lambda *_: (0,) * len(shape))


def _params(sem):
    return pltpu.CompilerParams(dimension_semantics=sem, vmem_limit_bytes=56 * 1024 * 1024)


def _row(v, width=None):
    v = v.reshape(1, -1).astype(F32)
    if width is not None and v.shape[1] < width:
        v = jnp.pad(v, ((0, 0), (0, width - v.shape[1])))
    return v


def _ffn_kernel(x_ref, g_ref, wg_ref, wu_ref, cg_ref, cu_ref, wd_ref, fg_ref, o_ref,
                h_ref, acc_ref, carg_ref, caru_ref, *, final_norm):
    tm, fc = x_ref.shape[0], wg_ref.shape[2]
    nc = wg_ref.shape[0]

    @pl.when(pl.program_id(0) == 0)
    def _():
        carg_ref[...] = jnp.zeros_like(carg_ref)
        caru_ref[...] = jnp.zeros_like(caru_ref)

    x = x_ref[...]
    h_ref[...] = _rms(x, g_ref[...]).astype(BF16)
    acc_ref[...] = jnp.zeros_like(acc_ref)
    rowi = lax.broadcasted_iota(jnp.int32, (tm, fc), 0)

    def conv(u, car_ref, cw, c):
        car = car_ref[c]
        c2, c1 = car[6:7, :], car[7:8, :]
        s1 = jnp.where(rowi == 0, c1, pltpu.roll(u, 1, 0))
        s2 = jnp.where(rowi == 0, c2, jnp.where(rowi == 1, c1, pltpu.roll(u, 2, 0)))
        car_ref[c] = u[tm - 8:, :]
        return s2 * cw[0:1, :] + s1 * cw[1:2, :] + u * cw[2:3, :] + cw[3:4, :]

    def body(c, carry):
        h = h_ref[...]
        gate = conv(_dot(h, wg_ref[c]), carg_ref, cg_ref[c], c)
        up = conv(_dot(h, wu_ref[c]), caru_ref, cu_ref[c], c)
        act = gate * _sigmoid(gate) * up
        acc_ref[...] += _dot(act.astype(BF16), wd_ref[c])
        return carry

    lax.fori_loop(0, nc, body, 0)
    y = x + acc_ref[...]
    if final_norm:
        y = _rms(y, fg_ref[...])
    o_ref[...] = y


def _ffn(x, norm_g, w_up, conv_w, conv_b, w_down, final_g=None, tm=512, fc=256):
    s, d = x.shape
    f = w_down.shape[0]
    nc = f // fc

    def split(w):
        return w.reshape(d, nc, fc).transpose(1, 0, 2).astype(BF16)

    def taps(cw, cb):
        t = jnp.concatenate([cw, cb[None, :], jnp.zeros((4, f), F32)], axis=0)
        return t.reshape(8, nc, fc).transpose(1, 0, 2)

    wg, wu = split(w_up[:, :f]), split(w_up[:, f:])
    cg, cu = taps(conv_w[:, :f], conv_b[:f]), taps(conv_w[:, f:], conv_b[f:])
    wd = w_down.reshape(nc, fc, d).astype(BF16)
    fg = _row(final_g if final_g is not None else jnp.ones((d,), F32))
    return pl.pallas_call(
        functools.partial(_ffn_kernel, final_norm=final_g is not None),
        grid=(s // tm,),
        in_specs=[pl.BlockSpec((tm, d), lambda i: (i, 0)), _full((1, d)),
                  _full((nc, d, fc)), _full((nc, d, fc)), _full((nc, 8, fc)), _full((nc, 8, fc)),
                  _full((nc, fc, d)), _full((1, d))],
        out_specs=pl.BlockSpec((tm, d), lambda i: (i, 0)),
        out_shape=jax.ShapeDtypeStruct((s, d), F32),
        scratch_shapes=[pltpu.VMEM((tm, d), BF16), pltpu.VMEM((tm, d), F32),
                        pltpu.VMEM((nc, 8, fc), F32), pltpu.VMEM((nc, 8, fc), F32)],
        compiler_params=_params(("arbitrary",)),
        name="conv_ffn",
    )(x, _row(norm_g), wg, wu, cg, cu, wd, fg)


def _gla_kernel(x_ref, n1_ref, wm_ref, wa_ref, wup_ref, ba_ref, on_ref, wo_ref, o_ref,
                proj_ref, la_ref, ob_ref, st_ref):
    tb = x_ref.shape[0]
    hk, hv = GLA_HEADS * GLA_DK, GLA_HEADS * GLA_DV

    @pl.when(pl.program_id(0) == 0)
    def _():
        st_ref[...] = jnp.zeros_like(st_ref)

    x = x_ref[...]
    h = _rms(x, n1_ref[...]).astype(BF16)
    proj_ref[...] = _dot(h, wm_ref[...])
    a_low = _dot(h, wa_ref[...])
    la_ref[...] = _log_sigmoid(_bdot(a_low, wup_ref[...]) + ba_ref[...]) * (1.0 / GLA_TAU)
    row, col = _tri_masks(CHUNK)
    causal = row >= col
    tri = causal.astype(F32)
    scale = GLA_DK ** -0.5

    def chunk(c, carry):
        rows = pl.ds(pl.multiple_of(c * CHUNK, CHUNK), CHUNK)
        b = _dot(tri, la_ref[rows, :], precision=HIGHEST)
        b_last = b[CHUNK - 1:CHUNK, :]
        e_pos, e_neg = jnp.exp(b), jnp.exp(-b)
        e_end, e_last = jnp.exp(b_last - b), jnp.exp(b_last)
        for hh in range(GLA_HEADS):
            ks = slice(hh * GLA_DK, (hh + 1) * GLA_DK)
            q = proj_ref[rows, hh * GLA_DK:(hh + 1) * GLA_DK] * scale
            k = proj_ref[rows, hk + hh * GLA_DK:hk + (hh + 1) * GLA_DK]
            v = proj_ref[rows, 2 * hk + hh * GLA_DV:2 * hk + (hh + 1) * GLA_DV]
            q_dec = q * e_pos[:, ks]
            k_inv = k * e_neg[:, ks]
            scores = jnp.where(causal, _bdot_nt(q_dec, k_inv), 0.0)
            st = st_ref[hh]
            ob_ref[rows, hh * GLA_DV:(hh + 1) * GLA_DV] = _bdot(scores, v) + _bdot_nt(q_dec, st)
            k_end = k * e_end[:, ks]
            st_ref[hh] = st * e_last[:, ks] + _bdot_tn(v, k_end)
        return carry

    lax.fori_loop(0, tb // CHUNK, chunk, 0)
    normed = [_rms(ob_ref[:, hh * GLA_DV:(hh + 1) * GLA_DV], on_ref[:, hh * GLA_DV:(hh + 1) * GLA_DV])
              for hh in range(GLA_HEADS)]
    r = proj_ref[:, 2 * hk + hv:2 * hk + 2 * hv]
    y = jnp.concatenate(normed, axis=-1) * (r * _sigmoid(r))
    o_ref[...] = x + _bdot(y, wo_ref[...])


def _gla(x, n1, w_in, w_alpha_up, b_alpha, out_norm, w_out, tb=256):
    s, d = x.shape
    hk, hv = GLA_HEADS * GLA_DK, GLA_HEADS * GLA_DV
    nm = 2 * hk + 2 * hv
    rank = w_alpha_up.shape[0]
    wm = w_in[:, :nm].astype(BF16)
    wa = jnp.pad(w_in[:, nm:], ((0, 0), (0, LANES - rank))).astype(BF16)
    wup = jnp.pad(w_alpha_up, ((0, LANES - rank), (0, 0))).astype(BF16)
    return pl.pallas_call(
        _gla_kernel,
        grid=(s // tb,),
        in_specs=[pl.BlockSpec((tb, d), lambda i: (i, 0)), _full((1, d)), _full((d, nm)),
                  _full((d, LANES)), _full((LANES, hk)), _full((1, hk)), _full((1, hv)), _full((hv, d))],
        out_specs=pl.BlockSpec((tb, d), lambda i: (i, 0)),
        out_shape=jax.ShapeDtypeStruct((s, d), F32),
        scratch_shapes=[pltpu.VMEM((tb, nm), F32), pltpu.VMEM((tb, hk), F32), pltpu.VMEM((tb, hv), F32),
                        pltpu.VMEM((GLA_HEADS, GLA_DV, GLA_DK), F32)],
        compiler_params=_params(("arbitrary",)),
        name="gla_mixer",
    )(x, _row(n1), wm, wa, wup, _row(b_alpha), _row(out_norm), w_out.astype(BF16))


def _mlstm_kernel(x_ref, n1_ref, wm_ref, wif_ref, wift_ref, bif_ref, bift_ref, on_ref, wo_ref, o_ref,
                  h_ref, proj_ref, gc_ref, ob_ref, c_ref, n_ref, m_ref):
    tb = x_ref.shape[0]
    nh = ML_HEADS
    hk, hv = nh * ML_DK, nh * ML_DV

    @pl.when(pl.program_id(0) == 0)
    def _():
        c_ref[...] = jnp.zeros_like(c_ref)
        n_ref[...] = jnp.zeros_like(n_ref)
        m_ref[...] = jnp.zeros_like(m_ref)

    x = x_ref[...]
    h = _rms(x, n1_ref[...]).astype(BF16)
    h_ref[...] = h
    proj_ref[...] = _dot(h, wm_ref[...])
    gc_ref[...] = _dot(h, wif_ref[...]) + bif_ref[...]
    row, col = _tri_masks(CHUNK)
    causal = row >= col
    tri = causal.astype(F32)
    scale = ML_DK ** -0.5

    def chunk(c, carry):
        rows = pl.ds(pl.multiple_of(c * CHUNK, CHUNK), CHUNK)
        g_col = gc_ref[rows, :]
        g_row = _dot_nt(wift_ref[...], h_ref[rows, :]) + bift_ref[...]
        b_col = _dot(tri, _log_sigmoid(g_col), precision=HIGHEST)
        b_row = _dot_nt(_log_sigmoid(g_row), tri, precision=HIGHEST)
        for hh in range(nh):
            bc, ic = b_col[:, nh + hh:nh + hh + 1], g_col[:, hh:hh + 1]
            br, ir = b_row[nh + hh:nh + hh + 1, :], g_row[hh:hh + 1, :]
            q = proj_ref[rows, hh * ML_DK:(hh + 1) * ML_DK]
            k = proj_ref[rows, hk + hh * ML_DK:hk + (hh + 1) * ML_DK] * scale
            v = proj_ref[rows, 2 * hk + hh * ML_DV:2 * hk + (hh + 1) * ML_DV]
            m_prev = m_ref[0:1, hh:hh + 1]
            d_log = jnp.where(causal, bc - br + ir, -jnp.inf)
            inter = bc + m_prev
            m_t = jnp.maximum(inter, jnp.max(d_log, axis=-1, keepdims=True))
            w_intra = jnp.exp(d_log - m_t)
            sc = jnp.exp(inter - m_t)
            qk = _bdot_nt(q, k) * w_intra
            c_st, n_st = c_ref[hh], n_ref[hh]
            num = _bdot(qk, v) + sc * _bdot(q, c_st)
            den = jnp.sum(qk, axis=-1, keepdims=True) + sc * jnp.sum(q * n_st, axis=-1, keepdims=True)
            ob_ref[rows, hh * ML_DV:(hh + 1) * ML_DV] = num / jnp.maximum(jnp.abs(den), jnp.exp(-m_t))
            b_last = bc[CHUNK - 1:CHUNK, :]
            g_end = b_last - bc + ic
            m_new = jnp.maximum(b_last + m_prev, jnp.max(g_end, axis=0, keepdims=True))
            kw = k * jnp.exp(g_end - m_new)
            carry_scale = jnp.exp(b_last + m_prev - m_new)
            c_ref[hh] = carry_scale * c_st + _bdot_tn(kw, v)
            n_ref[hh] = carry_scale * n_st + jnp.sum(kw, axis=0, keepdims=True)
            m_ref[0:1, hh:hh + 1] = m_new
        return carry

    lax.fori_loop(0, tb // CHUNK, chunk, 0)
    normed = [_rms(ob_ref[:, hh * ML_DV:(hh + 1) * ML_DV], on_ref[:, hh * ML_DV:(hh + 1) * ML_DV])
              for hh in range(nh)]
    o_pre = proj_ref[:, 2 * hk + hv:2 * hk + 2 * hv]
    y = jnp.concatenate(normed, axis=-1) * _sigmoid(o_pre)
    o_ref[...] = x + _bdot(y, wo_ref[...])


def _mlstm(x, n1, w_in, b_if, out_norm, w_out, tb=256):
    s, d = x.shape
    nh = ML_HEADS
    hk, hv = nh * ML_DK, nh * ML_DV
    nm = 2 * hk + 2 * hv
    wm = w_in[:, :nm].astype(BF16)
    w_if = w_in[:, nm:]
    wif = jnp.pad(w_if, ((0, 0), (0, LANES - 2 * nh))).astype(BF16)
    wift = w_if.T.astype(BF16)
    return pl.pallas_call(
        _mlstm_kernel,
        grid=(s // tb,),
        in_specs=[pl.BlockSpec((tb, d), lambda i: (i, 0)), _full((1, d)), _full((d, nm)),
                  _full((d, LANES)), _full((2 * nh, d)), _full((1, LANES)), _full((2 * nh, 1)),
                  _full((1, hv)), _full((hv, d))],
        out_specs=pl.BlockSpec((tb, d), lambda i: (i, 0)),
        out_shape=jax.ShapeDtypeStruct((s, d), F32),
        scratch_shapes=[pltpu.VMEM((tb, d), BF16), pltpu.VMEM((tb, nm), F32), pltpu.VMEM((tb, LANES), F32),
                        pltpu.VMEM((tb, hv), F32), pltpu.VMEM((nh, ML_DK, ML_DV), F32),
                        pltpu.VMEM((nh, 1, ML_DK), F32), pltpu.VMEM((8, LANES), F32)],
        compiler_params=_params(("arbitrary",)),
        name="mlstm_mixer",
    )(x, _row(n1), wm, wif, wift, _row(b_if, LANES), b_if.reshape(2 * nh, 1).astype(F32),
      _row(out_norm), w_out.astype(BF16))


def _rwkv_proj_kernel(x_ref, n1_ref, mu_ref, wr_ref, wk_ref, wv_ref, w0_ref, w1_ref, w2_ref,
                      a0_ref, a1_ref, a2_ref, g1_ref, g2_ref, kk_ref, ka_ref,
                      r_ref, lw_ref, k_ref, v_ref, kkraw_ref, a_ref, g_ref, car_ref):
    tm = x_ref.shape[0]

    @pl.when(pl.program_id(0) == 0)
    def _():
        car_ref[...] = jnp.zeros_like(car_ref)

    h = _rms(x_ref[...], n1_ref[...])
    rowi = lax.broadcasted_iota(jnp.int32, h.shape, 0)
    h_prev = jnp.where(rowi == 0, car_ref[7:8, :], pltpu.roll(h, 1, 0))
    car_ref[...] = h[tm - 8:, :]
    xx = h_prev - h
    xr, xw, xk, xv, xa, xg = (h + xx * mu_ref[j:j + 1, :] for j in range(6))
    r = _bdot(xr, wr_ref[...])
    k = _bdot(xk, wk_ref[...])
    v = _bdot(xv, wv_ref[...])
    w_log = -_softplus(-(w0_ref[...] + _bdot(jnp.tanh(_bdot(xw, w1_ref[...])), w2_ref[...]))) - 0.5
    a = _sigmoid(a0_ref[...] + _bdot(_bdot(xa, a1_ref[...]), a2_ref[...]))
    g = _bdot(_sigmoid(_bdot(xg, g1_ref[...])), g2_ref[...])
    r_ref[...] = r
    lw_ref[...] = -jnp.exp(w_log)
    k_ref[...] = k * (1.0 + (a - 1.0) * ka_ref[...])
    v_ref[...] = v
    kkraw_ref[...] = k * kk_ref[...]
    a_ref[...] = a
    g_ref[...] = g


def _rwkv_proj(x, n1, mu, w_rkv, w0, w1, w2, a0, a1, a2, g1, g2, k_k, k_a, tm=256):
    s, d = x.shape

    def padc(w):
        return jnp.pad(w, ((0, 0), (0, LANES - w.shape[1]))).astype(BF16) if w.shape[1] < LANES else w.astype(BF16)

    def padr(w):
        return jnp.pad(w, ((0, LANES - w.shape[0]), (0, 0))).astype(BF16) if w.shape[0] < LANES else w.astype(BF16)

    mu8 = jnp.pad(mu, ((0, 8 - mu.shape[0]), (0, 0)))
    blk = pl.BlockSpec((tm, d), lambda i: (i, 0))
    out = jax.ShapeDtypeStruct((s, d), F32)
    return pl.pallas_call(
        _rwkv_proj_kernel,
        grid=(s // tm,),
        in_specs=[blk, _full((1, d)), _full((8, d)), _full((d, d)), _full((d, d)), _full((d, d)),
                  _full((1, d)), _full((d, LANES)), _full((LANES, d)),
                  _full((1, d)), _full((d, LANES)), _full((LANES, d)),
                  _full((d, LANES)), _full((LANES, d)), _full((1, d)), _full((1, d))],
        out_specs=[blk] * 7,
        out_shape=[out] * 7,
        scratch_shapes=[pltpu.VMEM((8, d), F32)],
        compiler_params=_params(("arbitrary",)),
        name="rwkv_proj",
    )(x, _row(n1), mu8, w_rkv[0].astype(BF16), w_rkv[1].astype(BF16), w_rkv[2].astype(BF16),
      _row(w0), padc(w1), padr(w2), _row(a0), padc(a1), padr(a2), padc(g1), padr(g2), _row(k_k), _row(k_a))


def _unit_lower_inverse(a_strict, row, col):
    n = a_strict.shape[0]
    eye = (row == col).astype(F32)
    blk16 = (row // 16) == (col // 16)
    a_d = jnp.where(blk16, a_strict, 0.0)
    inv = eye + a_d
    p = a_d
    for _ in range(3):
        p = _dot(p, p, precision=HIGHEST)
        inv = inv + _dot(inv, p, precision=HIGHEST)
    size = 32
    prev = blk16
    while size <= n:
        cur = (row // size) == (col // size)
        off = jnp.where(cur & ~prev, a_strict, 0.0)
        inv = inv + _dot(_dot(inv, off, precision=HIGHEST), inv, precision=HIGHEST)
        prev = cur
        size *= 2
    return inv


def _rwkv_scan_kernel(r_ref, lw_ref, k_ref, v_ref, kkraw_ref, a_ref, rk_ref, gg_ref, gb_ref, o_ref, s_ref):
    tb = r_ref.shape[0]
    n = RWKV_HEAD

    @pl.when(pl.program_id(1) == 0)
    def _():
        s_ref[...] = jnp.zeros_like(s_ref)

    row, col = _tri_masks(CHUNK)
    incl = row >= col
    strict = row > col
    tri = incl.astype(F32)

    def chunk(c, carry):
        rows = pl.ds(pl.multiple_of(c * CHUNK, CHUNK), CHUNK)
        for hh in range(LANES // n):
            ls = slice(hh * n, (hh + 1) * n)
            r, lw, k, v = r_ref[rows, ls], lw_ref[rows, ls], k_ref[rows, ls], v_ref[rows, ls]
            kk, a = kkraw_ref[rows, ls], a_ref[rows, ls]
            kk = kk / jnp.maximum(jnp.sqrt(jnp.sum(kk * kk, axis=-1, keepdims=True)), 1e-12)
            cum = _dot(tri, lw, precision=HIGHEST)
            cum_last = cum[CHUNK - 1:CHUNK, :]
            e_neg = jnp.exp(-cum)
            e_end = jnp.exp(cum_last - cum)
            b_vec = kk * a
            a_t = -kk * jnp.exp(cum - lw)
            r_t = r * jnp.exp(cum)
            b_t, k_t = b_vec * e_neg, k * e_neg
            a4 = _dot_nt(jnp.concatenate([a_t, r_t], axis=0), jnp.concatenate([b_t, k_t], axis=0),
                         precision=HIGHEST)
            a_ab = jnp.where(strict, a4[:CHUNK, :CHUNK], 0.0)
            a_ak = jnp.where(strict, a4[:CHUNK, CHUNK:], 0.0)
            a_rb = jnp.where(incl, a4[CHUNK:, :CHUNK], 0.0)
            a_rk = jnp.where(incl, a4[CHUNK:, CHUNK:], 0.0)
            t_inv = _unit_lower_inverse(a_ab, row, col)
            s0 = s_ref[hh]
            rhs = _dot_nt(a_t, s0, precision=HIGHEST) + _dot(a_ak, v, precision=HIGHEST)
            u = _dot(t_inv, rhs, precision=HIGHEST)
            y = (_dot_nt(r_t, s0, precision=HIGHEST) + _dot(a_rb, u, precision=HIGHEST)
                 + _dot(a_rk, v, precision=HIGHEST))
            s_ref[hh] = (s0 * jnp.exp(cum_last) + _dot_tn(u, b_vec * e_end, precision=HIGHEST)
                         + _dot_tn(v, k * e_end, precision=HIGHEST))
            mean = jnp.mean(y, axis=-1, keepdims=True)
            var = jnp.mean(jnp.square(y - mean), axis=-1, keepdims=True)
            yn = (y - mean) * lax.rsqrt(var + RWKV_GN_EPS) * gg_ref[:, ls] + gb_ref[:, ls]
            bonus = jnp.sum(r * k * rk_ref[:, ls], axis=-1, keepdims=True) * v
            o_ref[rows, ls] = yn + bonus
        return carry

    lax.fori_loop(0, tb // CHUNK, chunk, 0)


def _rwkv_scan(r, lw, k, v, kkraw, a, r_k, gn_g, gn_b, tb=256):
    s, d = r.shape
    blk = pl.BlockSpec((tb, LANES), lambda p, i: (i, p))
    vec = pl.BlockSpec((1, LANES), lambda p, i: (0, p))
    return pl.pallas_call(
        _rwkv_scan_kernel,
        grid=(d // LANES, s // tb),
        in_specs=[blk] * 6 + [vec] * 3,
        out_specs=blk,
        out_shape=jax.ShapeDtypeStruct((s, d), F32),
        scratch_shapes=[pltpu.VMEM((LANES // RWKV_HEAD, RWKV_HEAD, RWKV_HEAD), F32)],
        compiler_params=_params(("arbitrary", "arbitrary")),
        name="rwkv_scan",
    )(r, lw, k, v, kkraw, a, _row(r_k), _row(gn_g), _row(gn_b))


def _out_proj_kernel(x_ref, a_ref, g_ref, w_ref, o_ref, *, gated):
    a = a_ref[...]
    if gated:
        a = a * g_ref[...]
    o_ref[...] = x_ref[...] + _bdot(a, w_ref[...])


def _out_proj(x, a, w, gate=None, tm=512):
    s, d = x.shape
    blk = pl.BlockSpec((tm, d), lambda i: (i, 0))
    return pl.pallas_call(
        functools.partial(_out_proj_kernel, gated=gate is not None),
        grid=(s // tm,),
        in_specs=[blk, blk, blk, _full((d, d))],
        out_specs=blk,
        out_shape=jax.ShapeDtypeStruct((s, d), F32),
        compiler_params=_params(("arbitrary",)),
        name="out_proj",
    )(x, a, gate if gate is not None else a, w.astype(BF16))


def _sb_qkv_kernel(x_ref, n1_ref, w_ref, q_ref, k_ref, v_ref):
    d = x_ref.shape[1]
    h = _rms(x_ref[...], n1_ref[...]).astype(BF16)
    qkv = _dot(h, w_ref[...])
    q_ref[...] = (qkv[:, :d] * SB_HEAD_DIM ** -0.5).astype(BF16)
    k_ref[...] = qkv[:, d:2 * d].astype(BF16)
    v_ref[...] = qkv[:, 2 * d:].astype(BF16)


def _sb_qkv(x, n1, w_qkv, tm=512):
    s, d = x.shape
    blk = pl.BlockSpec((tm, d), lambda i: (i, 0))
    out = jax.ShapeDtypeStruct((s, d), BF16)
    return pl.pallas_call(
        _sb_qkv_kernel,
        grid=(s // tm,),
        in_specs=[blk, _full((1, d)), _full((d, 3 * d))],
        out_specs=[blk] * 3,
        out_shape=[out] * 3,
        compiler_params=_params(("arbitrary",)),
        name="sb_qkv",
    )(x, _row(n1), w_qkv.astype(BF16))


def _sb_attn_kernel(q_ref, k_ref, v_ref, o_ref, *, tq):
    qi = pl.program_id(1)
    q = q_ref[0]
    row, col = _tri_masks(tq)
    before = col < row
    later = (row > col).astype(BF16)

    def tile(kb, c, acc, diagonal):
        ks = pl.ds(pl.multiple_of(kb * tq, tq), tq)
        z = _dot_nt(q, k_ref[0, ks, :])
        lk = -_softplus(z)
        if diagonal:
            lk = jnp.where(before, lk, 0.0)
        hi = lk.astype(BF16)
        lo = (lk - hi.astype(F32)).astype(BF16)
        excl = _dot(hi, later) + _dot(lo, later)
        w = jnp.exp(z + lk + excl + c)
        if diagonal:
            w = jnp.where(before, w, 0.0)
        acc = acc + _dot(w.astype(BF16), v_ref[0, ks, :])
        return c + excl[:, 0:1] + lk[:, 0:1], acc

    c0 = jnp.zeros((tq, 1), F32)
    acc0 = jnp.zeros((tq, SB_HEAD_DIM), F32)
    c, acc = tile(qi, c0, acc0, True)

    def cond(st):
        kb, c, _ = st
        return jnp.logical_and(kb >= 0, jnp.max(c) > SB_ZERO_LOG)

    def body(st):
        kb, c, acc = st
        c, acc = tile(kb, c, acc, False)
        return kb - 1, c, acc

    _, _, acc = lax.while_loop(cond, body, (qi - 1, c, acc))
    o_ref[0] = acc


def _sb_attn(q, k, v, tq=256):
    nh, s, dh = q.shape
    return pl.pallas_call(
        functools.partial(_sb_attn_kernel, tq=tq),
        grid=(nh, s // tq),
        in_specs=[pl.BlockSpec((1, tq, dh), lambda h, i: (h, i, 0)),
                  pl.BlockSpec((1, s, dh), lambda h, i: (h, 0, 0)),
                  pl.BlockSpec((1, s, dh), lambda h, i: (h, 0, 0))],
        out_specs=pl.BlockSpec((1, tq, dh), lambda h, i: (h, i, 0)),
        out_shape=jax.ShapeDtypeStruct((nh, s, dh), F32),
        compiler_params=_params(("arbitrary", "arbitrary")),
        name="sb_attn",
    )(q, k, v)


def _stick_breaking(x, n1, w_qkv, w_out):
    s, d = x.shape
    q, k, v = _sb_qkv(x, n1, w_qkv)
    heads = lambda t: t.reshape(s, SB_HEADS, SB_HEAD_DIM).transpose(1, 0, 2)
    o = _sb_attn(heads(q), heads(k), heads(v))
    return _out_proj(x, o.transpose(1, 0, 2).reshape(s, d), w_out)


def _rwkv7(x, n1, mu, w_rkv, w0, w1, w2, a0, a1, a2, g1, g2, k_k, k_a, r_k, gn_g, gn_b, w_out):
    r, lw, k, v, kkraw, a, g = _rwkv_proj(x, n1, mu, w_rkv, w0, w1, w2, a0, a1, a2, g1, g2, k_k, k_a)
    o = _rwkv_scan(r, lw, k, v, kkraw, a, r_k, gn_g, gn_b)
    return _out_proj(x, o, w_out, gate=g)


def kernel(x, l0_norm1, l0_gla_w_in, l0_gla_w_alpha_up, l0_gla_b_alpha, l0_gla_out_norm, l0_gla_w_out, l0_norm2, l0_ffn_w_up, l0_ffn_conv_w, l0_ffn_conv_b, l0_ffn_w_down, l1_norm1, l1_rwkv_mu, l1_rwkv_w_rkv, l1_rwkv_w0, l1_rwkv_w1, l1_rwkv_w2, l1_rwkv_a0, l1_rwkv_a1, l1_rwkv_a2, l1_rwkv_g1, l1_rwkv_g2, l1_rwkv_k_k, l1_rwkv_k_a, l1_rwkv_r_k, l1_rwkv_gn_g, l1_rwkv_gn_b, l1_rwkv_w_out, l1_norm2, l1_ffn_w_up, l1_ffn_conv_w, l1_ffn_conv_b, l1_ffn_w_down, l2_norm1, l2_sb_w_qkv, l2_sb_w_out, l2_norm2, l2_ffn_w_up, l2_ffn_conv_w, l2_ffn_conv_b, l2_ffn_w_down, l3_norm1, l3_ml_w_in, l3_ml_b_if, l3_ml_out_norm, l3_ml_w_out, l3_norm2, l3_ffn_w_up, l3_ffn_conv_w, l3_ffn_conv_b, l3_ffn_w_down, final_norm):
    b, s, d = x.shape
    outs = []
    for bi in range(b):
        h = x[bi]
        h = _gla(h, l0_norm1, l0_gla_w_in, l0_gla_w_alpha_up, l0_gla_b_alpha, l0_gla_out_norm, l0_gla_w_out)
        h = _ffn(h, l0_norm2, l0_ffn_w_up, l0_ffn_conv_w, l0_ffn_conv_b, l0_ffn_w_down)
        h = _rwkv7(h, l1_norm1, l1_rwkv_mu, l1_rwkv_w_rkv, l1_rwkv_w0, l1_rwkv_w1, l1_rwkv_w2,
                   l1_rwkv_a0, l1_rwkv_a1, l1_rwkv_a2, l1_rwkv_g1, l1_rwkv_g2,
                   l1_rwkv_k_k, l1_rwkv_k_a, l1_rwkv_r_k, l1_rwkv_gn_g, l1_rwkv_gn_b, l1_rwkv_w_out)
        h = _ffn(h, l1_norm2, l1_ffn_w_up, l1_ffn_conv_w, l1_ffn_conv_b, l1_ffn_w_down)
        h = _stick_breaking(h, l2_norm1, l2_sb_w_qkv, l2_sb_w_out)
        h = _ffn(h, l2_norm2, l2_ffn_w_up, l2_ffn_conv_w, l2_ffn_conv_b, l2_ffn_w_down)
        h = _mlstm(h, l3_norm1, l3_ml_w_in, l3_ml_b_if, l3_ml_out_norm, l3_ml_w_out)
        h = _ffn(h, l3_norm2, l3_ffn_w_up, l3_ffn_conv_w, l3_ffn_conv_b, l3_ffn_w_down, final_g=final_norm)
        outs.append(h)
    return jnp.stack(outs, axis=0)
```

```python
import functools

import jax
import jax.numpy as jnp
from jax import lax
from jax.experimental import pallas as pl
from jax.experimental.pallas import tpu as pltpu

F32 = jnp.float32
BF16 = jnp.bfloat16
HIGHEST = lax.Precision.HIGHEST

NORM_EPS = 1e-6
CHUNK = 64
GLA_HEADS, GLA_DK, GLA_DV, GLA_TAU = 4, 128, 256, 16.0
ML_HEADS, ML_DK, ML_DV = 4, 128, 256
RWKV_HEAD = 64
RWKV_GN_EPS = 64e-5
SB_HEADS, SB_HEAD_DIM = 16, 64
CONV_WIDTH = 3
LANES = 128
SB_ZERO_LOG = -104.0


def _rms(x, g, eps=NORM_EPS):
    return x * lax.rsqrt(jnp.mean(x * x, axis=-1, keepdims=True) + eps) * g


def _softplus(x):
    return jnp.maximum(x, 0.0) + jnp.log1p(jnp.exp(-jnp.abs(x)))


def _log_sigmoid(x):
    return -_softplus(-x)


def _sigmoid(x):
    return 1.0 / (1.0 + jnp.exp(-x))


def _dot(a, b, **kw):
    return jnp.dot(a, b, preferred_element_type=F32, **kw)


def _dot_nt(a, b, **kw):
    return lax.dot_general(a, b, (((1,), (1,)), ((), ())), preferred_element_type=F32, **kw)


def _dot_tn(a, b, **kw):
    return lax.dot_general(a, b, (((0,), (0,)), ((), ())), preferred_element_type=F32, **kw)


def _bdot(a, b):
    return _dot(a.astype(BF16), b.astype(BF16))


def _bdot_nt(a, b):
    return _dot_nt(a.astype(BF16), b.astype(BF16))


def _bdot_tn(a, b):
    return _dot_tn(a.astype(BF16), b.astype(BF16))


def _tri_masks(n):
    row = lax.broadcasted_iota(jnp.int32, (n, n), 0)
    col = lax.broadcasted_iota(jnp.int32, (n, n), 1)
    return row, col


def _full(shape):
    return pl.BlockSpec(shape, lambda *_: (0,) * len(shape))


def _const(shape):
    return pl.BlockSpec(shape, lambda *_: (0,) * len(shape), pipeline_mode=pl.Buffered(1))


def _params(sem):
    return pltpu.CompilerParams(dimension_semantics=sem, vmem_limit_bytes=56 * 1024 * 1024)


def _row(v, width=None):
    v = v.reshape(1, -1).astype(F32)
    if width is not None and v.shape[1] < width:
        v = jnp.pad(v, ((0, 0), (0, width - v.shape[1])))
    return v


def _ffn_kernel(x_ref, g_ref, wup_ref, cw_ref, wd_ref, fg_ref, o_ref, h_ref, act_ref, car_ref, *, final_norm, fc):
    tm = x_ref.shape[0]
    f = wd_ref.shape[0]

    @pl.when(pl.program_id(0) == 0)
    def _():
        car_ref[...] = jnp.zeros_like(car_ref)

    x = x_ref[...]
    h_ref[...] = _rms(x, g_ref[...]).astype(BF16)
    rowi = lax.broadcasted_iota(jnp.int32, (tm, fc), 0)

    def conv_up(cols):
        u = _dot(h_ref[...], wup_ref[:, cols])
        c2, c1 = car_ref[6:7, cols], car_ref[7:8, cols]
        s1 = jnp.where(rowi == 0, c1, pltpu.roll(u, 1, 0))
        s2 = jnp.where(rowi == 0, c2, jnp.where(rowi == 1, c1, pltpu.roll(u, 2, 0)))
        car_ref[:, cols] = u[tm - 8:, :]
        return s2 * cw_ref[0:1, cols] + s1 * cw_ref[1:2, cols] + u * cw_ref[2:3, cols] + cw_ref[3:4, cols]

    for c in range(f // fc):
        gate = conv_up(slice(c * fc, (c + 1) * fc))
        up = conv_up(slice(f + c * fc, f + (c + 1) * fc))
        act_ref[:, c * fc:(c + 1) * fc] = (gate * _sigmoid(gate) * up).astype(BF16)
    y = x + _dot(act_ref[...], wd_ref[...])
    if final_norm:
        y = _rms(y, fg_ref[...])
    o_ref[...] = y


def _ffn(x, norm_g, w_up, conv_w, conv_b, w_down, final_g=None, tm=512, fc=256):
    s, d = x.shape
    f = w_down.shape[0]
    taps = jnp.concatenate([conv_w, conv_b[None, :], jnp.zeros((4, 2 * f), F32)], axis=0)
    fg = _row(final_g if final_g is not None else jnp.ones((d,), F32))
    return pl.pallas_call(
        functools.partial(_ffn_kernel, final_norm=final_g is not None, fc=fc),
        grid=(s // tm,),
        in_specs=[pl.BlockSpec((tm, d), lambda i: (i, 0)), _full((1, d)), _const((d, 2 * f)), _full((8, 2 * f)),
                  _const((f, d)), _full((1, d))],
        out_specs=pl.BlockSpec((tm, d), lambda i: (i, 0)),
        out_shape=jax.ShapeDtypeStruct((s, d), F32),
        scratch_shapes=[pltpu.VMEM((tm, d), BF16), pltpu.VMEM((tm, f), BF16), pltpu.VMEM((8, 2 * f), F32)],
        compiler_params=_params(("arbitrary",)),
        name="conv_ffn",
    )(x, _row(norm_g), w_up.astype(BF16), taps, w_down.astype(BF16), fg)


def _gla_kernel(x_ref, n1_ref, wm_ref, wa_ref, wup_ref, ba_ref, on_ref, wo_ref, o_ref,
                proj_ref, la_ref, ob_ref, st_ref):
    tb = x_ref.shape[0]
    hk, hv = GLA_HEADS * GLA_DK, GLA_HEADS * GLA_DV

    @pl.when(pl.program_id(0) == 0)
    def _():
        st_ref[...] = jnp.zeros_like(st_ref)

    x = x_ref[...]
    h = _rms(x, n1_ref[...]).astype(BF16)
    proj_ref[...] = _dot(h, wm_ref[...])
    a_low = _dot(h, wa_ref[...])
    la_ref[...] = _log_sigmoid(_bdot(a_low, wup_ref[...]) + ba_ref[...]) * (1.0 / GLA_TAU)
    row, col = _tri_masks(CHUNK)
    causal = row >= col
    tri = causal.astype(F32)
    scale = GLA_DK ** -0.5

    def chunk(c, carry):
        rows = pl.ds(pl.multiple_of(c * CHUNK, CHUNK), CHUNK)
        b = _dot(tri, la_ref[rows, :], precision=HIGHEST)
        b_last = b[CHUNK - 1:CHUNK, :]
        e_pos, e_neg = jnp.exp(b), jnp.exp(-b)
        e_end, e_last = jnp.exp(b_last - b), jnp.exp(b_last)
        for hh in range(GLA_HEADS):
            ks = slice(hh * GLA_DK, (hh + 1) * GLA_DK)
            q = proj_ref[rows, hh * GLA_DK:(hh + 1) * GLA_DK] * scale
            k = proj_ref[rows, hk + hh * GLA_DK:hk + (hh + 1) * GLA_DK]
            v = proj_ref[rows, 2 * hk + hh * GLA_DV:2 * hk + (hh + 1) * GLA_DV]
            q_dec = q * e_pos[:, ks]
            k_inv = k * e_neg[:, ks]
            scores = jnp.where(causal, _bdot_nt(q_dec, k_inv), 0.0)
            st = st_ref[hh]
            ob_ref[rows, hh * GLA_DV:(hh + 1) * GLA_DV] = _bdot(scores, v) + _bdot_nt(q_dec, st)
            k_end = k * e_end[:, ks]
            st_ref[hh] = st * e_last[:, ks] + _bdot_tn(v, k_end)
        return carry

    lax.fori_loop(0, tb // CHUNK, chunk, 0)
    normed = [_rms(ob_ref[:, hh * GLA_DV:(hh + 1) * GLA_DV], on_ref[:, hh * GLA_DV:(hh + 1) * GLA_DV])
              for hh in range(GLA_HEADS)]
    r = proj_ref[:, 2 * hk + hv:2 * hk + 2 * hv]
    y = jnp.concatenate(normed, axis=-1) * (r * _sigmoid(r))
    o_ref[...] = x + _bdot(y, wo_ref[...])


def _gla(x, n1, w_in, w_alpha_up, b_alpha, out_norm, w_out, tb=256):
    s, d = x.shape
    hk, hv = GLA_HEADS * GLA_DK, GLA_HEADS * GLA_DV
    nm = 2 * hk + 2 * hv
    rank = w_alpha_up.shape[0]
    wm = w_in[:, :nm].astype(BF16)
    wa = jnp.pad(w_in[:, nm:], ((0, 0), (0, LANES - rank))).astype(BF16)
    wup = jnp.pad(w_alpha_up, ((0, LANES - rank), (0, 0))).astype(BF16)
    return pl.pallas_call(
        _gla_kernel,
        grid=(s // tb,),
        in_specs=[pl.BlockSpec((tb, d), lambda i: (i, 0)), _full((1, d)), _full((d, nm)),
                  _full((d, LANES)), _full((LANES, hk)), _full((1, hk)), _full((1, hv)), _full((hv, d))],
        out_specs=pl.BlockSpec((tb, d), lambda i: (i, 0)),
        out_shape=jax.ShapeDtypeStruct((s, d), F32),
        scratch_shapes=[pltpu.VMEM((tb, nm), F32), pltpu.VMEM((tb, hk), F32), pltpu.VMEM((tb, hv), F32),
                        pltpu.VMEM((GLA_HEADS, GLA_DV, GLA_DK), F32)],
        compiler_params=_params(("arbitrary",)),
        name="gla_mixer",
    )(x, _row(n1), wm, wa, wup, _row(b_alpha), _row(out_norm), w_out.astype(BF16))


def _mlstm_kernel(x_ref, n1_ref, wm_ref, wif_ref, wift_ref, bif_ref, bift_ref, on_ref, wo_ref, o_ref,
                  h_ref, proj_ref, gc_ref, ob_ref, c_ref, n_ref, m_ref):
    tb = x_ref.shape[0]
    nh = ML_HEADS
    hk, hv = nh * ML_DK, nh * ML_DV

    @pl.when(pl.program_id(0) == 0)
    def _():
        c_ref[...] = jnp.zeros_like(c_ref)
        n_ref[...] = jnp.zeros_like(n_ref)
        m_ref[...] = jnp.zeros_like(m_ref)

    x = x_ref[...]
    h = _rms(x, n1_ref[...]).astype(BF16)
    h_ref[...] = h
    proj_ref[...] = _dot(h, wm_ref[...])
    gc_ref[...] = _dot(h, wif_ref[...]) + bif_ref[...]
    row, col = _tri_masks(CHUNK)
    causal = row >= col
    tri = causal.astype(F32)
    scale = ML_DK ** -0.5

    def chunk(c, carry):
        rows = pl.ds(pl.multiple_of(c * CHUNK, CHUNK), CHUNK)
        g_col = gc_ref[rows, :]
        g_row = _dot_nt(wift_ref[...], h_ref[rows, :]) + bift_ref[...]
        b_col = _dot(tri, _log_sigmoid(g_col), precision=HIGHEST)
        b_row = _dot_nt(_log_sigmoid(g_row), tri, precision=HIGHEST)
        for hh in range(nh):
            bc, ic = b_col[:, nh + hh:nh + hh + 1], g_col[:, hh:hh + 1]
            br, ir = b_row[nh + hh:nh + hh + 1, :], g_row[hh:hh + 1, :]
            q = proj_ref[rows, hh * ML_DK:(hh + 1) * ML_DK]
            k = proj_ref[rows, hk + hh * ML_DK:hk + (hh + 1) * ML_DK] * scale
            v = proj_ref[rows, 2 * hk + hh * ML_DV:2 * hk + (hh + 1) * ML_DV]
            m_prev = m_ref[0:1, hh:hh + 1]
            d_log = jnp.where(causal, bc - br + ir, -jnp.inf)
            inter = bc + m_prev
            m_t = jnp.maximum(inter, jnp.max(d_log, axis=-1, keepdims=True))
            w_intra = jnp.exp(d_log - m_t)
            sc = jnp.exp(inter - m_t)
            qk = _bdot_nt(q, k) * w_intra
            c_st, n_st = c_ref[hh], n_ref[hh]
            num = _bdot(qk, v) + sc * _bdot(q, c_st)
            den = jnp.sum(qk, axis=-1, keepdims=True) + sc * jnp.sum(q * n_st, axis=-1, keepdims=True)
            ob_ref[rows, hh * ML_DV:(hh + 1) * ML_DV] = num / jnp.maximum(jnp.abs(den), jnp.exp(-m_t))
            b_last = bc[CHUNK - 1:CHUNK, :]
            g_end = b_last - bc + ic
            m_new = jnp.maximum(b_last + m_prev, jnp.max(g_end, axis=0, keepdims=True))
            kw = k * jnp.exp(g_end - m_new)
            carry_scale = jnp.exp(b_last + m_prev - m_new)
            c_ref[hh] = carry_scale * c_st + _bdot_tn(kw, v)
            n_ref[hh] = carry_scale * n_st + jnp.sum(kw, axis=0, keepdims=True)
            m_ref[0:1, hh:hh + 1] = m_new
        return carry

    lax.fori_loop(0, tb // CHUNK, chunk, 0)
    normed = [_rms(ob_ref[:, hh * ML_DV:(hh + 1) * ML_DV], on_ref[:, hh * ML_DV:(hh + 1) * ML_DV])
              for hh in range(nh)]
    o_pre = proj_ref[:, 2 * hk + hv:2 * hk + 2 * hv]
    y = jnp.concatenate(normed, axis=-1) * _sigmoid(o_pre)
    o_ref[...] = x + _bdot(y, wo_ref[...])


def _mlstm(x, n1, w_in, b_if, out_norm, w_out, tb=256):
    s, d = x.shape
    nh = ML_HEADS
    hk, hv = nh * ML_DK, nh * ML_DV
    nm = 2 * hk + 2 * hv
    wm = w_in[:, :nm].astype(BF16)
    w_if = w_in[:, nm:]
    wif = jnp.pad(w_if, ((0, 0), (0, LANES - 2 * nh))).astype(BF16)
    wift = w_if.T.astype(BF16)
    return pl.pallas_call(
        _mlstm_kernel,
        grid=(s // tb,),
        in_specs=[pl.BlockSpec((tb, d), lambda i: (i, 0)), _full((1, d)), _full((d, nm)),
                  _full((d, LANES)), _full((2 * nh, d)), _full((1, LANES)), _full((2 * nh, 1)),
                  _full((1, hv)), _full((hv, d))],
        out_specs=pl.BlockSpec((tb, d), lambda i: (i, 0)),
        out_shape=jax.ShapeDtypeStruct((s, d), F32),
        scratch_shapes=[pltpu.VMEM((tb, d), BF16), pltpu.VMEM((tb, nm), F32), pltpu.VMEM((tb, LANES), F32),
                        pltpu.VMEM((tb, hv), F32), pltpu.VMEM((nh, ML_DK, ML_DV), F32),
                        pltpu.VMEM((nh, 1, ML_DK), F32), pltpu.VMEM((8, LANES), F32)],
        compiler_params=_params(("arbitrary",)),
        name="mlstm_mixer",
    )(x, _row(n1), wm, wif, wift, _row(b_if, LANES), b_if.reshape(2 * nh, 1).astype(F32),
      _row(out_norm), w_out.astype(BF16))


def _rwkv_proj_kernel(x_ref, n1_ref, mu_ref, wr_ref, wk_ref, wv_ref, w0_ref, w1_ref, w2_ref,
                      a0_ref, a1_ref, a2_ref, g1_ref, g2_ref, kk_ref, ka_ref,
                      r_ref, lw_ref, k_ref, v_ref, kkraw_ref, a_ref, g_ref, car_ref):
    tm = x_ref.shape[0]

    @pl.when(pl.program_id(0) == 0)
    def _():
        car_ref[...] = jnp.zeros_like(car_ref)

    h = _rms(x_ref[...], n1_ref[...])
    rowi = lax.broadcasted_iota(jnp.int32, h.shape, 0)
    h_prev = jnp.where(rowi == 0, car_ref[7:8, :], pltpu.roll(h, 1, 0))
    car_ref[...] = h[tm - 8:, :]
    xx = h_prev - h
    xr, xw, xk, xv, xa, xg = (h + xx * mu_ref[j:j + 1, :] for j in range(6))
    r = _bdot(xr, wr_ref[...])
    k = _bdot(xk, wk_ref[...])
    v = _bdot(xv, wv_ref[...])
    w_log = -_softplus(-(w0_ref[...] + _bdot(jnp.tanh(_bdot(xw, w1_ref[...])), w2_ref[...]))) - 0.5
    a = _sigmoid(a0_ref[...] + _bdot(_bdot(xa, a1_ref[...]), a2_ref[...]))
    g = _bdot(_sigmoid(_bdot(xg, g1_ref[...])), g2_ref[...])
    r_ref[...] = r
    lw_ref[...] = -jnp.exp(w_log)
    k_ref[...] = k * (1.0 + (a - 1.0) * ka_ref[...])
    v_ref[...] = v
    kkraw_ref[...] = k * kk_ref[...]
    a_ref[...] = a
    g_ref[...] = g


def _rwkv_proj(x, n1, mu, w_rkv, w0, w1, w2, a0, a1, a2, g1, g2, k_k, k_a, tm=256):
    s, d = x.shape

    def padc(w):
        return jnp.pad(w, ((0, 0), (0, LANES - w.shape[1]))).astype(BF16) if w.shape[1] < LANES else w.astype(BF16)

    def padr(w):
        return jnp.pad(w, ((0, LANES - w.shape[0]), (0, 0))).astype(BF16) if w.shape[0] < LANES else w.astype(BF16)

    mu8 = jnp.pad(mu, ((0, 8 - mu.shape[0]), (0, 0)))
    blk = pl.BlockSpec((tm, d), lambda i: (i, 0))
    out = jax.ShapeDtypeStruct((s, d), F32)
    return pl.pallas_call(
        _rwkv_proj_kernel,
        grid=(s // tm,),
        in_specs=[blk, _full((1, d)), _full((8, d)), _full((d, d)), _full((d, d)), _full((d, d)),
                  _full((1, d)), _full((d, LANES)), _full((LANES, d)),
                  _full((1, d)), _full((d, LANES)), _full((LANES, d)),
                  _full((d, LANES)), _full((LANES, d)), _full((1, d)), _full((1, d))],
        out_specs=[blk] * 7,
        out_shape=[out] * 7,
        scratch_shapes=[pltpu.VMEM((8, d), F32)],
        compiler_params=_params(("arbitrary",)),
        name="rwkv_proj",
    )(x, _row(n1), mu8, w_rkv[0].astype(BF16), w_rkv[1].astype(BF16), w_rkv[2].astype(BF16),
      _row(w0), padc(w1), padr(w2), _row(a0), padc(a1), padr(a2), padc(g1), padr(g2), _row(k_k), _row(k_a))


def _unit_lower_inverse(a_strict, row, col):
    n = a_strict.shape[0]
    eye = (row == col).astype(F32)
    blk16 = (row // 16) == (col // 16)
    p = jnp.where(blk16, a_strict, 0.0)
    inv = eye + p
    p = _bdot(p, p)
    for _ in range(2):
        both = _bdot(jnp.concatenate([p, inv], axis=0), p)
        p, inv = both[:n], inv + both[n:]
    inv = inv + _bdot(inv, p)
    size = 32
    prev = blk16
    while size <= CHUNK:
        cur = (row // size) == (col // size)
        off = jnp.where(cur & ~prev, a_strict, 0.0)
        inv = inv + _bdot(_bdot(inv, off), inv)
        prev = cur
        size *= 2
    return inv


def _rwkv_scan_kernel(r_ref, lw_ref, k_ref, v_ref, kkraw_ref, a_ref, rk_ref, gg_ref, gb_ref, o_ref, s_ref):
    tb = r_ref.shape[0]
    n = RWKV_HEAD
    grp = 2 * CHUNK
    big = 2 * grp

    @pl.when(pl.program_id(1) == 0)
    def _():
        s_ref[...] = jnp.zeros_like(s_ref)

    rown = lax.broadcasted_iota(jnp.int32, (grp, LANES), 0)
    head0 = lax.broadcasted_iota(jnp.int32, (1, LANES), 1) < n
    chunk0 = rown < CHUNK
    row, col = _tri_masks(big)
    same = (row // CHUNK) == (col // CHUNK)
    incl = same & (row >= col)
    strict = same & (row > col)
    tri = incl[:grp, :grp].astype(F32)
    srow = lax.broadcasted_iota(jnp.int32, (big, LANES), 0)
    slane = lax.broadcasted_iota(jnp.int32, (big, LANES), 1)
    srow_chunk0 = (srow // CHUNK) % 2 == 0
    diag = (srow % grp) == slane
    own = ((rown < n) == head0)

    def stack(t):
        return jnp.concatenate([jnp.where(head0, t, 0.0), jnp.where(head0, 0.0, t)], axis=0)

    def blockdiag(t):
        s = stack(t)
        return jnp.concatenate([jnp.where(srow_chunk0, s, 0.0), jnp.where(srow_chunk0, 0.0, s)], axis=1)

    def head_sum(t):
        return jnp.where(head0, jnp.sum(jnp.where(head0, t, 0.0), axis=-1, keepdims=True),
                         jnp.sum(jnp.where(head0, 0.0, t), axis=-1, keepdims=True))

    def per_chunk(t, c):
        return jnp.concatenate([t[c * CHUNK:(c + 1) * CHUNK], t[grp + c * CHUNK:grp + (c + 1) * CHUNK]], axis=0)

    for g in range(tb // grp):
        rows = slice(g * grp, (g + 1) * grp)
        r, lw, k, v = r_ref[rows, :], lw_ref[rows, :], k_ref[rows, :], v_ref[rows, :]
        kk, a = kkraw_ref[rows, :], a_ref[rows, :]
        kk = kk / jnp.maximum(jnp.sqrt(head_sum(kk * kk)), 1e-12)
        cum = _dot(tri, lw, precision=HIGHEST)
        cum_last = jnp.where(chunk0, cum[CHUNK - 1:CHUNK, :], cum[grp - 1:grp, :])
        e_neg, e_end = jnp.exp(-cum), jnp.exp(cum_last - cum)
        b_vec = kk * a
        a_t = stack(-kk * jnp.exp(cum - lw))
        r_t = stack(r * jnp.exp(cum))
        v_s = stack(v)
        a4 = _bdot_nt(jnp.concatenate([a_t, r_t], axis=0),
                      jnp.concatenate([stack(b_vec * e_neg), stack(k * e_neg)], axis=0))
        a_ab = jnp.where(strict, a4[:big, :big], 0.0)
        a_xk = jnp.concatenate([jnp.where(strict, a4[:big, big:], 0.0), jnp.where(incl, a4[big:, big:], 0.0)], axis=0)
        a_rb = jnp.where(incl, a4[big:, :big], 0.0)
        t_inv = _unit_lower_inverse(a_ab, row, col)
        xv = _bdot(a_xk, v_s)
        z = _bdot(t_inv, jnp.concatenate([a_t, xv[:big]], axis=1))
        gh = _bdot_tn(blockdiag(b_vec * e_end), z)
        kv = _bdot_tn(blockdiag(k * e_end), v_s)
        ry = _bdot(a_rb, z)
        p_end = jnp.exp(jnp.concatenate([cum[CHUNK - 1:CHUNK, :], cum[grp - 1:grp, :]], axis=0))
        g_all = gh[:, :LANES] + jnp.where(diag, jnp.where(srow < grp, p_end[0:1, :], p_end[1:2, :]), 0.0)
        h_all = gh[:, LANES:] + kv
        rp_all = r_t + ry[:, :LANES]
        y0_all = ry[:, LANES:] + xv[big:]
        for c in range(2):
            crow = slice(g * grp + c * CHUNK, g * grp + (c + 1) * CHUNK)
            lhs = jnp.concatenate([g_all[c * grp:(c + 1) * grp], per_chunk(rp_all, c)], axis=0)
            both = _bdot(lhs, s_ref[...])
            s_ref[...] = both[:grp] + h_all[c * grp:(c + 1) * grp]
            y = both[grp:] + per_chunk(y0_all, c)
            mean = jnp.sum(y, axis=-1, keepdims=True) * (1.0 / n)
            dev = jnp.where(own, y - mean, 0.0)
            var = jnp.sum(dev * dev, axis=-1, keepdims=True) * (1.0 / n)
            yn = dev * lax.rsqrt(var + RWKV_GN_EPS)
            yn = yn[:CHUNK] + yn[CHUNK:]
            rc, kc, vc = r_ref[crow, :], k_ref[crow, :], v_ref[crow, :]
            bonus = head_sum(rc * kc * rk_ref[...])[:CHUNK] * vc
            o_ref[crow, :] = yn * gg_ref[...] + gb_ref[...] + bonus


def _rwkv_scan(r, lw, k, v, kkraw, a, r_k, gn_g, gn_b, tb=512):
    s, d = r.shape
    tb = min(tb, s)
    blk = pl.BlockSpec((tb, LANES), lambda p, i: (i, p))
    vec = pl.BlockSpec((1, LANES), lambda p, i: (0, p))
    return pl.pallas_call(
        _rwkv_scan_kernel,
        grid=(d // LANES, s // tb),
        in_specs=[blk] * 6 + [vec] * 3,
        out_specs=blk,
        out_shape=jax.ShapeDtypeStruct((s, d), F32),
        scratch_shapes=[pltpu.VMEM((LANES, LANES), F32)],
        compiler_params=_params(("arbitrary", "arbitrary")),
        name="rwkv_scan",
    )(r, lw, k, v, kkraw, a, _row(r_k), _row(gn_g), _row(gn_b))


def _out_proj_kernel(x_ref, a_ref, g_ref, w_ref, o_ref, *, gated):
    a = a_ref[...]
    if gated:
        a = a * g_ref[...]
    o_ref[...] = x_ref[...] + _bdot(a, w_ref[...])


def _out_proj(x, a, w, gate=None, tm=512):
    s, d = x.shape
    blk = pl.BlockSpec((tm, d), lambda i: (i, 0))
    return pl.pallas_call(
        functools.partial(_out_proj_kernel, gated=gate is not None),
        grid=(s // tm,),
        in_specs=[blk, blk, blk, _full((d, d))],
        out_specs=blk,
        out_shape=jax.ShapeDtypeStruct((s, d), F32),
        compiler_params=_params(("arbitrary",)),
        name="out_proj",
    )(x, a, gate if gate is not None else a, w.astype(BF16))


def _sb_qkv_kernel(x_ref, n1_ref, w_ref, q_ref, k_ref, v_ref):
    d = x_ref.shape[1]
    h = _rms(x_ref[...], n1_ref[...]).astype(BF16)
    qkv = _dot(h, w_ref[...])
    q_ref[...] = (qkv[:, :d] * SB_HEAD_DIM ** -0.5).astype(BF16)
    k_ref[...] = qkv[:, d:2 * d].astype(BF16)
    v_ref[...] = qkv[:, 2 * d:].astype(BF16)


def _sb_qkv(x, n1, w_qkv, tm=512):
    s, d = x.shape
    blk = pl.BlockSpec((tm, d), lambda i: (i, 0))
    out = jax.ShapeDtypeStruct((s, d), BF16)
    return pl.pallas_call(
        _sb_qkv_kernel,
        grid=(s // tm,),
        in_specs=[blk, _full((1, d)), _full((d, 3 * d))],
        out_specs=[blk] * 3,
        out_shape=[out] * 3,
        compiler_params=_params(("arbitrary",)),
        name="sb_qkv",
    )(x, _row(n1), w_qkv.astype(BF16))


def _sb_attn_kernel(q_ref, k_ref, v_ref, o_ref, *, tq):
    qi = pl.program_id(1)
    head0 = lax.broadcasted_iota(jnp.int32, (1, LANES), 1) < SB_HEAD_DIM
    q = q_ref[...]
    qs = (jnp.where(head0, q, jnp.zeros_like(q)), jnp.where(head0, jnp.zeros_like(q), q))
    row, col = _tri_masks(tq)
    before = col < row
    later = (row > col).astype(BF16)

    def tile(kb, cs, accs, diagonal):
        ks = pl.ds(pl.multiple_of(kb * tq, tq), tq)
        kt, vt = k_ref[ks, :], v_ref[ks, :]
        out_c, out_acc = [], []
        for h in range(2):
            z = _dot_nt(qs[h], kt)
            lk = -_softplus(z)
            if diagonal:
                lk = jnp.where(before, lk, 0.0)
            hi = lk.astype(BF16)
            lo = (lk - hi.astype(F32)).astype(BF16)
            excl = _dot(hi, later) + _dot(lo, later)
            w = jnp.exp(z + lk + excl + cs[h])
            if diagonal:
                w = jnp.where(before, w, 0.0)
            out_acc.append(accs[h] + _dot(w.astype(BF16), vt))
            out_c.append(cs[h] + excl[:, 0:1] + lk[:, 0:1])
        return tuple(out_c), tuple(out_acc)

    c0 = jnp.zeros((tq, 1), F32)
    acc0 = jnp.zeros((tq, LANES), F32)
    cs, accs = tile(qi, (c0, c0), (acc0, acc0), True)

    def cond(st):
        kb, cs, _ = st
        return jnp.logical_and(kb >= 0, jnp.max(jnp.maximum(cs[0], cs[1])) > SB_ZERO_LOG)

    def body(st):
        kb, cs, accs = st
        cs, accs = tile(kb, cs, accs, False)
        return kb - 1, cs, accs

    _, _, accs = lax.while_loop(cond, body, (qi - 1, cs, accs))
    o_ref[...] = jnp.where(head0, accs[0], accs[1])


def _sb_attn(q, k, v, tq=256):
    s, d = q.shape
    return pl.pallas_call(
        functools.partial(_sb_attn_kernel, tq=tq),
        grid=(d // LANES, s // tq),
        in_specs=[pl.BlockSpec((tq, LANES), lambda p, i: (i, p)),
                  pl.BlockSpec((s, LANES), lambda p, i: (0, p)),
                  pl.BlockSpec((s, LANES), lambda p, i: (0, p))],
        out_specs=pl.BlockSpec((tq, LANES), lambda p, i: (i, p)),
        out_shape=jax.ShapeDtypeStruct((s, d), F32),
        compiler_params=_params(("arbitrary", "arbitrary")),
        name="sb_attn",
    )(q, k, v)


def _stick_breaking(x, n1, w_qkv, w_out):
    q, k, v = _sb_qkv(x, n1, w_qkv)
    return _out_proj(x, _sb_attn(q, k, v), w_out)


def _rwkv7(x, n1, mu, w_rkv, w0, w1, w2, a0, a1, a2, g1, g2, k_k, k_a, r_k, gn_g, gn_b, w_out):
    r, lw, k, v, kkraw, a, g = _rwkv_proj(x, n1, mu, w_rkv, w0, w1, w2, a0, a1, a2, g1, g2, k_k, k_a)
    o = _rwkv_scan(r, lw, k, v, kkraw, a, r_k, gn_g, gn_b)
    return _out_proj(x, o, w_out, gate=g)


def kernel(x, l0_norm1, l0_gla_w_in, l0_gla_w_alpha_up, l0_gla_b_alpha, l0_gla_out_norm, l0_gla_w_out, l0_norm2, l0_ffn_w_up, l0_ffn_conv_w, l0_ffn_conv_b, l0_ffn_w_down, l1_norm1, l1_rwkv_mu, l1_rwkv_w_rkv, l1_rwkv_w0, l1_rwkv_w1, l1_rwkv_w2, l1_rwkv_a0, l1_rwkv_a1, l1_rwkv_a2, l1_rwkv_g1, l1_rwkv_g2, l1_rwkv_k_k, l1_rwkv_k_a, l1_rwkv_r_k, l1_rwkv_gn_g, l1_rwkv_gn_b, l1_rwkv_w_out, l1_norm2, l1_ffn_w_up, l1_ffn_conv_w, l1_ffn_conv_b, l1_ffn_w_down, l2_norm1, l2_sb_w_qkv, l2_sb_w_out, l2_norm2, l2_ffn_w_up, l2_ffn_conv_w, l2_ffn_conv_b, l2_ffn_w_down, l3_norm1, l3_ml_w_in, l3_ml_b_if, l3_ml_out_norm, l3_ml_w_out, l3_norm2, l3_ffn_w_up, l3_ffn_conv_w, l3_ffn_conv_b, l3_ffn_w_down, final_norm):
    b, s, d = x.shape
    outs = []
    for bi in range(b):
        h = x[bi]
        h = _gla(h, l0_norm1, l0_gla_w_in, l0_gla_w_alpha_up, l0_gla_b_alpha, l0_gla_out_norm, l0_gla_w_out)
        h = _ffn(h, l0_norm2, l0_ffn_w_up, l0_ffn_conv_w, l0_ffn_conv_b, l0_ffn_w_down)
        h = _rwkv7(h, l1_norm1, l1_rwkv_mu, l1_rwkv_w_rkv, l1_rwkv_w0, l1_rwkv_w1, l1_rwkv_w2,
                   l1_rwkv_a0, l1_rwkv_a1, l1_rwkv_a2, l1_rwkv_g1, l1_rwkv_g2,
                   l1_rwkv_k_k, l1_rwkv_k_a, l1_rwkv_r_k, l1_rwkv_gn_g, l1_rwkv_gn_b, l1_rwkv_w_out)
        h = _ffn(h, l1_norm2, l1_ffn_w_up, l1_ffn_conv_w, l1_ffn_conv_b, l1_ffn_w_down)
        h = _stick_breaking(h, l2_norm1, l2_sb_w_qkv, l2_sb_w_out)
        h = _ffn(h, l2_norm2, l2_ffn_w_up, l2_ffn_conv_w, l2_ffn_conv_b, l2_ffn_w_down)
        h = _mlstm(h, l3_norm1, l3_ml_w_in, l3_ml_b_if, l3_ml_out_norm, l3_ml_w_out)
        h = _ffn(h, l3_norm2, l3_ffn_w_up, l3_ffn_conv_w, l3_ffn_conv_b, l3_ffn_w_down, final_g=final_norm)
        outs.append(h)
    return jnp.stack(outs, axis=0)
```

```python
import functools

import jax
import jax.numpy as jnp
from jax import lax
from jax.experimental import pallas as pl
from jax.experimental.pallas import tpu as pltpu

F32 = jnp.float32
BF16 = jnp.bfloat16
HIGHEST = lax.Precision.HIGHEST

NORM_EPS = 1e-6
CHUNK = 64
GLA_HEADS, GLA_DK, GLA_DV, GLA_TAU = 4, 128, 256, 16.0
ML_HEADS, ML_DK, ML_DV = 4, 128, 256
ML_CHUNK = 256
RWKV_HEAD = 64
RWKV_GN_EPS = 64e-5
SB_HEADS, SB_HEAD_DIM = 16, 64
CONV_WIDTH = 3
LANES = 128
SB_ZERO_LOG = -104.0


def _rms(x, g, eps=NORM_EPS):
    return x * lax.rsqrt(jnp.mean(x * x, axis=-1, keepdims=True) + eps) * g


def _softplus(x):
    return jnp.maximum(x, 0.0) + jnp.log1p(jnp.exp(-jnp.abs(x)))


def _log_sigmoid(x):
    return -_softplus(-x)


def _sigmoid(x):
    return 1.0 / (1.0 + jnp.exp(-x))


def _dot(a, b, **kw):
    return jnp.dot(a, b, preferred_element_type=F32, **kw)


def _dot_nt(a, b, **kw):
    return lax.dot_general(a, b, (((1,), (1,)), ((), ())), preferred_element_type=F32, **kw)


def _dot_tn(a, b, **kw):
    return lax.dot_general(a, b, (((0,), (0,)), ((), ())), preferred_element_type=F32, **kw)


def _bdot(a, b):
    return _dot(a.astype(BF16), b.astype(BF16))


def _bdot_nt(a, b):
    return _dot_nt(a.astype(BF16), b.astype(BF16))


def _bdot_tn(a, b):
    return _dot_tn(a.astype(BF16), b.astype(BF16))


def _tri_masks(n):
    row = lax.broadcasted_iota(jnp.int32, (n, n), 0)
    col = lax.broadcasted_iota(jnp.int32, (n, n), 1)
    return row, col


def _full(shape):
    return pl.BlockSpec(shape, lambda *_: (0,) * len(shape), pipeline_mode=pl.Buffered(1))


def _params(sem):
    return pltpu.CompilerParams(dimension_semantics=sem, vmem_limit_bytes=56 * 1024 * 1024)


def _row(v, width=None):
    v = v.reshape(1, -1).astype(F32)
    if width is not None and v.shape[1] < width:
        v = jnp.pad(v, ((0, 0), (0, width - v.shape[1])))
    return v


def _ffn_kernel(x_ref, g_ref, wup_ref, cw_ref, wd_ref, fg_ref, o_ref, h_ref, act_ref, car_ref, *, final_norm, fc):
    tm = x_ref.shape[0]
    f = wd_ref.shape[0]

    @pl.when(pl.program_id(0) == 0)
    def _():
        car_ref[...] = jnp.zeros_like(car_ref)

    x = x_ref[...]
    h_ref[...] = _rms(x, g_ref[...]).astype(BF16)
    rowi = lax.broadcasted_iota(jnp.int32, (tm, fc), 0)

    def conv_up(cols):
        u = _dot(h_ref[...], wup_ref[:, cols])
        c2, c1 = car_ref[6:7, cols], car_ref[7:8, cols]
        s1 = jnp.where(rowi == 0, c1, pltpu.roll(u, 1, 0))
        s2 = jnp.where(rowi == 0, c2, jnp.where(rowi == 1, c1, pltpu.roll(u, 2, 0)))
        car_ref[:, cols] = u[tm - 8:, :]
        return s2 * cw_ref[0:1, cols] + s1 * cw_ref[1:2, cols] + u * cw_ref[2:3, cols] + cw_ref[3:4, cols]

    for c in range(f // fc):
        gate = conv_up(slice(c * fc, (c + 1) * fc))
        up = conv_up(slice(f + c * fc, f + (c + 1) * fc))
        act_ref[:, c * fc:(c + 1) * fc] = (gate * _sigmoid(gate) * up).astype(BF16)
    y = x + _dot(act_ref[...], wd_ref[...])
    if final_norm:
        y = _rms(y, fg_ref[...])
    o_ref[...] = y


def _ffn(x, norm_g, w_up, conv_w, conv_b, w_down, final_g=None, tm=1024, fc=256):
    s, d = x.shape
    f = w_down.shape[0]
    taps = jnp.concatenate([conv_w, conv_b[None, :], jnp.zeros((4, 2 * f), F32)], axis=0)
    fg = _row(final_g if final_g is not None else jnp.ones((d,), F32))
    return pl.pallas_call(
        functools.partial(_ffn_kernel, final_norm=final_g is not None, fc=fc),
        grid=(s // tm,),
        in_specs=[pl.BlockSpec((tm, d), lambda i: (i, 0)), _full((1, d)), _full((d, 2 * f)), _full((8, 2 * f)),
                  _full((f, d)), _full((1, d))],
        out_specs=pl.BlockSpec((tm, d), lambda i: (i, 0)),
        out_shape=jax.ShapeDtypeStruct((s, d), F32),
        scratch_shapes=[pltpu.VMEM((tm, d), BF16), pltpu.VMEM((tm, f), BF16), pltpu.VMEM((8, 2 * f), F32)],
        compiler_params=_params(("arbitrary",)),
        name="conv_ffn",
    )(x, _row(norm_g), w_up.astype(BF16), taps, w_down.astype(BF16), fg)


def _gla_kernel(x_ref, n1_ref, wm_ref, wa_ref, wup_ref, ba_ref, on_ref, wo_ref, o_ref,
                proj_ref, la_ref, ob_ref, st_ref):
    tb = x_ref.shape[0]
    hk, hv = GLA_HEADS * GLA_DK, GLA_HEADS * GLA_DV

    @pl.when(pl.program_id(0) == 0)
    def _():
        st_ref[...] = jnp.zeros_like(st_ref)

    x = x_ref[...]
    h = _rms(x, n1_ref[...]).astype(BF16)
    proj_ref[...] = _dot(h, wm_ref[...])
    a_low = _dot(h, wa_ref[...])
    la_ref[...] = _log_sigmoid(_bdot(a_low, wup_ref[...]) + ba_ref[...]) * (1.0 / GLA_TAU)
    grp = 4 * CHUNK
    row, col = _tri_masks(grp)
    causal = ((row // CHUNK) == (col // CHUNK)) & (row >= col)
    tri = causal.astype(F32)
    rowc = lax.broadcasted_iota(jnp.int32, (grp, hk), 0) // CHUNK
    head_of_lane = lax.broadcasted_iota(jnp.int32, (1, hk), 1) // GLA_DK
    scale = GLA_DK ** -0.5

    def stack(t):
        return jnp.concatenate([jnp.where(head_of_lane == hh, t, 0.0) for hh in range(GLA_HEADS)], axis=0)

    for g in range(tb // grp):
        r0 = g * grp
        rows = slice(r0, r0 + grp)
        b = _dot(tri, la_ref[rows, :], precision=HIGHEST)
        b_last = b[CHUNK - 1:CHUNK, :]
        for c in range(1, 4):
            b_last = jnp.where(rowc == c, b[(c + 1) * CHUNK - 1:(c + 1) * CHUNK, :], b_last)
        q_dec = proj_ref[rows, 0:hk] * scale * jnp.exp(b)
        k = proj_ref[rows, hk:2 * hk]
        k_inv = k * jnp.exp(-b)
        k_end = k * jnp.exp(b_last - b)
        for hh in range(GLA_HEADS):
            ks = slice(hh * GLA_DK, (hh + 1) * GLA_DK)
            scores = jnp.where(causal, _bdot_nt(q_dec[:, ks], k_inv[:, ks]), 0.0)
            v = proj_ref[rows, 2 * hk + hh * GLA_DV:2 * hk + (hh + 1) * GLA_DV]
            ob_ref[rows, hh * GLA_DV:(hh + 1) * GLA_DV] = _bdot(scores, v)
        for c in range(4):
            cr = slice(r0 + c * CHUNK, r0 + (c + 1) * CHUNK)
            lr = slice(c * CHUNK, (c + 1) * CHUNK)
            st = st_ref[...]
            inter = _bdot_nt(stack(q_dec[lr, :]), st)
            v_s = jnp.concatenate([proj_ref[cr, 2 * hk + hh * GLA_DV:2 * hk + (hh + 1) * GLA_DV]
                                   for hh in range(GLA_HEADS)], axis=0)
            for hh in range(GLA_HEADS):
                ob_ref[cr, hh * GLA_DV:(hh + 1) * GLA_DV] += inter[hh * CHUNK:(hh + 1) * CHUNK, :]
            e_last = jnp.exp(b[(c + 1) * CHUNK - 1:(c + 1) * CHUNK, :])
            st_ref[...] = st * e_last + _bdot_tn(v_s, stack(k_end[lr, :]))
    normed = [_rms(ob_ref[:, hh * GLA_DV:(hh + 1) * GLA_DV], on_ref[:, hh * GLA_DV:(hh + 1) * GLA_DV])
              for hh in range(GLA_HEADS)]
    r = proj_ref[:, 2 * hk + hv:2 * hk + 2 * hv]
    y = jnp.concatenate(normed, axis=-1) * (r * _sigmoid(r))
    o_ref[...] = x + _bdot(y, wo_ref[...])


def _gla(x, n1, w_in, w_alpha_up, b_alpha, out_norm, w_out, tb=512):
    s, d = x.shape
    hk, hv = GLA_HEADS * GLA_DK, GLA_HEADS * GLA_DV
    nm = 2 * hk + 2 * hv
    rank = w_alpha_up.shape[0]
    wm = w_in[:, :nm].astype(BF16)
    wa = jnp.pad(w_in[:, nm:], ((0, 0), (0, LANES - rank))).astype(BF16)
    wup = jnp.pad(w_alpha_up, ((0, LANES - rank), (0, 0))).astype(BF16)
    return pl.pallas_call(
        _gla_kernel,
        grid=(s // tb,),
        in_specs=[pl.BlockSpec((tb, d), lambda i: (i, 0)), _full((1, d)), _full((d, nm)),
                  _full((d, LANES)), _full((LANES, hk)), _full((1, hk)), _full((1, hv)), _full((hv, d))],
        out_specs=pl.BlockSpec((tb, d), lambda i: (i, 0)),
        out_shape=jax.ShapeDtypeStruct((s, d), F32),
        scratch_shapes=[pltpu.VMEM((tb, nm), F32), pltpu.VMEM((tb, hk), F32), pltpu.VMEM((tb, hv), F32),
                        pltpu.VMEM((GLA_DV, hk), F32)],
        compiler_params=_params(("arbitrary",)),
        name="gla_mixer",
    )(x, _row(n1), wm, wa, wup, _row(b_alpha), _row(out_norm), w_out.astype(BF16))


def _mlstm_kernel(x_ref, n1_ref, wm_ref, wif_ref, wift_ref, bif_ref, bift_ref, on_ref, wo_ref, o_ref,
                  h_ref, proj_ref, gc_ref, ob_ref, c_ref, n_ref, m_ref):
    tb = x_ref.shape[0]
    nh = ML_HEADS
    hk, hv = nh * ML_DK, nh * ML_DV

    @pl.when(pl.program_id(0) == 0)
    def _():
        c_ref[...] = jnp.zeros_like(c_ref)
        n_ref[...] = jnp.zeros_like(n_ref)
        m_ref[...] = jnp.zeros_like(m_ref)

    x = x_ref[...]
    h = _rms(x, n1_ref[...]).astype(BF16)
    h_ref[...] = h
    proj_ref[...] = _dot(h, wm_ref[...])
    gc_ref[...] = _dot(h, wif_ref[...]) + bif_ref[...]
    row, col = _tri_masks(ML_CHUNK)
    causal = row >= col
    tri = causal.astype(F32)
    scale = ML_DK ** -0.5

    for c in range(tb // ML_CHUNK):
        rows = slice(c * ML_CHUNK, (c + 1) * ML_CHUNK)
        g_col = gc_ref[rows, :]
        g_row = _dot_nt(wift_ref[...], h_ref[rows, :]) + bift_ref[...]
        b_col = _dot(tri, _log_sigmoid(g_col), precision=HIGHEST)
        b_row = _dot_nt(_log_sigmoid(g_row), tri, precision=HIGHEST)
        for hh in range(nh):
            bc, ic = b_col[:, nh + hh:nh + hh + 1], g_col[:, hh:hh + 1]
            br, ir = b_row[nh + hh:nh + hh + 1, :], g_row[hh:hh + 1, :]
            q = proj_ref[rows, hh * ML_DK:(hh + 1) * ML_DK]
            k = proj_ref[rows, hk + hh * ML_DK:hk + (hh + 1) * ML_DK] * scale
            v = proj_ref[rows, 2 * hk + hh * ML_DV:2 * hk + (hh + 1) * ML_DV]
            m_prev = m_ref[0:1, hh:hh + 1]
            d_log = jnp.where(causal, bc - br + ir, -jnp.inf)
            inter = bc + m_prev
            m_t = jnp.maximum(inter, jnp.max(d_log, axis=-1, keepdims=True))
            w_intra = jnp.exp(d_log - m_t)
            sc = jnp.exp(inter - m_t)
            qk = _bdot_nt(q, k) * w_intra
            c_st, n_st = c_ref[hh], n_ref[hh]
            num = _bdot(qk, v) + sc * _bdot(q, c_st)
            den = jnp.sum(qk, axis=-1, keepdims=True) + sc * jnp.sum(q * n_st, axis=-1, keepdims=True)
            ob_ref[rows, hh * ML_DV:(hh + 1) * ML_DV] = num / jnp.maximum(jnp.abs(den), jnp.exp(-m_t))
            b_last = bc[ML_CHUNK - 1:ML_CHUNK, :]
            g_end = b_last - bc + ic
            m_new = jnp.maximum(b_last + m_prev, jnp.max(g_end, axis=0, keepdims=True))
            kw = k * jnp.exp(g_end - m_new)
            carry_scale = jnp.exp(b_last + m_prev - m_new)
            c_ref[hh] = carry_scale * c_st + _bdot_tn(kw, v)
            n_ref[hh] = carry_scale * n_st + jnp.sum(kw, axis=0, keepdims=True)
            m_ref[0:1, hh:hh + 1] = m_new
    normed = [_rms(ob_ref[:, hh * ML_DV:(hh + 1) * ML_DV], on_ref[:, hh * ML_DV:(hh + 1) * ML_DV])
              for hh in range(nh)]
    o_pre = proj_ref[:, 2 * hk + hv:2 * hk + 2 * hv]
    y = jnp.concatenate(normed, axis=-1) * _sigmoid(o_pre)
    o_ref[...] = x + _bdot(y, wo_ref[...])


def _mlstm(x, n1, w_in, b_if, out_norm, w_out, tb=512):
    s, d = x.shape
    nh = ML_HEADS
    hk, hv = nh * ML_DK, nh * ML_DV
    nm = 2 * hk + 2 * hv
    wm = w_in[:, :nm].astype(BF16)
    w_if = w_in[:, nm:]
    wif = jnp.pad(w_if, ((0, 0), (0, LANES - 2 * nh))).astype(BF16)
    wift = w_if.T.astype(BF16)
    return pl.pallas_call(
        _mlstm_kernel,
        grid=(s // tb,),
        in_specs=[pl.BlockSpec((tb, d), lambda i: (i, 0)), _full((1, d)), _full((d, nm)),
                  _full((d, LANES)), _full((2 * nh, d)), _full((1, LANES)), _full((2 * nh, 1)),
                  _full((1, hv)), _full((hv, d))],
        out_specs=pl.BlockSpec((tb, d), lambda i: (i, 0)),
        out_shape=jax.ShapeDtypeStruct((s, d), F32),
        scratch_shapes=[pltpu.VMEM((tb, d), BF16), pltpu.VMEM((tb, nm), F32), pltpu.VMEM((tb, LANES), F32),
                        pltpu.VMEM((tb, hv), F32), pltpu.VMEM((nh, ML_DK, ML_DV), F32),
                        pltpu.VMEM((nh, 1, ML_DK), F32), pltpu.VMEM((8, LANES), F32)],
        compiler_params=_params(("arbitrary",)),
        name="mlstm_mixer",
    )(x, _row(n1), wm, wif, wift, _row(b_if, LANES), b_if.reshape(2 * nh, 1).astype(F32),
      _row(out_norm), w_out.astype(BF16))


def _rwkv_proj_kernel(x_ref, n1_ref, mu_ref, wr_ref, wk_ref, wv_ref, w0_ref, w1_ref, w2_ref,
                      a0_ref, a1_ref, a2_ref, g1_ref, g2_ref, kk_ref, ka_ref,
                      r_ref, lw_ref, k_ref, v_ref, kkraw_ref, a_ref, g_ref, car_ref):
    tm = x_ref.shape[0]

    @pl.when(pl.program_id(0) == 0)
    def _():
        car_ref[...] = jnp.zeros_like(car_ref)

    h = _rms(x_ref[...], n1_ref[...])
    rowi = lax.broadcasted_iota(jnp.int32, h.shape, 0)
    h_prev = jnp.where(rowi == 0, car_ref[7:8, :], pltpu.roll(h, 1, 0))
    car_ref[...] = h[tm - 8:, :]
    xx = h_prev - h
    xr, xw, xk, xv, xa, xg = (h + xx * mu_ref[j:j + 1, :] for j in range(6))
    r = _bdot(xr, wr_ref[...])
    k = _bdot(xk, wk_ref[...])
    v = _bdot(xv, wv_ref[...])
    w_log = -_softplus(-(w0_ref[...] + _bdot(jnp.tanh(_bdot(xw, w1_ref[...])), w2_ref[...]))) - 0.5
    a = _sigmoid(a0_ref[...] + _bdot(_bdot(xa, a1_ref[...]), a2_ref[...]))
    g = _bdot(_sigmoid(_bdot(xg, g1_ref[...])), g2_ref[...])
    r_ref[...] = r
    lw_ref[...] = -jnp.exp(w_log)
    k_ref[...] = k * (1.0 + (a - 1.0) * ka_ref[...])
    v_ref[...] = v
    kkraw_ref[...] = k * kk_ref[...]
    a_ref[...] = a
    g_ref[...] = g


def _rwkv_proj(x, n1, mu, w_rkv, w0, w1, w2, a0, a1, a2, g1, g2, k_k, k_a, tm=512):
    s, d = x.shape

    def padc(w):
        return jnp.pad(w, ((0, 0), (0, LANES - w.shape[1]))).astype(BF16) if w.shape[1] < LANES else w.astype(BF16)

    def padr(w):
        return jnp.pad(w, ((0, LANES - w.shape[0]), (0, 0))).astype(BF16) if w.shape[0] < LANES else w.astype(BF16)

    mu8 = jnp.pad(mu, ((0, 8 - mu.shape[0]), (0, 0)))
    blk = pl.BlockSpec((tm, d), lambda i: (i, 0))
    out = jax.ShapeDtypeStruct((s, d), F32)
    return pl.pallas_call(
        _rwkv_proj_kernel,
        grid=(s // tm,),
        in_specs=[blk, _full((1, d)), _full((8, d)), _full((d, d)), _full((d, d)), _full((d, d)),
                  _full((1, d)), _full((d, LANES)), _full((LANES, d)),
                  _full((1, d)), _full((d, LANES)), _full((LANES, d)),
                  _full((d, LANES)), _full((LANES, d)), _full((1, d)), _full((1, d))],
        out_specs=[blk] * 7,
        out_shape=[out] * 7,
        scratch_shapes=[pltpu.VMEM((8, d), F32)],
        compiler_params=_params(("arbitrary",)),
        name="rwkv_proj",
    )(x, _row(n1), mu8, w_rkv[0].astype(BF16), w_rkv[1].astype(BF16), w_rkv[2].astype(BF16),
      _row(w0), padc(w1), padr(w2), _row(a0), padc(a1), padr(a2), padc(g1), padr(g2), _row(k_k), _row(k_a))


def _unit_lower_inverse(a_strict, row, col):
    n = a_strict.shape[0]
    eye = (row == col).astype(F32)
    blk16 = (row // 16) == (col // 16)
    p = jnp.where(blk16, a_strict, 0.0)
    inv = eye + p
    p = _bdot(p, p)
    for _ in range(2):
        both = _bdot(jnp.concatenate([p, inv], axis=0), p)
        p, inv = both[:n], inv + both[n:]
    inv = inv + _bdot(inv, p)
    size = 32
    prev = blk16
    while size <= CHUNK:
        cur = (row // size) == (col // size)
        off = jnp.where(cur & ~prev, a_strict, 0.0)
        inv = inv + _bdot(_bdot(inv, off), inv)
        prev = cur
        size *= 2
    return inv


def _rwkv_scan_kernel(r_ref, lw_ref, k_ref, v_ref, kkraw_ref, a_ref, rk_ref, gg_ref, gb_ref, o_ref, s_ref):
    tb = r_ref.shape[0]
    n = RWKV_HEAD
    grp = 2 * CHUNK
    big = 2 * grp

    @pl.when(pl.program_id(1) == 0)
    def _():
        s_ref[...] = jnp.zeros_like(s_ref)

    rown = lax.broadcasted_iota(jnp.int32, (grp, LANES), 0)
    head0 = lax.broadcasted_iota(jnp.int32, (1, LANES), 1) < n
    chunk0 = rown < CHUNK
    row, col = _tri_masks(big)
    same = (row // CHUNK) == (col // CHUNK)
    incl = same & (row >= col)
    strict = same & (row > col)
    tri = incl[:grp, :grp].astype(F32)
    srow = lax.broadcasted_iota(jnp.int32, (big, LANES), 0)
    slane = lax.broadcasted_iota(jnp.int32, (big, LANES), 1)
    srow_chunk0 = (srow // CHUNK) % 2 == 0
    diag = (srow % grp) == slane
    own = ((rown < n) == head0)

    def stack(t):
        return jnp.concatenate([jnp.where(head0, t, 0.0), jnp.where(head0, 0.0, t)], axis=0)

    def blockdiag(t):
        s = stack(t)
        return jnp.concatenate([jnp.where(srow_chunk0, s, 0.0), jnp.where(srow_chunk0, 0.0, s)], axis=1)

    def head_sum(t):
        return jnp.where(head0, jnp.sum(jnp.where(head0, t, 0.0), axis=-1, keepdims=True),
                         jnp.sum(jnp.where(head0, 0.0, t), axis=-1, keepdims=True))

    def per_chunk(t, c):
        return jnp.concatenate([t[c * CHUNK:(c + 1) * CHUNK], t[grp + c * CHUNK:grp + (c + 1) * CHUNK]], axis=0)

    for g in range(tb // grp):
        rows = slice(g * grp, (g + 1) * grp)
        r, lw, k, v = r_ref[rows, :], lw_ref[rows, :], k_ref[rows, :], v_ref[rows, :]
        kk, a = kkraw_ref[rows, :], a_ref[rows, :]
        kk = kk / jnp.maximum(jnp.sqrt(head_sum(kk * kk)), 1e-12)
        cum = _dot(tri, lw, precision=HIGHEST)
        cum_last = jnp.where(chunk0, cum[CHUNK - 1:CHUNK, :], cum[grp - 1:grp, :])
        e_neg, e_end = jnp.exp(-cum), jnp.exp(cum_last - cum)
        b_vec = kk * a
        a_t = stack(-kk * jnp.exp(cum - lw))
        r_t = stack(r * jnp.exp(cum))
        v_s = stack(v)
        a4 = _bdot_nt(jnp.concatenate([a_t, r_t], axis=0),
                      jnp.concatenate([stack(b_vec * e_neg), stack(k * e_neg)], axis=0))
        a_ab = jnp.where(strict, a4[:big, :big], 0.0)
        a_xk = jnp.concatenate([jnp.where(strict, a4[:big, big:], 0.0), jnp.where(incl, a4[big:, big:], 0.0)], axis=0)
        a_rb = jnp.where(incl, a4[big:, :big], 0.0)
        t_inv = _unit_lower_inverse(a_ab, row, col)
        xv = _bdot(a_xk, v_s)
        z = _bdot(t_inv, jnp.concatenate([a_t, xv[:big]], axis=1))
        gh = _bdot_tn(blockdiag(b_vec * e_end), z)
        kv = _bdot_tn(blockdiag(k * e_end), v_s)
        ry = _bdot(a_rb, z)
        p_end = jnp.exp(jnp.concatenate([cum[CHUNK - 1:CHUNK, :], cum[grp - 1:grp, :]], axis=0))
        g_all = gh[:, :LANES] + jnp.where(diag, jnp.where(srow < grp, p_end[0:1, :], p_end[1:2, :]), 0.0)
        h_all = gh[:, LANES:] + kv
        rp_all = r_t + ry[:, :LANES]
        y0_all = ry[:, LANES:] + xv[big:]
        for c in range(2):
            crow = slice(g * grp + c * CHUNK, g * grp + (c + 1) * CHUNK)
            lhs = jnp.concatenate([g_all[c * grp:(c + 1) * grp], per_chunk(rp_all, c)], axis=0)
            both = _bdot(lhs, s_ref[...])
            s_ref[...] = both[:grp] + h_all[c * grp:(c + 1) * grp]
            y = both[grp:] + per_chunk(y0_all, c)
            mean = jnp.sum(y, axis=-1, keepdims=True) * (1.0 / n)
            dev = jnp.where(own, y - mean, 0.0)
            var = jnp.sum(dev * dev, axis=-1, keepdims=True) * (1.0 / n)
            yn = dev * lax.rsqrt(var + RWKV_GN_EPS)
            yn = yn[:CHUNK] + yn[CHUNK:]
            rc, kc, vc = r_ref[crow, :], k_ref[crow, :], v_ref[crow, :]
            bonus = head_sum(rc * kc * rk_ref[...])[:CHUNK] * vc
            o_ref[crow, :] = yn * gg_ref[...] + gb_ref[...] + bonus


def _rwkv_scan(r, lw, k, v, kkraw, a, r_k, gn_g, gn_b, tb=512):
    s, d = r.shape
    tb = min(tb, s)
    blk = pl.BlockSpec((tb, LANES), lambda p, i: (i, p))
    vec = pl.BlockSpec((1, LANES), lambda p, i: (0, p))
    return pl.pallas_call(
        _rwkv_scan_kernel,
        grid=(d // LANES, s // tb),
        in_specs=[blk] * 6 + [vec] * 3,
        out_specs=blk,
        out_shape=jax.ShapeDtypeStruct((s, d), F32),
        scratch_shapes=[pltpu.VMEM((LANES, LANES), F32)],
        compiler_params=_params(("arbitrary", "arbitrary")),
        name="rwkv_scan",
    )(r, lw, k, v, kkraw, a, _row(r_k), _row(gn_g), _row(gn_b))


def _out_proj_kernel(x_ref, a_ref, g_ref, w_ref, o_ref, *, gated):
    a = a_ref[...]
    if gated:
        a = a * g_ref[...]
    o_ref[...] = x_ref[...] + _bdot(a, w_ref[...])


def _out_proj(x, a, w, gate=None, tm=1024):
    s, d = x.shape
    blk = pl.BlockSpec((tm, d), lambda i: (i, 0))
    return pl.pallas_call(
        functools.partial(_out_proj_kernel, gated=gate is not None),
        grid=(s // tm,),
        in_specs=[blk, blk, blk, _full((d, d))],
        out_specs=blk,
        out_shape=jax.ShapeDtypeStruct((s, d), F32),
        compiler_params=_params(("arbitrary",)),
        name="out_proj",
    )(x, a, gate if gate is not None else a, w.astype(BF16))


def _sb_qkv_kernel(x_ref, n1_ref, w_ref, q_ref, k_ref, v_ref):
    d = x_ref.shape[1]
    h = _rms(x_ref[...], n1_ref[...]).astype(BF16)
    qkv = _dot(h, w_ref[...])
    q_ref[...] = (qkv[:, :d] * SB_HEAD_DIM ** -0.5).astype(BF16)
    k_ref[...] = qkv[:, d:2 * d].astype(BF16)
    v_ref[...] = qkv[:, 2 * d:].astype(BF16)


def _sb_qkv(x, n1, w_qkv, tm=1024):
    s, d = x.shape
    blk = pl.BlockSpec((tm, d), lambda i: (i, 0))
    out = jax.ShapeDtypeStruct((s, d), BF16)
    return pl.pallas_call(
        _sb_qkv_kernel,
        grid=(s // tm,),
        in_specs=[blk, _full((1, d)), _full((d, 3 * d))],
        out_specs=[blk] * 3,
        out_shape=[out] * 3,
        compiler_params=_params(("arbitrary",)),
        name="sb_qkv",
    )(x, _row(n1), w_qkv.astype(BF16))


def _sb_attn_kernel(q_ref, k_ref, v_ref, o_ref, *, tq):
    qi = pl.program_id(1)
    head0 = lax.broadcasted_iota(jnp.int32, (1, LANES), 1) < SB_HEAD_DIM
    q = q_ref[...]
    qs = (jnp.where(head0, q, jnp.zeros_like(q)), jnp.where(head0, jnp.zeros_like(q), q))
    row, col = _tri_masks(tq)
    before = col < row
    later = (row > col).astype(BF16)
    later2 = jnp.concatenate([later, later], axis=0)

    def tile(kb, cs, accs, diagonal):
        ks = pl.ds(pl.multiple_of(kb * tq, tq), tq)
        kt, vt = k_ref[ks, :], v_ref[ks, :]
        out_c, out_acc = [], []
        for h in range(2):
            z = _dot_nt(qs[h], kt)
            lk = -_softplus(z)
            if diagonal:
                lk = jnp.where(before, lk, 0.0)
            hi = lk.astype(BF16)
            lo = (lk - hi.astype(F32)).astype(BF16)
            excl = _dot(jnp.concatenate([hi, lo], axis=1), later2)
            w = jnp.exp(z + lk + excl + cs[h])
            if diagonal:
                w = jnp.where(before, w, 0.0)
            out_acc.append(accs[h] + _dot(w.astype(BF16), vt))
            out_c.append(cs[h] + excl[:, 0:1] + lk[:, 0:1])
        return tuple(out_c), tuple(out_acc)

    c0 = jnp.zeros((tq, 1), F32)
    acc0 = jnp.zeros((tq, LANES), F32)
    cs, accs = tile(qi, (c0, c0), (acc0, acc0), True)

    def cond(st):
        kb, cs, _ = st
        return jnp.logical_and(kb >= 0, jnp.max(jnp.maximum(cs[0], cs[1])) > SB_ZERO_LOG)

    def body(st):
        kb, cs, accs = st
        cs, accs = tile(kb, cs, accs, False)
        return kb - 1, cs, accs

    _, _, accs = lax.while_loop(cond, body, (qi - 1, cs, accs))
    o_ref[...] = jnp.where(head0, accs[0], accs[1])


def _sb_attn(q, k, v, tq=256):
    s, d = q.shape
    return pl.pallas_call(
        functools.partial(_sb_attn_kernel, tq=tq),
        grid=(d // LANES, s // tq),
        in_specs=[pl.BlockSpec((tq, LANES), lambda p, i: (i, p)),
                  pl.BlockSpec((s, LANES), lambda p, i: (0, p)),
                  pl.BlockSpec((s, LANES), lambda p, i: (0, p))],
        out_specs=pl.BlockSpec((tq, LANES), lambda p, i: (i, p)),
        out_shape=jax.ShapeDtypeStruct((s, d), F32),
        compiler_params=_params(("arbitrary", "arbitrary")),
        name="sb_attn",
    )(q, k, v)


def _stick_breaking(x, n1, w_qkv, w_out):
    q, k, v = _sb_qkv(x, n1, w_qkv)
    return _out_proj(x, _sb_attn(q, k, v), w_out)


def _rwkv7(x, n1, mu, w_rkv, w0, w1, w2, a0, a1, a2, g1, g2, k_k, k_a, r_k, gn_g, gn_b, w_out):
    r, lw, k, v, kkraw, a, g = _rwkv_proj(x, n1, mu, w_rkv, w0, w1, w2, a0, a1, a2, g1, g2, k_k, k_a)
    o = _rwkv_scan(r, lw, k, v, kkraw, a, r_k, gn_g, gn_b)
    return _out_proj(x, o, w_out, gate=g)


def kernel(x, l0_norm1, l0_gla_w_in, l0_gla_w_alpha_up, l0_gla_b_alpha, l0_gla_out_norm, l0_gla_w_out, l0_norm2, l0_ffn_w_up, l0_ffn_conv_w, l0_ffn_conv_b, l0_ffn_w_down, l1_norm1, l1_rwkv_mu, l1_rwkv_w_rkv, l1_rwkv_w0, l1_rwkv_w1, l1_rwkv_w2, l1_rwkv_a0, l1_rwkv_a1, l1_rwkv_a2, l1_rwkv_g1, l1_rwkv_g2, l1_rwkv_k_k, l1_rwkv_k_a, l1_rwkv_r_k, l1_rwkv_gn_g, l1_rwkv_gn_b, l1_rwkv_w_out, l1_norm2, l1_ffn_w_up, l1_ffn_conv_w, l1_ffn_conv_b, l1_ffn_w_down, l2_norm1, l2_sb_w_qkv, l2_sb_w_out, l2_norm2, l2_ffn_w_up, l2_ffn_conv_w, l2_ffn_conv_b, l2_ffn_w_down, l3_norm1, l3_ml_w_in, l3_ml_b_if, l3_ml_out_norm, l3_ml_w_out, l3_norm2, l3_ffn_w_up, l3_ffn_conv_w, l3_ffn_conv_b, l3_ffn_w_down, final_norm):
    b, s, d = x.shape
    outs = []
    for bi in range(b):
        h = x[bi]
        h = _gla(h, l0_norm1, l0_gla_w_in, l0_gla_w_alpha_up, l0_gla_b_alpha, l0_gla_out_norm, l0_gla_w_out)
        h = _ffn(h, l0_norm2, l0_ffn_w_up, l0_ffn_conv_w, l0_ffn_conv_b, l0_ffn_w_down)
        h = _rwkv7(h, l1_norm1, l1_rwkv_mu, l1_rwkv_w_rkv, l1_rwkv_w0, l1_rwkv_w1, l1_rwkv_w2,
                   l1_rwkv_a0, l1_rwkv_a1, l1_rwkv_a2, l1_rwkv_g1, l1_rwkv_g2,
                   l1_rwkv_k_k, l1_rwkv_k_a, l1_rwkv_r_k, l1_rwkv_gn_g, l1_rwkv_gn_b, l1_rwkv_w_out)
        h = _ffn(h, l1_norm2, l1_ffn_w_up, l1_ffn_conv_w, l1_ffn_conv_b, l1_ffn_w_down)
        h = _stick_breaking(h, l2_norm1, l2_sb_w_qkv, l2_sb_w_out)
        h = _ffn(h, l2_norm2, l2_ffn_w_up, l2_ffn_conv_w, l2_ffn_conv_b, l2_ffn_w_down)
        h = _mlstm(h, l3_norm1, l3_ml_w_in, l3_ml_b_if, l3_ml_out_norm, l3_ml_w_out)
        h = _ffn(h, l3_norm2, l3_ffn_w_up, l3_ffn_conv_w, l3_ffn_conv_b, l3_ffn_w_down, final_g=final_norm)
        outs.append(h)
    return jnp.stack(outs, axis=0)
```

```python
import functools

import jax
import jax.numpy as jnp
from jax import lax
from jax.experimental import pallas as pl
from jax.experimental.pallas import tpu as pltpu

F32 = jnp.float32
BF16 = jnp.bfloat16

NORM_EPS = 1e-6
CHUNK = 64
GLA_HEADS, GLA_DK, GLA_DV, GLA_TAU = 4, 128, 256, 16.0
ML_HEADS, ML_DK, ML_DV = 4, 128, 256
ML_CHUNK = 256
RWKV_HEAD = 64
RWKV_GN_EPS = 64e-5
SB_HEADS, SB_HEAD_DIM = 16, 64
CONV_WIDTH = 3
LANES = 128
SB_ZERO_LOG = -104.0


def _rms(x, g, eps=NORM_EPS):
    return x * lax.rsqrt(jnp.mean(x * x, axis=-1, keepdims=True) + eps) * g


def _softplus(x):
    return jnp.maximum(x, 0.0) + jnp.log1p(jnp.exp(-jnp.abs(x)))


def _log_sigmoid(x):
    return -_softplus(-x)


def _sigmoid(x):
    return 1.0 / (1.0 + jnp.exp(-x))


def _dot(a, b, **kw):
    return jnp.dot(a, b, preferred_element_type=F32, **kw)


def _dot_nt(a, b, **kw):
    return lax.dot_general(a, b, (((1,), (1,)), ((), ())), preferred_element_type=F32, **kw)


def _dot_tn(a, b, **kw):
    return lax.dot_general(a, b, (((0,), (0,)), ((), ())), preferred_element_type=F32, **kw)


def _bdot(a, b):
    return _dot(a.astype(BF16), b.astype(BF16))


def _bdot_nt(a, b):
    return _dot_nt(a.astype(BF16), b.astype(BF16))


def _bdot_tn(a, b):
    return _dot_tn(a.astype(BF16), b.astype(BF16))


def _split3(x, axis):
    hi = x.astype(BF16)
    r1 = x - hi.astype(F32)
    mid = r1.astype(BF16)
    lo = (r1 - mid.astype(F32)).astype(BF16)
    return jnp.concatenate([hi, mid, lo], axis=axis)


def _cumsum_rows(tri3, x):
    return _dot(tri3, _split3(x, 0))


def _lockstep(gens):
    out = [None] * len(gens)
    live = list(range(len(gens)))
    while live:
        for i in list(live):
            try:
                next(gens[i])
            except StopIteration as stop:
                out[i] = stop.value
                live.remove(i)
    return out


def _tri_masks(n):
    row = lax.broadcasted_iota(jnp.int32, (n, n), 0)
    col = lax.broadcasted_iota(jnp.int32, (n, n), 1)
    return row, col


def _full(shape):
    return pl.BlockSpec(shape, lambda *_: (0,) * len(shape), pipeline_mode=pl.Buffered(1))


def _params(sem):
    return pltpu.CompilerParams(dimension_semantics=sem, vmem_limit_bytes=56 * 1024 * 1024)


def _row(v, width=None):
    v = v.reshape(1, -1).astype(F32)
    if width is not None and v.shape[1] < width:
        v = jnp.pad(v, ((0, 0), (0, width - v.shape[1])))
    return v


def _ffn_kernel(x_ref, g_ref, wup_ref, cw_ref, wd_ref, fg_ref, o_ref, h_ref, act_ref, car_ref, *, final_norm, fc):
    tm = x_ref.shape[0]
    f = wd_ref.shape[0]

    @pl.when(pl.program_id(0) == 0)
    def _():
        car_ref[...] = jnp.zeros_like(car_ref)

    x = x_ref[...]
    h_ref[...] = _rms(x, g_ref[...]).astype(BF16)
    rowi = lax.broadcasted_iota(jnp.int32, (tm, fc), 0)

    def conv_up(cols):
        u = _dot(h_ref[...], wup_ref[:, cols])
        c2, c1 = car_ref[6:7, cols], car_ref[7:8, cols]
        s1 = jnp.where(rowi == 0, c1, pltpu.roll(u, 1, 0))
        s2 = jnp.where(rowi == 0, c2, jnp.where(rowi == 1, c1, pltpu.roll(u, 2, 0)))
        car_ref[:, cols] = u[tm - 8:, :]
        return s2 * cw_ref[0:1, cols] + s1 * cw_ref[1:2, cols] + u * cw_ref[2:3, cols] + cw_ref[3:4, cols]

    for c in range(f // fc):
        gate = conv_up(slice(c * fc, (c + 1) * fc))
        up = conv_up(slice(f + c * fc, f + (c + 1) * fc))
        act_ref[:, c * fc:(c + 1) * fc] = (gate * _sigmoid(gate) * up).astype(BF16)
    y = x + _dot(act_ref[...], wd_ref[...])
    if final_norm:
        y = _rms(y, fg_ref[...])
    o_ref[...] = y


def _ffn(x, norm_g, w_up, conv_w, conv_b, w_down, final_g=None, tm=1024, fc=256):
    s, d = x.shape
    f = w_down.shape[0]
    taps = jnp.concatenate([conv_w, conv_b[None, :], jnp.zeros((4, 2 * f), F32)], axis=0)
    fg = _row(final_g if final_g is not None else jnp.ones((d,), F32))
    return pl.pallas_call(
        functools.partial(_ffn_kernel, final_norm=final_g is not None, fc=fc),
        grid=(s // tm,),
        in_specs=[pl.BlockSpec((tm, d), lambda i: (i, 0)), _full((1, d)), _full((d, 2 * f)), _full((8, 2 * f)),
                  _full((f, d)), _full((1, d))],
        out_specs=pl.BlockSpec((tm, d), lambda i: (i, 0)),
        out_shape=jax.ShapeDtypeStruct((s, d), F32),
        scratch_shapes=[pltpu.VMEM((tm, d), BF16), pltpu.VMEM((tm, f), BF16), pltpu.VMEM((8, 2 * f), F32)],
        compiler_params=_params(("arbitrary",)),
        name="conv_ffn",
    )(x, _row(norm_g), w_up.astype(BF16), taps, w_down.astype(BF16), fg)


def _gla_kernel(x_ref, n1_ref, wm_ref, wa_ref, wup_ref, ba_ref, on_ref, wo_ref, o_ref,
                proj_ref, la_ref, ob_ref, st_ref):
    tb = x_ref.shape[0]
    hk, hv = GLA_HEADS * GLA_DK, GLA_HEADS * GLA_DV

    @pl.when(pl.program_id(0) == 0)
    def _():
        st_ref[...] = jnp.zeros_like(st_ref)

    x = x_ref[...]
    h = _rms(x, n1_ref[...]).astype(BF16)
    proj_ref[...] = _dot(h, wm_ref[...])
    a_low = _dot(h, wa_ref[...])
    la_ref[...] = _log_sigmoid(_bdot(a_low, wup_ref[...]) + ba_ref[...]) * (1.0 / GLA_TAU)
    grp = 4 * CHUNK
    row, col = _tri_masks(grp)
    causal = ((row // CHUNK) == (col // CHUNK)) & (row >= col)
    tri = causal.astype(BF16)
    tri3 = jnp.concatenate([tri, tri, tri], axis=1)
    rowc = lax.broadcasted_iota(jnp.int32, (grp, hk), 0) // CHUNK
    head_of_lane = lax.broadcasted_iota(jnp.int32, (1, hk), 1) // GLA_DK
    scale = GLA_DK ** -0.5

    def stack(t):
        return jnp.concatenate([jnp.where(head_of_lane == hh, t, 0.0) for hh in range(GLA_HEADS)], axis=0)

    for g in range(tb // grp):
        r0 = g * grp
        rows = slice(r0, r0 + grp)
        b = _cumsum_rows(tri3, la_ref[rows, :])
        b_last = b[CHUNK - 1:CHUNK, :]
        for c in range(1, 4):
            b_last = jnp.where(rowc == c, b[(c + 1) * CHUNK - 1:(c + 1) * CHUNK, :], b_last)
        q_dec = proj_ref[rows, 0:hk] * scale * jnp.exp(b)
        k = proj_ref[rows, hk:2 * hk]
        k_inv = k * jnp.exp(-b)
        k_end = k * jnp.exp(b_last - b)
        scores = [jnp.where(causal, _bdot_nt(q_dec[:, hh * GLA_DK:(hh + 1) * GLA_DK],
                                             k_inv[:, hh * GLA_DK:(hh + 1) * GLA_DK]), 0.0)
                  for hh in range(GLA_HEADS)]
        upd = []
        for c in range(4):
            cr = slice(r0 + c * CHUNK, r0 + (c + 1) * CHUNK)
            v_s = jnp.concatenate([proj_ref[cr, 2 * hk + hh * GLA_DV:2 * hk + (hh + 1) * GLA_DV]
                                   for hh in range(GLA_HEADS)], axis=0)
            upd.append(_bdot_tn(v_s, stack(k_end[c * CHUNK:(c + 1) * CHUNK, :])))
        intra = [_bdot(scores[hh], proj_ref[rows, 2 * hk + hh * GLA_DV:2 * hk + (hh + 1) * GLA_DV])
                 for hh in range(GLA_HEADS)]
        st = st_ref[...]
        inter = []
        for c in range(4):
            inter.append(_bdot_nt(stack(q_dec[c * CHUNK:(c + 1) * CHUNK, :]), st))
            st = st * jnp.exp(b[(c + 1) * CHUNK - 1:(c + 1) * CHUNK, :]) + upd[c]
        st_ref[...] = st
        for hh in range(GLA_HEADS):
            ob_ref[rows, hh * GLA_DV:(hh + 1) * GLA_DV] = intra[hh] + jnp.concatenate(
                [inter[c][hh * CHUNK:(hh + 1) * CHUNK, :] for c in range(4)], axis=0)
    normed = [_rms(ob_ref[:, hh * GLA_DV:(hh + 1) * GLA_DV], on_ref[:, hh * GLA_DV:(hh + 1) * GLA_DV])
              for hh in range(GLA_HEADS)]
    r = proj_ref[:, 2 * hk + hv:2 * hk + 2 * hv]
    y = jnp.concatenate(normed, axis=-1) * (r * _sigmoid(r))
    o_ref[...] = x + _bdot(y, wo_ref[...])


def _gla(x, n1, w_in, w_alpha_up, b_alpha, out_norm, w_out, tb=512):
    s, d = x.shape
    hk, hv = GLA_HEADS * GLA_DK, GLA_HEADS * GLA_DV
    nm = 2 * hk + 2 * hv
    rank = w_alpha_up.shape[0]
    wm = w_in[:, :nm].astype(BF16)
    wa = jnp.pad(w_in[:, nm:], ((0, 0), (0, LANES - rank))).astype(BF16)
    wup = jnp.pad(w_alpha_up, ((0, LANES - rank), (0, 0))).astype(BF16)
    return pl.pallas_call(
        _gla_kernel,
        grid=(s // tb,),
        in_specs=[pl.BlockSpec((tb, d), lambda i: (i, 0)), _full((1, d)), _full((d, nm)),
                  _full((d, LANES)), _full((LANES, hk)), _full((1, hk)), _full((1, hv)), _full((hv, d))],
        out_specs=pl.BlockSpec((tb, d), lambda i: (i, 0)),
        out_shape=jax.ShapeDtypeStruct((s, d), F32),
        scratch_shapes=[pltpu.VMEM((tb, nm), F32), pltpu.VMEM((tb, hk), F32), pltpu.VMEM((tb, hv), F32),
                        pltpu.VMEM((GLA_DV, hk), F32)],
        compiler_params=_params(("arbitrary",)),
        name="gla_mixer",
    )(x, _row(n1), wm, wa, wup, _row(b_alpha), _row(out_norm), w_out.astype(BF16))


def _mlstm_kernel(x_ref, n1_ref, wm_ref, wif_ref, wift_ref, bif_ref, bift_ref, on_ref, wo_ref, o_ref,
                  h_ref, proj_ref, gc_ref, ob_ref, c_ref, n_ref, m_ref):
    tb = x_ref.shape[0]
    nh = ML_HEADS
    hk, hv = nh * ML_DK, nh * ML_DV

    @pl.when(pl.program_id(0) == 0)
    def _():
        c_ref[...] = jnp.zeros_like(c_ref)
        n_ref[...] = jnp.zeros_like(n_ref)
        m_ref[...] = jnp.zeros_like(m_ref)

    x = x_ref[...]
    h = _rms(x, n1_ref[...]).astype(BF16)
    h_ref[...] = h
    proj_ref[...] = _dot(h, wm_ref[...])
    gc_ref[...] = _dot(h, wif_ref[...]) + bif_ref[...]
    row, col = _tri_masks(ML_CHUNK)
    causal = row >= col
    tri = causal.astype(BF16)
    tri3 = jnp.concatenate([tri, tri, tri], axis=1)
    scale = ML_DK ** -0.5

    for c in range(tb // ML_CHUNK):
        rows = slice(c * ML_CHUNK, (c + 1) * ML_CHUNK)
        g_col = gc_ref[rows, :]
        g_row = _dot_nt(wift_ref[...], h_ref[rows, :]) + bift_ref[...]
        b_col = _cumsum_rows(tri3, _log_sigmoid(g_col))
        b_row = _dot_nt(_split3(_log_sigmoid(g_row), 1), tri3)
        def head(hh):
            bc, ic = b_col[:, nh + hh:nh + hh + 1], g_col[:, hh:hh + 1]
            br, ir = b_row[nh + hh:nh + hh + 1, :], g_row[hh:hh + 1, :]
            q = proj_ref[rows, hh * ML_DK:(hh + 1) * ML_DK]
            k = proj_ref[rows, hk + hh * ML_DK:hk + (hh + 1) * ML_DK] * scale
            v = proj_ref[rows, 2 * hk + hh * ML_DV:2 * hk + (hh + 1) * ML_DV]
            m_prev = m_ref[0:1, hh:hh + 1]
            d_log = jnp.where(causal, bc - br + ir, -jnp.inf)
            inter = bc + m_prev
            m_t = jnp.maximum(inter, jnp.max(d_log, axis=-1, keepdims=True))
            w_intra = jnp.exp(d_log - m_t)
            sc = jnp.exp(inter - m_t)
            c_st, n_st = c_ref[hh], n_ref[hh]
            qk = _bdot_nt(q, k)
            qc = _bdot(q, c_st)
            yield
            qk = qk * w_intra
            b_last = bc[ML_CHUNK - 1:ML_CHUNK, :]
            g_end = b_last - bc + ic
            m_new = jnp.maximum(b_last + m_prev, jnp.max(g_end, axis=0, keepdims=True))
            kw = k * jnp.exp(g_end - m_new)
            carry_scale = jnp.exp(b_last + m_prev - m_new)
            num = _bdot(qk, v) + sc * qc
            c_ref[hh] = carry_scale * c_st + _bdot_tn(kw, v)
            yield
            den = jnp.sum(qk, axis=-1, keepdims=True) + sc * jnp.sum(q * n_st, axis=-1, keepdims=True)
            ob_ref[rows, hh * ML_DV:(hh + 1) * ML_DV] = num / jnp.maximum(jnp.abs(den), jnp.exp(-m_t))
            n_ref[hh] = carry_scale * n_st + jnp.sum(kw, axis=0, keepdims=True)
            m_ref[0:1, hh:hh + 1] = m_new

        _lockstep([head(hh) for hh in range(nh)])
    normed = [_rms(ob_ref[:, hh * ML_DV:(hh + 1) * ML_DV], on_ref[:, hh * ML_DV:(hh + 1) * ML_DV])
              for hh in range(nh)]
    o_pre = proj_ref[:, 2 * hk + hv:2 * hk + 2 * hv]
    y = jnp.concatenate(normed, axis=-1) * _sigmoid(o_pre)
    o_ref[...] = x + _bdot(y, wo_ref[...])


def _mlstm(x, n1, w_in, b_if, out_norm, w_out, tb=512):
    s, d = x.shape
    nh = ML_HEADS
    hk, hv = nh * ML_DK, nh * ML_DV
    nm = 2 * hk + 2 * hv
    wm = w_in[:, :nm].astype(BF16)
    w_if = w_in[:, nm:]
    wif = jnp.pad(w_if, ((0, 0), (0, LANES - 2 * nh))).astype(BF16)
    wift = w_if.T.astype(BF16)
    return pl.pallas_call(
        _mlstm_kernel,
        grid=(s // tb,),
        in_specs=[pl.BlockSpec((tb, d), lambda i: (i, 0)), _full((1, d)), _full((d, nm)),
                  _full((d, LANES)), _full((2 * nh, d)), _full((1, LANES)), _full((2 * nh, 1)),
                  _full((1, hv)), _full((hv, d))],
        out_specs=pl.BlockSpec((tb, d), lambda i: (i, 0)),
        out_shape=jax.ShapeDtypeStruct((s, d), F32),
        scratch_shapes=[pltpu.VMEM((tb, d), BF16), pltpu.VMEM((tb, nm), F32), pltpu.VMEM((tb, LANES), F32),
                        pltpu.VMEM((tb, hv), F32), pltpu.VMEM((nh, ML_DK, ML_DV), F32),
                        pltpu.VMEM((nh, 1, ML_DK), F32), pltpu.VMEM((8, LANES), F32)],
        compiler_params=_params(("arbitrary",)),
        name="mlstm_mixer",
    )(x, _row(n1), wm, wif, wift, _row(b_if, LANES), b_if.reshape(2 * nh, 1).astype(F32),
      _row(out_norm), w_out.astype(BF16))


def _rwkv_proj_kernel(x_ref, n1_ref, mu_ref, wr_ref, wk_ref, wv_ref, w0_ref, w1_ref, w2_ref,
                      a0_ref, a1_ref, a2_ref, g1_ref, g2_ref, kk_ref, ka_ref,
                      r_ref, lw_ref, k_ref, v_ref, kkraw_ref, a_ref, g_ref, car_ref):
    tm = x_ref.shape[0]

    @pl.when(pl.program_id(0) == 0)
    def _():
        car_ref[...] = jnp.zeros_like(car_ref)

    h = _rms(x_ref[...], n1_ref[...])
    rowi = lax.broadcasted_iota(jnp.int32, h.shape, 0)
    h_prev = jnp.where(rowi == 0, car_ref[7:8, :], pltpu.roll(h, 1, 0))
    car_ref[...] = h[tm - 8:, :]
    xx = h_prev - h
    xr, xw, xk, xv, xa, xg = (h + xx * mu_ref[j:j + 1, :] for j in range(6))
    r = _bdot(xr, wr_ref[...])
    k = _bdot(xk, wk_ref[...])
    v = _bdot(xv, wv_ref[...])
    w_log = -_softplus(-(w0_ref[...] + _bdot(jnp.tanh(_bdot(xw, w1_ref[...])), w2_ref[...]))) - 0.5
    a = _sigmoid(a0_ref[...] + _bdot(_bdot(xa, a1_ref[...]), a2_ref[...]))
    g = _bdot(_sigmoid(_bdot(xg, g1_ref[...])), g2_ref[...])
    r_ref[...] = r
    lw_ref[...] = -jnp.exp(w_log)
    k_ref[...] = k * (1.0 + (a - 1.0) * ka_ref[...])
    v_ref[...] = v
    kkraw_ref[...] = k * kk_ref[...]
    a_ref[...] = a
    g_ref[...] = g


def _rwkv_proj(x, n1, mu, w_rkv, w0, w1, w2, a0, a1, a2, g1, g2, k_k, k_a, tm=512):
    s, d = x.shape

    def padc(w):
        return jnp.pad(w, ((0, 0), (0, LANES - w.shape[1]))).astype(BF16) if w.shape[1] < LANES else w.astype(BF16)

    def padr(w):
        return jnp.pad(w, ((0, LANES - w.shape[0]), (0, 0))).astype(BF16) if w.shape[0] < LANES else w.astype(BF16)

    mu8 = jnp.pad(mu, ((0, 8 - mu.shape[0]), (0, 0)))
    blk = pl.BlockSpec((tm, d), lambda i: (i, 0))
    out = jax.ShapeDtypeStruct((s, d), F32)
    return pl.pallas_call(
        _rwkv_proj_kernel,
        grid=(s // tm,),
        in_specs=[blk, _full((1, d)), _full((8, d)), _full((d, d)), _full((d, d)), _full((d, d)),
                  _full((1, d)), _full((d, LANES)), _full((LANES, d)),
                  _full((1, d)), _full((d, LANES)), _full((LANES, d)),
                  _full((d, LANES)), _full((LANES, d)), _full((1, d)), _full((1, d))],
        out_specs=[blk] * 7,
        out_shape=[out] * 7,
        scratch_shapes=[pltpu.VMEM((8, d), F32)],
        compiler_params=_params(("arbitrary",)),
        name="rwkv_proj",
    )(x, _row(n1), mu8, w_rkv[0].astype(BF16), w_rkv[1].astype(BF16), w_rkv[2].astype(BF16),
      _row(w0), padc(w1), padr(w2), _row(a0), padc(a1), padr(a2), padc(g1), padr(g2), _row(k_k), _row(k_a))


def _unit_lower_inverse(a_strict, row, col):
    n = a_strict.shape[0]
    eye = (row == col).astype(F32)
    blk16 = (row // 16) == (col // 16)
    p = jnp.where(blk16, a_strict, 0.0)
    inv = eye + p
    p = _bdot(p, p)
    yield
    for _ in range(2):
        both = _bdot(jnp.concatenate([p, inv], axis=0), p)
        yield
        p, inv = both[:n], inv + both[n:]
    inv16 = inv + _bdot(inv, p)
    yield
    blk32 = (row // 32) == (col // 32)
    off32 = jnp.where(blk32 & ~blk16, a_strict, 0.0)
    off64 = jnp.where(blk32, 0.0, a_strict)
    x = _bdot(inv16, jnp.concatenate([off32, off64], axis=1))
    yield
    x32, x64 = x[:, :n], x[:, n:]
    y = _bdot(x32, jnp.concatenate([inv16, x64], axis=1))
    yield
    inv32 = inv16 + y[:, :n]
    return inv32 + _bdot(x64 + y[:, n:], inv32)


def _rwkv_scan_kernel(r_ref, lw_ref, k_ref, v_ref, kkraw_ref, a_ref, rk_ref, gg_ref, gb_ref, o_ref, s_ref):
    tb = r_ref.shape[0]
    n = RWKV_HEAD
    grp = 2 * CHUNK
    big = 2 * grp

    @pl.when(pl.program_id(1) == 0)
    def _():
        s_ref[...] = jnp.zeros_like(s_ref)

    rown = lax.broadcasted_iota(jnp.int32, (grp, LANES), 0)
    head0 = lax.broadcasted_iota(jnp.int32, (1, LANES), 1) < n
    chunk0 = rown < CHUNK
    row, col = _tri_masks(big)
    same = (row // CHUNK) == (col // CHUNK)
    incl = same & (row >= col)
    strict = same & (row > col)
    tri = incl[:grp, :grp].astype(BF16)
    tri3 = jnp.concatenate([tri, tri, tri], axis=1)
    srow = lax.broadcasted_iota(jnp.int32, (big, LANES), 0)
    slane = lax.broadcasted_iota(jnp.int32, (big, LANES), 1)
    srow_chunk0 = (srow // CHUNK) % 2 == 0
    diag = (srow % grp) == slane
    own = ((rown < n) == head0)

    def stack(t):
        return jnp.concatenate([jnp.where(head0, t, 0.0), jnp.where(head0, 0.0, t)], axis=0)

    def blockdiag(t):
        s = stack(t)
        return jnp.concatenate([jnp.where(srow_chunk0, s, 0.0), jnp.where(srow_chunk0, 0.0, s)], axis=1)

    def spread(t):
        return jnp.concatenate([jnp.where(srow < grp, t, 0.0), jnp.where(srow < grp, 0.0, t)], axis=1)

    def head_sum(t):
        return jnp.where(head0, jnp.sum(jnp.where(head0, t, 0.0), axis=-1, keepdims=True),
                         jnp.sum(jnp.where(head0, 0.0, t), axis=-1, keepdims=True))

    def per_chunk(t, c):
        return jnp.concatenate([t[c * CHUNK:(c + 1) * CHUNK], t[grp + c * CHUNK:grp + (c + 1) * CHUNK]], axis=0)

    def phase1(g):
        rows = slice(g * grp, (g + 1) * grp)
        r, lw, k, v = r_ref[rows, :], lw_ref[rows, :], k_ref[rows, :], v_ref[rows, :]
        kk, a = kkraw_ref[rows, :], a_ref[rows, :]
        kk = kk / jnp.maximum(jnp.sqrt(head_sum(kk * kk)), 1e-12)
        cum = _cumsum_rows(tri3, lw)
        yield
        cum_last = jnp.where(chunk0, cum[CHUNK - 1:CHUNK, :], cum[grp - 1:grp, :])
        e_neg, e_end = jnp.exp(-cum), jnp.exp(cum_last - cum)
        b_vec = kk * a
        a_t = stack(-kk * jnp.exp(cum - lw))
        r_t = stack(r * jnp.exp(cum))
        v_s = stack(v)
        a4 = _bdot_nt(jnp.concatenate([a_t, r_t], axis=0),
                      jnp.concatenate([b_vec * e_neg, k * e_neg], axis=0))
        yield
        a_ab = jnp.where(strict, spread(a4[:big, :grp]), 0.0)
        a_ak = jnp.where(strict, spread(a4[:big, grp:]), 0.0)
        a_rb = jnp.where(incl, spread(a4[big:, :grp]), 0.0)
        a_rk = jnp.where(incl, spread(a4[big:, grp:]), 0.0)
        akv = _bdot(a_ak, v_s)
        t_inv = yield from _unit_lower_inverse(a_ab, row, col)
        yield
        z = _bdot(t_inv, jnp.concatenate([a_t, akv], axis=1))
        yield
        zv = jnp.concatenate([z, jnp.concatenate([jnp.zeros_like(v_s), v_s], axis=1)], axis=0)
        gh = _bdot_tn(jnp.concatenate([blockdiag(b_vec * e_end), blockdiag(k * e_end)], axis=0), zv)
        ry = _bdot(jnp.concatenate([a_rb, a_rk], axis=1), zv)
        p_end = jnp.exp(jnp.concatenate([cum[CHUNK - 1:CHUNK, :], cum[grp - 1:grp, :]], axis=0))
        g_all = gh[:, :LANES] + jnp.where(diag, jnp.where(srow < grp, p_end[0:1, :], p_end[1:2, :]), 0.0)
        h_all = gh[:, LANES:]
        rp_all = r_t + ry[:, :LANES]
        y0_all = ry[:, LANES:]
        return g_all, h_all, rp_all, y0_all

    staged = _lockstep([phase1(g) for g in range(tb // grp)])
    for g, (g_all, h_all, rp_all, y0_all) in enumerate(staged):
        for c in range(2):
            crow = slice(g * grp + c * CHUNK, g * grp + (c + 1) * CHUNK)
            lhs = jnp.concatenate([g_all[c * grp:(c + 1) * grp], per_chunk(rp_all, c)], axis=0)
            both = _bdot(lhs, s_ref[...])
            s_ref[...] = both[:grp] + h_all[c * grp:(c + 1) * grp]
            y = both[grp:] + per_chunk(y0_all, c)
            mean = jnp.sum(y, axis=-1, keepdims=True) * (1.0 / n)
            dev = jnp.where(own, y - mean, 0.0)
            var = jnp.sum(dev * dev, axis=-1, keepdims=True) * (1.0 / n)
            yn = dev * lax.rsqrt(var + RWKV_GN_EPS)
            yn = yn[:CHUNK] + yn[CHUNK:]
            rc, kc, vc = r_ref[crow, :], k_ref[crow, :], v_ref[crow, :]
            bonus = head_sum(rc * kc * rk_ref[...])[:CHUNK] * vc
            o_ref[crow, :] = yn * gg_ref[...] + gb_ref[...] + bonus


def _rwkv_scan(r, lw, k, v, kkraw, a, r_k, gn_g, gn_b, tb=1024):
    s, d = r.shape
    tb = min(tb, s)
    blk = pl.BlockSpec((tb, LANES), lambda p, i: (i, p))
    vec = pl.BlockSpec((1, LANES), lambda p, i: (0, p))
    return pl.pallas_call(
        _rwkv_scan_kernel,
        grid=(d // LANES, s // tb),
        in_specs=[blk] * 6 + [vec] * 3,
        out_specs=blk,
        out_shape=jax.ShapeDtypeStruct((s, d), F32),
        scratch_shapes=[pltpu.VMEM((LANES, LANES), F32)],
        compiler_params=_params(("arbitrary", "arbitrary")),
        name="rwkv_scan",
    )(r, lw, k, v, kkraw, a, _row(r_k), _row(gn_g), _row(gn_b))


def _out_proj_kernel(x_ref, a_ref, g_ref, w_ref, o_ref, *, gated):
    a = a_ref[...]
    if gated:
        a = a * g_ref[...]
    o_ref[...] = x_ref[...] + _bdot(a, w_ref[...])


def _out_proj(x, a, w, gate=None, tm=1024):
    s, d = x.shape
    blk = pl.BlockSpec((tm, d), lambda i: (i, 0))
    return pl.pallas_call(
        functools.partial(_out_proj_kernel, gated=gate is not None),
        grid=(s // tm,),
        in_specs=[blk, blk, blk, _full((d, d))],
        out_specs=blk,
        out_shape=jax.ShapeDtypeStruct((s, d), F32),
        compiler_params=_params(("arbitrary",)),
        name="out_proj",
    )(x, a, gate if gate is not None else a, w.astype(BF16))


def _sb_qkv_kernel(x_ref, n1_ref, w_ref, q_ref, k_ref, v_ref):
    d = x_ref.shape[1]
    h = _rms(x_ref[...], n1_ref[...]).astype(BF16)
    qkv = _dot(h, w_ref[...])
    q_ref[...] = (qkv[:, :d] * SB_HEAD_DIM ** -0.5).astype(BF16)
    k_ref[...] = qkv[:, d:2 * d].astype(BF16)
    v_ref[...] = qkv[:, 2 * d:].astype(BF16)


def _sb_qkv(x, n1, w_qkv, tm=1024):
    s, d = x.shape
    blk = pl.BlockSpec((tm, d), lambda i: (i, 0))
    out = jax.ShapeDtypeStruct((s, d), BF16)
    return pl.pallas_call(
        _sb_qkv_kernel,
        grid=(s // tm,),
        in_specs=[blk, _full((1, d)), _full((d, 3 * d))],
        out_specs=[blk] * 3,
        out_shape=[out] * 3,
        compiler_params=_params(("arbitrary",)),
        name="sb_qkv",
    )(x, _row(n1), w_qkv.astype(BF16))


def _sb_attn_kernel(q_ref, k_ref, v_ref, o_ref, *, tq):
    qi = pl.program_id(1)
    head0 = lax.broadcasted_iota(jnp.int32, (1, LANES), 1) < SB_HEAD_DIM
    q = q_ref[...]
    zero = jnp.zeros_like(q)
    q2 = jnp.concatenate([jnp.where(head0, q, zero), jnp.where(head0, zero, q)], axis=0)
    row, col = _tri_masks(tq)
    before = jnp.concatenate([col < row, col < row], axis=0)
    later = (row > col).astype(BF16)
    later2 = jnp.concatenate([later, later], axis=0)

    def tile(kb, c, acc, diagonal):
        ks = pl.ds(pl.multiple_of(kb * tq, tq), tq)
        z = _dot_nt(q2, k_ref[ks, :])
        lk = -_softplus(z)
        if diagonal:
            lk = jnp.where(before, lk, 0.0)
        hi = lk.astype(BF16)
        lo = (lk - hi.astype(F32)).astype(BF16)
        excl = _dot(jnp.concatenate([hi, lo], axis=1), later2)
        w = jnp.exp(z + lk + excl + c)
        if diagonal:
            w = jnp.where(before, w, 0.0)
        return c + excl[:, 0:1] + lk[:, 0:1], acc + _dot(w.astype(BF16), v_ref[ks, :])

    c, acc = tile(qi, jnp.zeros((2 * tq, 1), F32), jnp.zeros((2 * tq, LANES), F32), True)

    def cond(st):
        kb, c, _ = st
        return jnp.logical_and(kb >= 0, jnp.max(c) > SB_ZERO_LOG)

    def body(st):
        kb, c, acc = st
        c, acc = tile(kb, c, acc, False)
        return kb - 1, c, acc

    _, _, acc = lax.while_loop(cond, body, (qi - 1, c, acc))
    o_ref[...] = jnp.where(head0, acc[:tq], acc[tq:])


def _sb_attn(q, k, v, tq=256):
    s, d = q.shape
    return pl.pallas_call(
        functools.partial(_sb_attn_kernel, tq=tq),
        grid=(d // LANES, s // tq),
        in_specs=[pl.BlockSpec((tq, LANES), lambda p, i: (i, p)),
                  pl.BlockSpec((s, LANES), lambda p, i: (0, p)),
                  pl.BlockSpec((s, LANES), lambda p, i: (0, p))],
        out_specs=pl.BlockSpec((tq, LANES), lambda p, i: (i, p)),
        out_shape=jax.ShapeDtypeStruct((s, d), F32),
        compiler_params=_params(("arbitrary", "arbitrary")),
        name="sb_attn",
    )(q, k, v)


def _stick_breaking(x, n1, w_qkv, w_out):
    q, k, v = _sb_qkv(x, n1, w_qkv)
    return _out_proj(x, _sb_attn(q, k, v), w_out)


def _rwkv7(x, n1, mu, w_rkv, w0, w1, w2, a0, a1, a2, g1, g2, k_k, k_a, r_k, gn_g, gn_b, w_out):
    r, lw, k, v, kkraw, a, g = _rwkv_proj(x, n1, mu, w_rkv, w0, w1, w2, a0, a1, a2, g1, g2, k_k, k_a)
    o = _rwkv_scan(r, lw, k, v, kkraw, a, r_k, gn_g, gn_b)
    return _out_proj(x, o, w_out, gate=g)


def kernel(x, l0_norm1, l0_gla_w_in, l0_gla_w_alpha_up, l0_gla_b_alpha, l0_gla_out_norm, l0_gla_w_out, l0_norm2, l0_ffn_w_up, l0_ffn_conv_w, l0_ffn_conv_b, l0_ffn_w_down, l1_norm1, l1_rwkv_mu, l1_rwkv_w_rkv, l1_rwkv_w0, l1_rwkv_w1, l1_rwkv_w2, l1_rwkv_a0, l1_rwkv_a1, l1_rwkv_a2, l1_rwkv_g1, l1_rwkv_g2, l1_rwkv_k_k, l1_rwkv_k_a, l1_rwkv_r_k, l1_rwkv_gn_g, l1_rwkv_gn_b, l1_rwkv_w_out, l1_norm2, l1_ffn_w_up, l1_ffn_conv_w, l1_ffn_conv_b, l1_ffn_w_down, l2_norm1, l2_sb_w_qkv, l2_sb_w_out, l2_norm2, l2_ffn_w_up, l2_ffn_conv_w, l2_ffn_conv_b, l2_ffn_w_down, l3_norm1, l3_ml_w_in, l3_ml_b_if, l3_ml_out_norm, l3_ml_w_out, l3_norm2, l3_ffn_w_up, l3_ffn_conv_w, l3_ffn_conv_b, l3_ffn_w_down, final_norm):
    b, s, d = x.shape
    outs = []
    for bi in range(b):
        h = x[bi]
        h = _gla(h, l0_norm1, l0_gla_w_in, l0_gla_w_alpha_up, l0_gla_b_alpha, l0_gla_out_norm, l0_gla_w_out)
        h = _ffn(h, l0_norm2, l0_ffn_w_up, l0_ffn_conv_w, l0_ffn_conv_b, l0_ffn_w_down)
        h = _rwkv7(h, l1_norm1, l1_rwkv_mu, l1_rwkv_w_rkv, l1_rwkv_w0, l1_rwkv_w1, l1_rwkv_w2,
                   l1_rwkv_a0, l1_rwkv_a1, l1_rwkv_a2, l1_rwkv_g1, l1_rwkv_g2,
                   l1_rwkv_k_k, l1_rwkv_k_a, l1_rwkv_r_k, l1_rwkv_gn_g, l1_rwkv_gn_b, l1_rwkv_w_out)
        h = _ffn(h, l1_norm2, l1_ffn_w_up, l1_ffn_conv_w, l1_ffn_conv_b, l1_ffn_w_down)
        h = _stick_breaking(h, l2_norm1, l2_sb_w_qkv, l2_sb_w_out)
        h = _ffn(h, l2_norm2, l2_ffn_w_up, l2_ffn_conv_w, l2_ffn_conv_b, l2_ffn_w_down)
        h = _mlstm(h, l3_norm1, l3_ml_w_in, l3_ml_b_if, l3_ml_out_norm, l3_ml_w_out)
        h = _ffn(h, l3_norm2, l3_ffn_w_up, l3_ffn_conv_w, l3_ffn_conv_b, l3_ffn_w_down, final_g=final_norm)
        outs.append(h)
    return jnp.stack(outs, axis=0)
```

```python
import functools

import jax
import jax.numpy as jnp
from jax import lax
from jax.experimental import pallas as pl
from jax.experimental.pallas import tpu as pltpu

F32 = jnp.float32
BF16 = jnp.bfloat16

NORM_EPS = 1e-6
CHUNK = 64
GLA_HEADS, GLA_DK, GLA_DV, GLA_TAU = 4, 128, 256, 16.0
ML_HEADS, ML_DK, ML_DV = 4, 128, 256
ML_CHUNK = 256
RWKV_HEAD = 64
RWKV_GN_EPS = 64e-5
SB_HEADS, SB_HEAD_DIM = 16, 64
CONV_WIDTH = 3
LANES = 128
SB_ZERO_LOG = -104.0


def _rms(x, g, eps=NORM_EPS):
    return x * lax.rsqrt(jnp.mean(x * x, axis=-1, keepdims=True) + eps) * g


def _softplus(x):
    return jnp.maximum(x, 0.0) + jnp.log1p(jnp.exp(-jnp.abs(x)))


def _log_sigmoid(x):
    return -_softplus(-x)


def _sigmoid(x):
    return 1.0 / (1.0 + jnp.exp(-x))


def _dot(a, b, **kw):
    return jnp.dot(a, b, preferred_element_type=F32, **kw)


def _dot_nt(a, b, **kw):
    return lax.dot_general(a, b, (((1,), (1,)), ((), ())), preferred_element_type=F32, **kw)


def _dot_tn(a, b, **kw):
    return lax.dot_general(a, b, (((0,), (0,)), ((), ())), preferred_element_type=F32, **kw)


def _bdot(a, b):
    return _dot(a.astype(BF16), b.astype(BF16))


def _bdot_nt(a, b):
    return _dot_nt(a.astype(BF16), b.astype(BF16))


def _bdot_tn(a, b):
    return _dot_tn(a.astype(BF16), b.astype(BF16))


def _split3(x, axis):
    hi = x.astype(BF16)
    r1 = x - hi.astype(F32)
    mid = r1.astype(BF16)
    lo = (r1 - mid.astype(F32)).astype(BF16)
    return jnp.concatenate([hi, mid, lo], axis=axis)


def _cumsum_rows(tri3, x):
    return _dot(tri3, _split3(x, 0))


def _lockstep(gens):
    out = [None] * len(gens)
    live = list(range(len(gens)))
    while live:
        for i in list(live):
            try:
                next(gens[i])
            except StopIteration as stop:
                out[i] = stop.value
                live.remove(i)
    return out


def _tri_masks(n):
    row = lax.broadcasted_iota(jnp.int32, (n, n), 0)
    col = lax.broadcasted_iota(jnp.int32, (n, n), 1)
    return row, col


def _full(shape):
    return pl.BlockSpec(shape, lambda *_: (0,) * len(shape), pipeline_mode=pl.Buffered(1))


def _params(sem):
    return pltpu.CompilerParams(dimension_semantics=sem, vmem_limit_bytes=56 * 1024 * 1024)


def _row(v, width=None):
    v = v.reshape(1, -1).astype(F32)
    if width is not None and v.shape[1] < width:
        v = jnp.pad(v, ((0, 0), (0, width - v.shape[1])))
    return v


def _ffn_kernel(x_ref, g_ref, wup_ref, cw_ref, wd_ref, fg_ref, o_ref, h_ref, act_ref, car_ref, *, final_norm, fc):
    tm = x_ref.shape[0]
    f = wd_ref.shape[0]

    @pl.when(pl.program_id(0) == 0)
    def _():
        car_ref[...] = jnp.zeros_like(car_ref)

    x = x_ref[...]
    h_ref[...] = _rms(x, g_ref[...]).astype(BF16)
    rowi = lax.broadcasted_iota(jnp.int32, (tm, fc), 0)

    def conv_up(cols):
        u = _dot(h_ref[...], wup_ref[:, cols])
        c2, c1 = car_ref[6:7, cols], car_ref[7:8, cols]
        s1 = jnp.where(rowi == 0, c1, pltpu.roll(u, 1, 0))
        s2 = jnp.where(rowi == 0, c2, jnp.where(rowi == 1, c1, pltpu.roll(u, 2, 0)))
        car_ref[:, cols] = u[tm - 8:, :]
        return s2 * cw_ref[0:1, cols] + s1 * cw_ref[1:2, cols] + u * cw_ref[2:3, cols] + cw_ref[3:4, cols]

    for c in range(f // fc):
        gate = conv_up(slice(c * fc, (c + 1) * fc))
        up = conv_up(slice(f + c * fc, f + (c + 1) * fc))
        act_ref[:, c * fc:(c + 1) * fc] = (gate * _sigmoid(gate) * up).astype(BF16)
    y = x + _dot(act_ref[...], wd_ref[...])
    if final_norm:
        y = _rms(y, fg_ref[...])
    o_ref[...] = y


def _ffn(x, norm_g, w_up, conv_w, conv_b, w_down, final_g=None, tm=1024, fc=256):
    s, d = x.shape
    f = w_down.shape[0]
    taps = jnp.concatenate([conv_w, conv_b[None, :], jnp.zeros((4, 2 * f), F32)], axis=0)
    fg = _row(final_g if final_g is not None else jnp.ones((d,), F32))
    return pl.pallas_call(
        functools.partial(_ffn_kernel, final_norm=final_g is not None, fc=fc),
        grid=(s // tm,),
        in_specs=[pl.BlockSpec((tm, d), lambda i: (i, 0)), _full((1, d)), _full((d, 2 * f)), _full((8, 2 * f)),
                  _full((f, d)), _full((1, d))],
        out_specs=pl.BlockSpec((tm, d), lambda i: (i, 0)),
        out_shape=jax.ShapeDtypeStruct((s, d), F32),
        scratch_shapes=[pltpu.VMEM((tm, d), BF16), pltpu.VMEM((tm, f), BF16), pltpu.VMEM((8, 2 * f), F32)],
        compiler_params=_params(("arbitrary",)),
        name="conv_ffn",
    )(x, _row(norm_g), w_up.astype(BF16), taps, w_down.astype(BF16), fg)


def _gla_kernel(x_ref, n1_ref, wm_ref, wa_ref, wup_ref, ba_ref, on_ref, wo_ref, o_ref,
                proj_ref, la_ref, ob_ref, st_ref):
    tb = x_ref.shape[0]
    hk, hv = GLA_HEADS * GLA_DK, GLA_HEADS * GLA_DV

    @pl.when(pl.program_id(0) == 0)
    def _():
        st_ref[...] = jnp.zeros_like(st_ref)

    x = x_ref[...]
    h = _rms(x, n1_ref[...]).astype(BF16)
    proj_ref[...] = _dot(h, wm_ref[...])
    a_low = _dot(h, wa_ref[...])
    la_ref[...] = _log_sigmoid(_bdot(a_low, wup_ref[...]) + ba_ref[...]) * (1.0 / GLA_TAU)
    grp = 4 * CHUNK
    row, col = _tri_masks(grp)
    causal = ((row // CHUNK) == (col // CHUNK)) & (row >= col)
    tri = causal.astype(BF16)
    tri3 = jnp.concatenate([tri, tri, tri], axis=1)
    rowc = lax.broadcasted_iota(jnp.int32, (grp, hk), 0) // CHUNK
    head_of_lane = lax.broadcasted_iota(jnp.int32, (1, hk), 1) // GLA_DK
    scale = GLA_DK ** -0.5

    def stack(t):
        return jnp.concatenate([jnp.where(head_of_lane == hh, t, 0.0) for hh in range(GLA_HEADS)], axis=0)

    for g in range(tb // grp):
        r0 = g * grp
        rows = slice(r0, r0 + grp)
        b = _cumsum_rows(tri3, la_ref[rows, :])
        b_last = b[CHUNK - 1:CHUNK, :]
        for c in range(1, 4):
            b_last = jnp.where(rowc == c, b[(c + 1) * CHUNK - 1:(c + 1) * CHUNK, :], b_last)
        q_dec = proj_ref[rows, 0:hk] * scale * jnp.exp(b)
        k = proj_ref[rows, hk:2 * hk]
        k_inv = k * jnp.exp(-b)
        k_end = k * jnp.exp(b_last - b)
        scores = [jnp.where(causal, _bdot_nt(q_dec[:, hh * GLA_DK:(hh + 1) * GLA_DK],
                                             k_inv[:, hh * GLA_DK:(hh + 1) * GLA_DK]), 0.0)
                  for hh in range(GLA_HEADS)]
        upd = []
        for c in range(4):
            cr = slice(r0 + c * CHUNK, r0 + (c + 1) * CHUNK)
            v_s = jnp.concatenate([proj_ref[cr, 2 * hk + hh * GLA_DV:2 * hk + (hh + 1) * GLA_DV]
                                   for hh in range(GLA_HEADS)], axis=0)
            upd.append(_bdot_tn(v_s, stack(k_end[c * CHUNK:(c + 1) * CHUNK, :])))
        intra = [_bdot(scores[hh], proj_ref[rows, 2 * hk + hh * GLA_DV:2 * hk + (hh + 1) * GLA_DV])
                 for hh in range(GLA_HEADS)]
        st = st_ref[...]
        inter = []
        for c in range(4):
            inter.append(_bdot_nt(stack(q_dec[c * CHUNK:(c + 1) * CHUNK, :]), st))
            st = st * jnp.exp(b[(c + 1) * CHUNK - 1:(c + 1) * CHUNK, :]) + upd[c]
        st_ref[...] = st
        for hh in range(GLA_HEADS):
            ob_ref[rows, hh * GLA_DV:(hh + 1) * GLA_DV] = intra[hh] + jnp.concatenate(
                [inter[c][hh * CHUNK:(hh + 1) * CHUNK, :] for c in range(4)], axis=0)
    normed = [_rms(ob_ref[:, hh * GLA_DV:(hh + 1) * GLA_DV], on_ref[:, hh * GLA_DV:(hh + 1) * GLA_DV])
              for hh in range(GLA_HEADS)]
    r = proj_ref[:, 2 * hk + hv:2 * hk + 2 * hv]
    y = jnp.concatenate(normed, axis=-1) * (r * _sigmoid(r))
    o_ref[...] = x + _bdot(y, wo_ref[...])


def _gla(x, n1, w_in, w_alpha_up, b_alpha, out_norm, w_out, tb=512):
    s, d = x.shape
    hk, hv = GLA_HEADS * GLA_DK, GLA_HEADS * GLA_DV
    nm = 2 * hk + 2 * hv
    rank = w_alpha_up.shape[0]
    wm = w_in[:, :nm].astype(BF16)
    wa = jnp.pad(w_in[:, nm:], ((0, 0), (0, LANES - rank))).astype(BF16)
    wup = jnp.pad(w_alpha_up, ((0, LANES - rank), (0, 0))).astype(BF16)
    return pl.pallas_call(
        _gla_kernel,
        grid=(s // tb,),
        in_specs=[pl.BlockSpec((tb, d), lambda i: (i, 0)), _full((1, d)), _full((d, nm)),
                  _full((d, LANES)), _full((LANES, hk)), _full((1, hk)), _full((1, hv)), _full((hv, d))],
        out_specs=pl.BlockSpec((tb, d), lambda i: (i, 0)),
        out_shape=jax.ShapeDtypeStruct((s, d), F32),
        scratch_shapes=[pltpu.VMEM((tb, nm), F32), pltpu.VMEM((tb, hk), F32), pltpu.VMEM((tb, hv), F32),
                        pltpu.VMEM((GLA_DV, hk), F32)],
        compiler_params=_params(("arbitrary",)),
        name="gla_mixer",
    )(x, _row(n1), wm, wa, wup, _row(b_alpha), _row(out_norm), w_out.astype(BF16))


def _mlstm_kernel(x_ref, n1_ref, wm_ref, wif_ref, wift_ref, bif_ref, bift_ref, on_ref, wo_ref, o_ref,
                  h_ref, proj_ref, gc_ref, ob_ref, c_ref, n_ref, m_ref):
    tb = x_ref.shape[0]
    nh = ML_HEADS
    hk, hv = nh * ML_DK, nh * ML_DV

    @pl.when(pl.program_id(0) == 0)
    def _():
        c_ref[...] = jnp.zeros_like(c_ref)
        n_ref[...] = jnp.zeros_like(n_ref)
        m_ref[...] = jnp.zeros_like(m_ref)

    x = x_ref[...]
    h = _rms(x, n1_ref[...]).astype(BF16)
    h_ref[...] = h
    proj_ref[...] = _dot(h, wm_ref[...])
    gc_ref[...] = _dot(h, wif_ref[...]) + bif_ref[...]
    row, col = _tri_masks(ML_CHUNK)
    causal = row >= col
    tri = causal.astype(BF16)
    tri3 = jnp.concatenate([tri, tri, tri], axis=1)
    scale = ML_DK ** -0.5

    for c in range(tb // ML_CHUNK):
        rows = slice(c * ML_CHUNK, (c + 1) * ML_CHUNK)
        g_col = gc_ref[rows, :]
        g_row = _dot_nt(wift_ref[...], h_ref[rows, :]) + bift_ref[...]
        b_col = _cumsum_rows(tri3, _log_sigmoid(g_col))
        b_row = _dot_nt(_split3(_log_sigmoid(g_row), 1), tri3)
        def head(hh):
            bc, ic = b_col[:, nh + hh:nh + hh + 1], g_col[:, hh:hh + 1]
            br, ir = b_row[nh + hh:nh + hh + 1, :], g_row[hh:hh + 1, :]
            q = proj_ref[rows, hh * ML_DK:(hh + 1) * ML_DK]
            k = proj_ref[rows, hk + hh * ML_DK:hk + (hh + 1) * ML_DK] * scale
            v = proj_ref[rows, 2 * hk + hh * ML_DV:2 * hk + (hh + 1) * ML_DV]
            m_prev = m_ref[0:1, hh:hh + 1]
            d_log = jnp.where(causal, bc - br + ir, -jnp.inf)
            inter = bc + m_prev
            m_t = jnp.maximum(inter, jnp.max(d_log, axis=-1, keepdims=True))
            w_intra = jnp.exp(d_log - m_t)
            sc = jnp.exp(inter - m_t)
            c_st, n_st = c_ref[hh], n_ref[hh]
            qk = _bdot_nt(q, k)
            qc = _bdot(q, c_st)
            yield
            qk = qk * w_intra
            b_last = bc[ML_CHUNK - 1:ML_CHUNK, :]
            g_end = b_last - bc + ic
            m_new = jnp.maximum(b_last + m_prev, jnp.max(g_end, axis=0, keepdims=True))
            kw = k * jnp.exp(g_end - m_new)
            carry_scale = jnp.exp(b_last + m_prev - m_new)
            num = _bdot(qk, v) + sc * qc
            c_ref[hh] = carry_scale * c_st + _bdot_tn(kw, v)
            yield
            den = jnp.sum(qk, axis=-1, keepdims=True) + sc * jnp.sum(q * n_st, axis=-1, keepdims=True)
            ob_ref[rows, hh * ML_DV:(hh + 1) * ML_DV] = num / jnp.maximum(jnp.abs(den), jnp.exp(-m_t))
            n_ref[hh] = carry_scale * n_st + jnp.sum(kw, axis=0, keepdims=True)
            m_ref[0:1, hh:hh + 1] = m_new

        _lockstep([head(hh) for hh in range(nh)])
    normed = [_rms(ob_ref[:, hh * ML_DV:(hh + 1) * ML_DV], on_ref[:, hh * ML_DV:(hh + 1) * ML_DV])
              for hh in range(nh)]
    o_pre = proj_ref[:, 2 * hk + hv:2 * hk + 2 * hv]
    y = jnp.concatenate(normed, axis=-1) * _sigmoid(o_pre)
    o_ref[...] = x + _bdot(y, wo_ref[...])


def _mlstm(x, n1, w_in, b_if, out_norm, w_out, tb=512):
    s, d = x.shape
    nh = ML_HEADS
    hk, hv = nh * ML_DK, nh * ML_DV
    nm = 2 * hk + 2 * hv
    wm = w_in[:, :nm].astype(BF16)
    w_if = w_in[:, nm:]
    wif = jnp.pad(w_if, ((0, 0), (0, LANES - 2 * nh))).astype(BF16)
    wift = w_if.T.astype(BF16)
    return pl.pallas_call(
        _mlstm_kernel,
        grid=(s // tb,),
        in_specs=[pl.BlockSpec((tb, d), lambda i: (i, 0)), _full((1, d)), _full((d, nm)),
                  _full((d, LANES)), _full((2 * nh, d)), _full((1, LANES)), _full((2 * nh, 1)),
                  _full((1, hv)), _full((hv, d))],
        out_specs=pl.BlockSpec((tb, d), lambda i: (i, 0)),
        out_shape=jax.ShapeDtypeStruct((s, d), F32),
        scratch_shapes=[pltpu.VMEM((tb, d), BF16), pltpu.VMEM((tb, nm), F32), pltpu.VMEM((tb, LANES), F32),
                        pltpu.VMEM((tb, hv), F32), pltpu.VMEM((nh, ML_DK, ML_DV), F32),
                        pltpu.VMEM((nh, 1, ML_DK), F32), pltpu.VMEM((8, LANES), F32)],
        compiler_params=_params(("arbitrary",)),
        name="mlstm_mixer",
    )(x, _row(n1), wm, wif, wift, _row(b_if, LANES), b_if.reshape(2 * nh, 1).astype(F32),
      _row(out_norm), w_out.astype(BF16))


def _rwkv_proj_kernel(x_ref, n1_ref, mu_ref, wr_ref, wk_ref, wv_ref, w0_ref, w1_ref, w2_ref,
                      a0_ref, a1_ref, a2_ref, g1_ref, g2_ref, kk_ref, ka_ref,
                      r_ref, lw_ref, k_ref, v_ref, kkraw_ref, a_ref, g_ref, car_ref):
    tm = x_ref.shape[0]

    @pl.when(pl.program_id(0) == 0)
    def _():
        car_ref[...] = jnp.zeros_like(car_ref)

    h = _rms(x_ref[...], n1_ref[...])
    rowi = lax.broadcasted_iota(jnp.int32, h.shape, 0)
    h_prev = jnp.where(rowi == 0, car_ref[7:8, :], pltpu.roll(h, 1, 0))
    car_ref[...] = h[tm - 8:, :]
    xx = h_prev - h
    xr, xw, xk, xv, xa, xg = (h + xx * mu_ref[j:j + 1, :] for j in range(6))
    r = _bdot(xr, wr_ref[...])
    k = _bdot(xk, wk_ref[...])
    v = _bdot(xv, wv_ref[...])
    w_log = -_softplus(-(w0_ref[...] + _bdot(jnp.tanh(_bdot(xw, w1_ref[...])), w2_ref[...]))) - 0.5
    a = _sigmoid(a0_ref[...] + _bdot(_bdot(xa, a1_ref[...]), a2_ref[...]))
    g = _bdot(_sigmoid(_bdot(xg, g1_ref[...])), g2_ref[...])
    r_ref[...] = r
    lw_ref[...] = -jnp.exp(w_log)
    k_ref[...] = k * (1.0 + (a - 1.0) * ka_ref[...])
    v_ref[...] = v
    kkraw_ref[...] = k * kk_ref[...]
    a_ref[...] = a
    g_ref[...] = g


def _rwkv_proj(x, n1, mu, w_rkv, w0, w1, w2, a0, a1, a2, g1, g2, k_k, k_a, tm=512):
    s, d = x.shape

    def padc(w):
        return jnp.pad(w, ((0, 0), (0, LANES - w.shape[1]))).astype(BF16) if w.shape[1] < LANES else w.astype(BF16)

    def padr(w):
        return jnp.pad(w, ((0, LANES - w.shape[0]), (0, 0))).astype(BF16) if w.shape[0] < LANES else w.astype(BF16)

    mu8 = jnp.pad(mu, ((0, 8 - mu.shape[0]), (0, 0)))
    blk = pl.BlockSpec((tm, d), lambda i: (i, 0))
    out = jax.ShapeDtypeStruct((s, d), F32)
    return pl.pallas_call(
        _rwkv_proj_kernel,
        grid=(s // tm,),
        in_specs=[blk, _full((1, d)), _full((8, d)), _full((d, d)), _full((d, d)), _full((d, d)),
                  _full((1, d)), _full((d, LANES)), _full((LANES, d)),
                  _full((1, d)), _full((d, LANES)), _full((LANES, d)),
                  _full((d, LANES)), _full((LANES, d)), _full((1, d)), _full((1, d))],
        out_specs=[blk] * 7,
        out_shape=[out] * 7,
        scratch_shapes=[pltpu.VMEM((8, d), F32)],
        compiler_params=_params(("arbitrary",)),
        name="rwkv_proj",
    )(x, _row(n1), mu8, w_rkv[0].astype(BF16), w_rkv[1].astype(BF16), w_rkv[2].astype(BF16),
      _row(w0), padc(w1), padr(w2), _row(a0), padc(a1), padr(a2), padc(g1), padr(g2), _row(k_k), _row(k_a))


def _unit_lower_inverse(a_strict, row, col):
    n = a_strict.shape[0]
    eye = (row == col).astype(F32)
    blk16 = (row // 16) == (col // 16)
    p = jnp.where(blk16, a_strict, 0.0)
    inv = eye + p
    p = _bdot(p, p)
    yield
    for _ in range(2):
        both = _bdot(jnp.concatenate([p, inv], axis=0), p)
        yield
        p, inv = both[:n], inv + both[n:]
    inv16 = inv + _bdot(inv, p)
    yield
    blk32 = (row // 32) == (col // 32)
    off32 = jnp.where(blk32 & ~blk16, a_strict, 0.0)
    off64 = jnp.where(blk32, 0.0, a_strict)
    x = _bdot(inv16, jnp.concatenate([off32, off64], axis=1))
    yield
    x32, x64 = x[:, :n], x[:, n:]
    y = _bdot(x32, jnp.concatenate([inv16, x64], axis=1))
    yield
    inv32 = inv16 + y[:, :n]
    return inv32 + _bdot(x64 + y[:, n:], inv32)


def _rwkv_scan_kernel(r_ref, lw_ref, k_ref, v_ref, kkraw_ref, a_ref, g_ref, rk_ref, gg_ref, gb_ref, o_ref, s_ref):
    tb = r_ref.shape[0]
    n = RWKV_HEAD
    grp = 2 * CHUNK
    big = 2 * grp

    @pl.when(pl.program_id(1) == 0)
    def _():
        s_ref[...] = jnp.zeros_like(s_ref)

    rown = lax.broadcasted_iota(jnp.int32, (grp, LANES), 0)
    head0 = lax.broadcasted_iota(jnp.int32, (1, LANES), 1) < n
    chunk0 = rown < CHUNK
    row, col = _tri_masks(big)
    same = (row // CHUNK) == (col // CHUNK)
    incl = same & (row >= col)
    strict = same & (row > col)
    tri = incl[:grp, :grp].astype(BF16)
    tri3 = jnp.concatenate([tri, tri, tri], axis=1)
    srow = lax.broadcasted_iota(jnp.int32, (big, LANES), 0)
    slane = lax.broadcasted_iota(jnp.int32, (big, LANES), 1)
    srow_chunk0 = (srow // CHUNK) % 2 == 0
    diag = (srow % grp) == slane
    own = ((rown < n) == head0)

    def stack(t):
        return jnp.concatenate([jnp.where(head0, t, 0.0), jnp.where(head0, 0.0, t)], axis=0)

    def blockdiag(t):
        s = stack(t)
        return jnp.concatenate([jnp.where(srow_chunk0, s, 0.0), jnp.where(srow_chunk0, 0.0, s)], axis=1)

    def spread(t):
        return jnp.concatenate([jnp.where(srow < grp, t, 0.0), jnp.where(srow < grp, 0.0, t)], axis=1)

    def head_sum(t):
        return jnp.where(head0, jnp.sum(jnp.where(head0, t, 0.0), axis=-1, keepdims=True),
                         jnp.sum(jnp.where(head0, 0.0, t), axis=-1, keepdims=True))

    def per_chunk(t, c):
        return jnp.concatenate([t[c * CHUNK:(c + 1) * CHUNK], t[grp + c * CHUNK:grp + (c + 1) * CHUNK]], axis=0)

    def phase1(g):
        rows = slice(g * grp, (g + 1) * grp)
        r, lw, k, v = r_ref[rows, :], lw_ref[rows, :], k_ref[rows, :], v_ref[rows, :]
        kk, a = kkraw_ref[rows, :], a_ref[rows, :]
        kk = kk / jnp.maximum(jnp.sqrt(head_sum(kk * kk)), 1e-12)
        cum = _cumsum_rows(tri3, lw)
        yield
        cum_last = jnp.where(chunk0, cum[CHUNK - 1:CHUNK, :], cum[grp - 1:grp, :])
        e_neg, e_end = jnp.exp(-cum), jnp.exp(cum_last - cum)
        b_vec = kk * a
        a_t = stack(-kk * jnp.exp(cum - lw))
        r_t = stack(r * jnp.exp(cum))
        v_s = stack(v)
        a4 = _bdot_nt(jnp.concatenate([a_t, r_t], axis=0),
                      jnp.concatenate([b_vec * e_neg, k * e_neg], axis=0))
        yield
        a_ab = jnp.where(strict, spread(a4[:big, :grp]), 0.0)
        a_ak = jnp.where(strict, spread(a4[:big, grp:]), 0.0)
        a_rb = jnp.where(incl, spread(a4[big:, :grp]), 0.0)
        a_rk = jnp.where(incl, spread(a4[big:, grp:]), 0.0)
        akv = _bdot(a_ak, v_s)
        t_inv = yield from _unit_lower_inverse(a_ab, row, col)
        yield
        z = _bdot(t_inv, jnp.concatenate([a_t, akv], axis=1))
        yield
        zv = jnp.concatenate([z, jnp.concatenate([jnp.zeros_like(v_s), v_s], axis=1)], axis=0)
        gh = _bdot_tn(jnp.concatenate([blockdiag(b_vec * e_end), blockdiag(k * e_end)], axis=0), zv)
        ry = _bdot(jnp.concatenate([a_rb, a_rk], axis=1), zv)
        p_end = jnp.exp(jnp.concatenate([cum[CHUNK - 1:CHUNK, :], cum[grp - 1:grp, :]], axis=0))
        g_all = gh[:, :LANES] + jnp.where(diag, jnp.where(srow < grp, p_end[0:1, :], p_end[1:2, :]), 0.0)
        h_all = gh[:, LANES:]
        rp_all = r_t + ry[:, :LANES]
        y0_all = ry[:, LANES:]
        return g_all, h_all, rp_all, y0_all

    staged = _lockstep([phase1(g) for g in range(tb // grp)])
    for g, (g_all, h_all, rp_all, y0_all) in enumerate(staged):
        for c in range(2):
            crow = slice(g * grp + c * CHUNK, g * grp + (c + 1) * CHUNK)
            lhs = jnp.concatenate([g_all[c * grp:(c + 1) * grp], per_chunk(rp_all, c)], axis=0)
            both = _bdot(lhs, s_ref[...])
            s_ref[...] = both[:grp] + h_all[c * grp:(c + 1) * grp]
            y = both[grp:] + per_chunk(y0_all, c)
            mean = jnp.sum(y, axis=-1, keepdims=True) * (1.0 / n)
            dev = jnp.where(own, y - mean, 0.0)
            var = jnp.sum(dev * dev, axis=-1, keepdims=True) * (1.0 / n)
            yn = dev * lax.rsqrt(var + RWKV_GN_EPS)
            yn = yn[:CHUNK] + yn[CHUNK:]
            rc, kc, vc = r_ref[crow, :], k_ref[crow, :], v_ref[crow, :]
            bonus = head_sum(rc * kc * rk_ref[...])[:CHUNK] * vc
            o_ref[crow, :] = ((yn * gg_ref[...] + gb_ref[...] + bonus) * g_ref[crow, :]).astype(BF16)


def _rwkv_scan(r, lw, k, v, kkraw, a, g, r_k, gn_g, gn_b, tb=1024):
    s, d = r.shape
    tb = min(tb, s)
    blk = pl.BlockSpec((tb, LANES), lambda p, i: (i, p))
    vec = pl.BlockSpec((1, LANES), lambda p, i: (0, p))
    return pl.pallas_call(
        _rwkv_scan_kernel,
        grid=(d // LANES, s // tb),
        in_specs=[blk] * 7 + [vec] * 3,
        out_specs=blk,
        out_shape=jax.ShapeDtypeStruct((s, d), BF16),
        scratch_shapes=[pltpu.VMEM((LANES, LANES), F32)],
        compiler_params=_params(("arbitrary", "arbitrary")),
        name="rwkv_scan",
    )(r, lw, k, v, kkraw, a, g, _row(r_k), _row(gn_g), _row(gn_b))


def _out_proj_kernel(x_ref, a_ref, w_ref, o_ref):
    o_ref[...] = x_ref[...] + _dot(a_ref[...], w_ref[...])


def _out_proj(x, a, w, tm=1024):
    s, d = x.shape
    blk = pl.BlockSpec((tm, d), lambda i: (i, 0))
    return pl.pallas_call(
        _out_proj_kernel,
        grid=(s // tm,),
        in_specs=[blk, blk, _full((d, d))],
        out_specs=blk,
        out_shape=jax.ShapeDtypeStruct((s, d), F32),
        compiler_params=_params(("arbitrary",)),
        name="out_proj",
    )(x, a, w.astype(BF16))


def _sb_qkv_kernel(x_ref, n1_ref, w_ref, q_ref, k_ref, v_ref):
    d = x_ref.shape[1]
    h = _rms(x_ref[...], n1_ref[...]).astype(BF16)
    qkv = _dot(h, w_ref[...])
    q_ref[...] = (qkv[:, :d] * SB_HEAD_DIM ** -0.5).astype(BF16)
    k_ref[...] = qkv[:, d:2 * d].astype(BF16)
    v_ref[...] = qkv[:, 2 * d:].astype(BF16)


def _sb_qkv(x, n1, w_qkv, tm=1024):
    s, d = x.shape
    blk = pl.BlockSpec((tm, d), lambda i: (i, 0))
    out = jax.ShapeDtypeStruct((s, d), BF16)
    return pl.pallas_call(
        _sb_qkv_kernel,
        grid=(s // tm,),
        in_specs=[blk, _full((1, d)), _full((d, 3 * d))],
        out_specs=[blk] * 3,
        out_shape=[out] * 3,
        compiler_params=_params(("arbitrary",)),
        name="sb_qkv",
    )(x, _row(n1), w_qkv.astype(BF16))


def _sb_attn_kernel(q_ref, k_ref, v_ref, o_ref, *, tq):
    qi = pl.program_id(1)
    head0 = lax.broadcasted_iota(jnp.int32, (1, LANES), 1) < SB_HEAD_DIM
    q = q_ref[...]
    zero = jnp.zeros_like(q)
    q2 = jnp.concatenate([jnp.where(head0, q, zero), jnp.where(head0, zero, q)], axis=0)
    row, col = _tri_masks(tq)
    before = jnp.concatenate([col < row, col < row], axis=0)
    later = (row > col).astype(BF16)
    later2 = jnp.concatenate([later, later], axis=0)

    def pair(kb, c, acc, diagonal):
        valid = kb >= 1
        ks = (pl.ds(pl.multiple_of(kb * tq, tq), tq), pl.ds(pl.multiple_of(jnp.maximum(kb - 1, 0) * tq, tq), tq))
        zs = [_dot_nt(q2, k_ref[s, :]) for s in ks]
        sps, excls = [], []
        for j, z in enumerate(zs):
            neg_abs = pltpu.bitcast(pltpu.bitcast(z, jnp.uint32) | jnp.uint32(0x80000000), F32)
            sp = jnp.maximum(z, 0.0) + jnp.log(1.0 + jnp.exp(neg_abs))
            if diagonal and j == 0:
                sp = jnp.where(before, sp, 0.0)
            hi = pltpu.bitcast(pltpu.bitcast(sp, jnp.uint32) & jnp.uint32(0xFFFF0000), F32)
            sps.append(sp)
            excls.append(_dot(jnp.concatenate([hi.astype(BF16), (sp - hi).astype(BF16)], axis=1), later2))
        tot0 = excls[0][:, 0:1] + sps[0][:, 0:1]
        tot1 = jnp.where(valid, excls[1][:, 0:1] + sps[1][:, 0:1], 0.0)
        w0 = jnp.exp(zs[0] - sps[0] - excls[0] + c)
        if diagonal:
            w0 = jnp.where(before, w0, 0.0)
        w1 = jnp.where(valid, jnp.exp(zs[1] - sps[1] - excls[1] + (c - tot0)), 0.0)
        pv = _dot(jnp.concatenate([w0.astype(BF16), w1.astype(BF16)], axis=1),
                  jnp.concatenate([v_ref[ks[0], :], v_ref[ks[1], :]], axis=0))
        return c - tot0 - tot1, acc + pv

    c, acc = pair(qi, jnp.zeros((2 * tq, 1), F32), jnp.zeros((2 * tq, LANES), F32), True)

    def cond(st):
        kb, c, _ = st
        return jnp.logical_and(kb >= 0, jnp.max(c) > SB_ZERO_LOG)

    def body(st):
        kb, c, acc = st
        c, acc = pair(kb, c, acc, False)
        return kb - 2, c, acc

    _, _, acc = lax.while_loop(cond, body, (qi - 2, c, acc))
    o_ref[...] = jnp.where(head0, acc[:tq], acc[tq:]).astype(BF16)


def _sb_attn(q, k, v, tq=256):
    s, d = q.shape
    return pl.pallas_call(
        functools.partial(_sb_attn_kernel, tq=tq),
        grid=(d // LANES, s // tq),
        in_specs=[pl.BlockSpec((tq, LANES), lambda p, i: (i, p)),
                  pl.BlockSpec((s, LANES), lambda p, i: (0, p)),
                  pl.BlockSpec((s, LANES), lambda p, i: (0, p))],
        out_specs=pl.BlockSpec((tq, LANES), lambda p, i: (i, p)),
        out_shape=jax.ShapeDtypeStruct((s, d), BF16),
        compiler_params=_params(("arbitrary", "arbitrary")),
        name="sb_attn",
    )(q, k, v)


def _stick_breaking(x, n1, w_qkv, w_out):
    q, k, v = _sb_qkv(x, n1, w_qkv)
    return _out_proj(x, _sb_attn(q, k, v), w_out)


def _rwkv7(x, n1, mu, w_rkv, w0, w1, w2, a0, a1, a2, g1, g2, k_k, k_a, r_k, gn_g, gn_b, w_out):
    r, lw, k, v, kkraw, a, g = _rwkv_proj(x, n1, mu, w_rkv, w0, w1, w2, a0, a1, a2, g1, g2, k_k, k_a)
    return _out_proj(x, _rwkv_scan(r, lw, k, v, kkraw, a, g, r_k, gn_g, gn_b), w_out)


def kernel(x, l0_norm1, l0_gla_w_in, l0_gla_w_alpha_up, l0_gla_b_alpha, l0_gla_out_norm, l0_gla_w_out, l0_norm2, l0_ffn_w_up, l0_ffn_conv_w, l0_ffn_conv_b, l0_ffn_w_down, l1_norm1, l1_rwkv_mu, l1_rwkv_w_rkv, l1_rwkv_w0, l1_rwkv_w1, l1_rwkv_w2, l1_rwkv_a0, l1_rwkv_a1, l1_rwkv_a2, l1_rwkv_g1, l1_rwkv_g2, l1_rwkv_k_k, l1_rwkv_k_a, l1_rwkv_r_k, l1_rwkv_gn_g, l1_rwkv_gn_b, l1_rwkv_w_out, l1_norm2, l1_ffn_w_up, l1_ffn_conv_w, l1_ffn_conv_b, l1_ffn_w_down, l2_norm1, l2_sb_w_qkv, l2_sb_w_out, l2_norm2, l2_ffn_w_up, l2_ffn_conv_w, l2_ffn_conv_b, l2_ffn_w_down, l3_norm1, l3_ml_w_in, l3_ml_b_if, l3_ml_out_norm, l3_ml_w_out, l3_norm2, l3_ffn_w_up, l3_ffn_conv_w, l3_ffn_conv_b, l3_ffn_w_down, final_norm):
    b, s, d = x.shape
    outs = []
    for bi in range(b):
        h = x[bi]
        h = _gla(h, l0_norm1, l0_gla_w_in, l0_gla_w_alpha_up, l0_gla_b_alpha, l0_gla_out_norm, l0_gla_w_out)
        h = _ffn(h, l0_norm2, l0_ffn_w_up, l0_ffn_conv_w, l0_ffn_conv_b, l0_ffn_w_down)
        h = _rwkv7(h, l1_norm1, l1_rwkv_mu, l1_rwkv_w_rkv, l1_rwkv_w0, l1_rwkv_w1, l1_rwkv_w2,
                   l1_rwkv_a0, l1_rwkv_a1, l1_rwkv_a2, l1_rwkv_g1, l1_rwkv_g2,
                   l1_rwkv_k_k, l1_rwkv_k_a, l1_rwkv_r_k, l1_rwkv_gn_g, l1_rwkv_gn_b, l1_rwkv_w_out)
        h = _ffn(h, l1_norm2, l1_ffn_w_up, l1_ffn_conv_w, l1_ffn_conv_b, l1_ffn_w_down)
        h = _stick_breaking(h, l2_norm1, l2_sb_w_qkv, l2_sb_w_out)
        h = _ffn(h, l2_norm2, l2_ffn_w_up, l2_ffn_conv_w, l2_ffn_conv_b, l2_ffn_w_down)
        h = _mlstm(h, l3_norm1, l3_ml_w_in, l3_ml_b_if, l3_ml_out_norm, l3_ml_w_out)
        h = _ffn(h, l3_norm2, l3_ffn_w_up, l3_ffn_conv_w, l3_ffn_conv_b, l3_ffn_w_down, final_g=final_norm)
        outs.append(h)
    return jnp.stack(outs, axis=0)
```

```python
import functools

import jax
import jax.numpy as jnp
from jax import lax
from jax.experimental import pallas as pl
from jax.experimental.pallas import tpu as pltpu

F32 = jnp.float32
BF16 = jnp.bfloat16

NORM_EPS = 1e-6
CHUNK = 64
GLA_HEADS, GLA_DK, GLA_DV, GLA_TAU = 4, 128, 256, 16.0
ML_HEADS, ML_DK, ML_DV = 4, 128, 256
ML_CHUNK = 256
RWKV_HEAD = 64
RWKV_GN_EPS = 64e-5
SB_HEADS, SB_HEAD_DIM = 16, 64
CONV_WIDTH = 3
LANES = 128
SB_ZERO_LOG = -104.0


def _rms(x, g, eps=NORM_EPS):
    return x * lax.rsqrt(jnp.mean(x * x, axis=-1, keepdims=True) + eps) * g


def _softplus(x):
    return jnp.maximum(x, 0.0) + jnp.log1p(jnp.exp(-jnp.abs(x)))


def _log_sigmoid(x):
    return -_softplus(-x)


def _sigmoid(x):
    return 1.0 / (1.0 + jnp.exp(-x))


def _dot(a, b, **kw):
    return jnp.dot(a, b, preferred_element_type=F32, **kw)


def _dot_nt(a, b, **kw):
    return lax.dot_general(a, b, (((1,), (1,)), ((), ())), preferred_element_type=F32, **kw)


def _dot_tn(a, b, **kw):
    return lax.dot_general(a, b, (((0,), (0,)), ((), ())), preferred_element_type=F32, **kw)


def _bdot(a, b):
    return _dot(a.astype(BF16), b.astype(BF16))


def _bdot_nt(a, b):
    return _dot_nt(a.astype(BF16), b.astype(BF16))


def _bdot_tn(a, b):
    return _dot_tn(a.astype(BF16), b.astype(BF16))


def _split3(x, axis):
    hi = x.astype(BF16)
    r1 = x - hi.astype(F32)
    mid = r1.astype(BF16)
    lo = (r1 - mid.astype(F32)).astype(BF16)
    return jnp.concatenate([hi, mid, lo], axis=axis)


def _cumsum_rows(tri3, x):
    return _dot(tri3, _split3(x, 0))


def _lockstep(gens):
    out = [None] * len(gens)
    live = list(range(len(gens)))
    while live:
        for i in list(live):
            try:
                next(gens[i])
            except StopIteration as stop:
                out[i] = stop.value
                live.remove(i)
    return out


def _tri_masks(n):
    row = lax.broadcasted_iota(jnp.int32, (n, n), 0)
    col = lax.broadcasted_iota(jnp.int32, (n, n), 1)
    return row, col


def _full(shape):
    return pl.BlockSpec(shape, lambda *_: (0,) * len(shape), pipeline_mode=pl.Buffered(1))


def _params(sem):
    return pltpu.CompilerParams(dimension_semantics=sem, vmem_limit_bytes=56 * 1024 * 1024)


def _row(v, width=None):
    v = v.reshape(1, -1).astype(F32)
    if width is not None and v.shape[1] < width:
        v = jnp.pad(v, ((0, 0), (0, width - v.shape[1])))
    return v


def _ffn_kernel(x_ref, g_ref, wup_ref, cw_ref, wd_ref, fg_ref, o_ref, h_ref, act_ref, car_ref, *, final_norm, fc):
    tm = x_ref.shape[0]
    f = wd_ref.shape[0]

    @pl.when(pl.program_id(0) == 0)
    def _():
        car_ref[...] = jnp.zeros_like(car_ref)

    x = x_ref[...]
    h_ref[...] = _rms(x, g_ref[...]).astype(BF16)
    rowi = lax.broadcasted_iota(jnp.int32, (tm, fc), 0)

    def conv_up(cols):
        u = _dot(h_ref[...], wup_ref[:, cols])
        c2, c1 = car_ref[6:7, cols], car_ref[7:8, cols]
        s1 = jnp.where(rowi == 0, c1, pltpu.roll(u, 1, 0))
        s2 = jnp.where(rowi == 0, c2, jnp.where(rowi == 1, c1, pltpu.roll(u, 2, 0)))
        car_ref[:, cols] = u[tm - 8:, :]
        return s2 * cw_ref[0:1, cols] + s1 * cw_ref[1:2, cols] + u * cw_ref[2:3, cols] + cw_ref[3:4, cols]

    for c in range(f // fc):
        gate = conv_up(slice(c * fc, (c + 1) * fc))
        up = conv_up(slice(f + c * fc, f + (c + 1) * fc))
        act_ref[:, c * fc:(c + 1) * fc] = (gate * _sigmoid(gate) * up).astype(BF16)
    y = x + _dot(act_ref[...], wd_ref[...])
    if final_norm:
        y = _rms(y, fg_ref[...])
    o_ref[...] = y


def _ffn(x, norm_g, w_up, conv_w, conv_b, w_down, final_g=None, tm=1024, fc=256):
    s, d = x.shape
    f = w_down.shape[0]
    taps = jnp.concatenate([conv_w, conv_b[None, :], jnp.zeros((4, 2 * f), F32)], axis=0)
    fg = _row(final_g if final_g is not None else jnp.ones((d,), F32))
    return pl.pallas_call(
        functools.partial(_ffn_kernel, final_norm=final_g is not None, fc=fc),
        grid=(s // tm,),
        in_specs=[pl.BlockSpec((tm, d), lambda i: (i, 0)), _full((1, d)), _full((d, 2 * f)), _full((8, 2 * f)),
                  _full((f, d)), _full((1, d))],
        out_specs=pl.BlockSpec((tm, d), lambda i: (i, 0)),
        out_shape=jax.ShapeDtypeStruct((s, d), F32),
        scratch_shapes=[pltpu.VMEM((tm, d), BF16), pltpu.VMEM((tm, f), BF16), pltpu.VMEM((8, 2 * f), F32)],
        compiler_params=_params(("arbitrary",)),
        name="conv_ffn",
    )(x, _row(norm_g), w_up.astype(BF16), taps, w_down.astype(BF16), fg)


def _gla_kernel(x_ref, n1_ref, wm_ref, wa_ref, wup_ref, ba_ref, on_ref, wo_ref, o_ref,
                proj_ref, la_ref, ob_ref, st_ref):
    tb = x_ref.shape[0]
    hk, hv = GLA_HEADS * GLA_DK, GLA_HEADS * GLA_DV

    @pl.when(pl.program_id(0) == 0)
    def _():
        st_ref[...] = jnp.zeros_like(st_ref)

    x = x_ref[...]
    h = _rms(x, n1_ref[...]).astype(BF16)
    proj_ref[...] = _dot(h, wm_ref[...])
    a_low = _dot(h, wa_ref[...])
    la_ref[...] = _log_sigmoid(_bdot(a_low, wup_ref[...]) + ba_ref[...]) * (1.0 / GLA_TAU)
    grp = 4 * CHUNK
    row, col = _tri_masks(grp)
    causal = ((row // CHUNK) == (col // CHUNK)) & (row >= col)
    tri = causal.astype(BF16)
    tri3 = jnp.concatenate([tri, tri, tri], axis=1)
    rowc = lax.broadcasted_iota(jnp.int32, (grp, hk), 0) // CHUNK
    head_of_lane = lax.broadcasted_iota(jnp.int32, (1, hk), 1) // GLA_DK
    scale = GLA_DK ** -0.5

    def stack(t):
        return jnp.concatenate([jnp.where(head_of_lane == hh, t, 0.0) for hh in range(GLA_HEADS)], axis=0)

    for g in range(tb // grp):
        r0 = g * grp
        rows = slice(r0, r0 + grp)
        b = _cumsum_rows(tri3, la_ref[rows, :])
        b_last = b[CHUNK - 1:CHUNK, :]
        for c in range(1, 4):
            b_last = jnp.where(rowc == c, b[(c + 1) * CHUNK - 1:(c + 1) * CHUNK, :], b_last)
        q_dec = proj_ref[rows, 0:hk] * scale * jnp.exp(b)
        k = proj_ref[rows, hk:2 * hk]
        k_inv = k * jnp.exp(-b)
        k_end = k * jnp.exp(b_last - b)
        scores = [jnp.where(causal, _bdot_nt(q_dec[:, hh * GLA_DK:(hh + 1) * GLA_DK],
                                             k_inv[:, hh * GLA_DK:(hh + 1) * GLA_DK]), 0.0)
                  for hh in range(GLA_HEADS)]
        upd = []
        for c in range(4):
            cr = slice(r0 + c * CHUNK, r0 + (c + 1) * CHUNK)
            v_s = jnp.concatenate([proj_ref[cr, 2 * hk + hh * GLA_DV:2 * hk + (hh + 1) * GLA_DV]
                                   for hh in range(GLA_HEADS)], axis=0)
            upd.append(_bdot_tn(v_s, stack(k_end[c * CHUNK:(c + 1) * CHUNK, :])))
        intra = [_bdot(scores[hh], proj_ref[rows, 2 * hk + hh * GLA_DV:2 * hk + (hh + 1) * GLA_DV])
                 for hh in range(GLA_HEADS)]
        st = st_ref[...]
        inter = []
        for c in range(4):
            inter.append(_bdot_nt(stack(q_dec[c * CHUNK:(c + 1) * CHUNK, :]), st))
            st = st * jnp.exp(b[(c + 1) * CHUNK - 1:(c + 1) * CHUNK, :]) + upd[c]
        st_ref[...] = st
        for hh in range(GLA_HEADS):
            ob_ref[rows, hh * GLA_DV:(hh + 1) * GLA_DV] = intra[hh] + jnp.concatenate(
                [inter[c][hh * CHUNK:(hh + 1) * CHUNK, :] for c in range(4)], axis=0)
    normed = [_rms(ob_ref[:, hh * GLA_DV:(hh + 1) * GLA_DV], on_ref[:, hh * GLA_DV:(hh + 1) * GLA_DV])
              for hh in range(GLA_HEADS)]
    r = proj_ref[:, 2 * hk + hv:2 * hk + 2 * hv]
    y = jnp.concatenate(normed, axis=-1) * (r * _sigmoid(r))
    o_ref[...] = x + _bdot(y, wo_ref[...])


def _gla(x, n1, w_in, w_alpha_up, b_alpha, out_norm, w_out, tb=512):
    s, d = x.shape
    hk, hv = GLA_HEADS * GLA_DK, GLA_HEADS * GLA_DV
    nm = 2 * hk + 2 * hv
    rank = w_alpha_up.shape[0]
    wm = w_in[:, :nm].astype(BF16)
    wa = jnp.pad(w_in[:, nm:], ((0, 0), (0, LANES - rank))).astype(BF16)
    wup = jnp.pad(w_alpha_up, ((0, LANES - rank), (0, 0))).astype(BF16)
    return pl.pallas_call(
        _gla_kernel,
        grid=(s // tb,),
        in_specs=[pl.BlockSpec((tb, d), lambda i: (i, 0)), _full((1, d)), _full((d, nm)),
                  _full((d, LANES)), _full((LANES, hk)), _full((1, hk)), _full((1, hv)), _full((hv, d))],
        out_specs=pl.BlockSpec((tb, d), lambda i: (i, 0)),
        out_shape=jax.ShapeDtypeStruct((s, d), F32),
        scratch_shapes=[pltpu.VMEM((tb, nm), F32), pltpu.VMEM((tb, hk), F32), pltpu.VMEM((tb, hv), F32),
                        pltpu.VMEM((GLA_DV, hk), F32)],
        compiler_params=_params(("arbitrary",)),
        name="gla_mixer",
    )(x, _row(n1), wm, wa, wup, _row(b_alpha), _row(out_norm), w_out.astype(BF16))


def _mlstm_kernel(x_ref, n1_ref, wm_ref, wif_ref, wift_ref, bif_ref, bift_ref, on_ref, wo_ref, o_ref,
                  h_ref, proj_ref, gc_ref, ob_ref, c_ref, n_ref, m_ref):
    tb = x_ref.shape[0]
    nh = ML_HEADS
    hk, hv = nh * ML_DK, nh * ML_DV

    @pl.when(pl.program_id(0) == 0)
    def _():
        c_ref[...] = jnp.zeros_like(c_ref)
        n_ref[...] = jnp.zeros_like(n_ref)
        m_ref[...] = jnp.zeros_like(m_ref)

    x = x_ref[...]
    h = _rms(x, n1_ref[...]).astype(BF16)
    h_ref[...] = h
    proj_ref[...] = _dot(h, wm_ref[...])
    gc_ref[...] = _dot(h, wif_ref[...]) + bif_ref[...]
    row, col = _tri_masks(ML_CHUNK)
    causal = row >= col
    tri = causal.astype(BF16)
    tri3 = jnp.concatenate([tri, tri, tri], axis=1)
    scale = ML_DK ** -0.5

    for c in range(tb // ML_CHUNK):
        rows = slice(c * ML_CHUNK, (c + 1) * ML_CHUNK)
        g_col = gc_ref[rows, :]
        g_row = _dot_nt(wift_ref[...], h_ref[rows, :]) + bift_ref[...]
        b_col = _cumsum_rows(tri3, _log_sigmoid(g_col))
        b_row = _dot_nt(_split3(_log_sigmoid(g_row), 1), tri3)
        def head(hh):
            bc, ic = b_col[:, nh + hh:nh + hh + 1], g_col[:, hh:hh + 1]
            br, ir = b_row[nh + hh:nh + hh + 1, :], g_row[hh:hh + 1, :]
            q = proj_ref[rows, hh * ML_DK:(hh + 1) * ML_DK]
            k = proj_ref[rows, hk + hh * ML_DK:hk + (hh + 1) * ML_DK] * scale
            v = proj_ref[rows, 2 * hk + hh * ML_DV:2 * hk + (hh + 1) * ML_DV]
            m_prev = m_ref[0:1, hh:hh + 1]
            d_log = jnp.where(causal, bc - br + ir, -jnp.inf)
            inter = bc + m_prev
            m_t = jnp.maximum(inter, jnp.max(d_log, axis=-1, keepdims=True))
            w_intra = jnp.exp(d_log - m_t)
            sc = jnp.exp(inter - m_t)
            c_st, n_st = c_ref[hh], n_ref[hh]
            qk = _bdot_nt(q, k)
            qc = _bdot(q, c_st)
            yield
            qk = qk * w_intra
            b_last = bc[ML_CHUNK - 1:ML_CHUNK, :]
            g_end = b_last - bc + ic
            m_new = jnp.maximum(b_last + m_prev, jnp.max(g_end, axis=0, keepdims=True))
            kw = k * jnp.exp(g_end - m_new)
            carry_scale = jnp.exp(b_last + m_prev - m_new)
            num = _bdot(qk, v) + sc * qc
            c_ref[hh] = carry_scale * c_st + _bdot_tn(kw, v)
            yield
            den = jnp.sum(qk, axis=-1, keepdims=True) + sc * jnp.sum(q * n_st, axis=-1, keepdims=True)
            ob_ref[rows, hh * ML_DV:(hh + 1) * ML_DV] = num / jnp.maximum(jnp.abs(den), jnp.exp(-m_t))
            n_ref[hh] = carry_scale * n_st + jnp.sum(kw, axis=0, keepdims=True)
            m_ref[0:1, hh:hh + 1] = m_new

        _lockstep([head(hh) for hh in range(nh)])
    normed = [_rms(ob_ref[:, hh * ML_DV:(hh + 1) * ML_DV], on_ref[:, hh * ML_DV:(hh + 1) * ML_DV])
              for hh in range(nh)]
    o_pre = proj_ref[:, 2 * hk + hv:2 * hk + 2 * hv]
    y = jnp.concatenate(normed, axis=-1) * _sigmoid(o_pre)
    o_ref[...] = x + _bdot(y, wo_ref[...])


def _mlstm(x, n1, w_in, b_if, out_norm, w_out, tb=512):
    s, d = x.shape
    nh = ML_HEADS
    hk, hv = nh * ML_DK, nh * ML_DV
    nm = 2 * hk + 2 * hv
    wm = w_in[:, :nm].astype(BF16)
    w_if = w_in[:, nm:]
    wif = jnp.pad(w_if, ((0, 0), (0, LANES - 2 * nh))).astype(BF16)
    wift = w_if.T.astype(BF16)
    return pl.pallas_call(
        _mlstm_kernel,
        grid=(s // tb,),
        in_specs=[pl.BlockSpec((tb, d), lambda i: (i, 0)), _full((1, d)), _full((d, nm)),
                  _full((d, LANES)), _full((2 * nh, d)), _full((1, LANES)), _full((2 * nh, 1)),
                  _full((1, hv)), _full((hv, d))],
        out_specs=pl.BlockSpec((tb, d), lambda i: (i, 0)),
        out_shape=jax.ShapeDtypeStruct((s, d), F32),
        scratch_shapes=[pltpu.VMEM((tb, d), BF16), pltpu.VMEM((tb, nm), F32), pltpu.VMEM((tb, LANES), F32),
                        pltpu.VMEM((tb, hv), F32), pltpu.VMEM((nh, ML_DK, ML_DV), F32),
                        pltpu.VMEM((nh, 1, ML_DK), F32), pltpu.VMEM((8, LANES), F32)],
        compiler_params=_params(("arbitrary",)),
        name="mlstm_mixer",
    )(x, _row(n1), wm, wif, wift, _row(b_if, LANES), b_if.reshape(2 * nh, 1).astype(F32),
      _row(out_norm), w_out.astype(BF16))


def _rwkv_proj_kernel(x_ref, n1_ref, mu_ref, wr_ref, wk_ref, wv_ref, w0_ref, w1_ref, w2_ref,
                      a0_ref, a1_ref, a2_ref, g1_ref, g2_ref, kk_ref, ka_ref,
                      r_ref, lw_ref, k_ref, v_ref, kkraw_ref, a_ref, g_ref, car_ref):
    tm = x_ref.shape[0]

    @pl.when(pl.program_id(0) == 0)
    def _():
        car_ref[...] = jnp.zeros_like(car_ref)

    h = _rms(x_ref[...], n1_ref[...])
    rowi = lax.broadcasted_iota(jnp.int32, h.shape, 0)
    h_prev = jnp.where(rowi == 0, car_ref[7:8, :], pltpu.roll(h, 1, 0))
    car_ref[...] = h[tm - 8:, :]
    xx = h_prev - h
    xr, xw, xk, xv, xa, xg = (h + xx * mu_ref[j:j + 1, :] for j in range(6))
    r = _bdot(xr, wr_ref[...])
    k = _bdot(xk, wk_ref[...])
    v = _bdot(xv, wv_ref[...])
    w_log = -_softplus(-(w0_ref[...] + _bdot(jnp.tanh(_bdot(xw, w1_ref[...])), w2_ref[...]))) - 0.5
    a = _sigmoid(a0_ref[...] + _bdot(_bdot(xa, a1_ref[...]), a2_ref[...]))
    g = _bdot(_sigmoid(_bdot(xg, g1_ref[...])), g2_ref[...])
    r_ref[...] = r
    lw_ref[...] = -jnp.exp(w_log)
    k_ref[...] = k * (1.0 + (a - 1.0) * ka_ref[...])
    v_ref[...] = v
    kkraw_ref[...] = k * kk_ref[...]
    a_ref[...] = a
    g_ref[...] = g


def _rwkv_proj(x, n1, mu, w_rkv, w0, w1, w2, a0, a1, a2, g1, g2, k_k, k_a, tm=512):
    s, d = x.shape

    def padc(w):
        return jnp.pad(w, ((0, 0), (0, LANES - w.shape[1]))).astype(BF16) if w.shape[1] < LANES else w.astype(BF16)

    def padr(w):
        return jnp.pad(w, ((0, LANES - w.shape[0]), (0, 0))).astype(BF16) if w.shape[0] < LANES else w.astype(BF16)

    mu8 = jnp.pad(mu, ((0, 8 - mu.shape[0]), (0, 0)))
    blk = pl.BlockSpec((tm, d), lambda i: (i, 0))
    out = jax.ShapeDtypeStruct((s, d), F32)
    return pl.pallas_call(
        _rwkv_proj_kernel,
        grid=(s // tm,),
        in_specs=[blk, _full((1, d)), _full((8, d)), _full((d, d)), _full((d, d)), _full((d, d)),
                  _full((1, d)), _full((d, LANES)), _full((LANES, d)),
                  _full((1, d)), _full((d, LANES)), _full((LANES, d)),
                  _full((d, LANES)), _full((LANES, d)), _full((1, d)), _full((1, d))],
        out_specs=[blk] * 7,
        out_shape=[out] * 7,
        scratch_shapes=[pltpu.VMEM((8, d), F32)],
        compiler_params=_params(("arbitrary",)),
        name="rwkv_proj",
    )(x, _row(n1), mu8, w_rkv[0].astype(BF16), w_rkv[1].astype(BF16), w_rkv[2].astype(BF16),
      _row(w0), padc(w1), padr(w2), _row(a0), padc(a1), padr(a2), padc(g1), padr(g2), _row(k_k), _row(k_a))


def _unit_lower_inverse(a_strict, row, col):
    n = a_strict.shape[0]
    eye = (row == col).astype(F32)
    blk16 = (row // 16) == (col // 16)
    p = jnp.where(blk16, a_strict, 0.0)
    inv = eye + p
    p = _bdot(p, p)
    yield
    for _ in range(2):
        both = _bdot(jnp.concatenate([p, inv], axis=0), p)
        yield
        p, inv = both[:n], inv + both[n:]
    inv16 = inv + _bdot(inv, p)
    yield
    blk32 = (row // 32) == (col // 32)
    off32 = jnp.where(blk32 & ~blk16, a_strict, 0.0)
    off64 = jnp.where(blk32, 0.0, a_strict)
    x = _bdot(inv16, jnp.concatenate([off32, off64], axis=1))
    yield
    x32, x64 = x[:, :n], x[:, n:]
    y = _bdot(x32, jnp.concatenate([inv16, x64], axis=1))
    yield
    inv32 = inv16 + y[:, :n]
    return inv32 + _bdot(x64 + y[:, n:], inv32)


def _rwkv_scan_kernel(r_ref, lw_ref, k_ref, v_ref, kkraw_ref, a_ref, g_ref, rk_ref, gg_ref, gb_ref, o_ref, s_ref):
    tb = r_ref.shape[0]
    n = RWKV_HEAD
    grp = 2 * CHUNK
    big = 2 * grp

    @pl.when(pl.program_id(1) == 0)
    def _():
        s_ref[...] = jnp.zeros_like(s_ref)

    rown = lax.broadcasted_iota(jnp.int32, (grp, LANES), 0)
    head0 = lax.broadcasted_iota(jnp.int32, (1, LANES), 1) < n
    chunk0 = rown < CHUNK
    row, col = _tri_masks(big)
    same = (row // CHUNK) == (col // CHUNK)
    incl = same & (row >= col)
    strict = same & (row > col)
    tri = incl[:grp, :grp].astype(BF16)
    tri3 = jnp.concatenate([tri, tri, tri], axis=1)
    srow = lax.broadcasted_iota(jnp.int32, (big, LANES), 0)
    slane = lax.broadcasted_iota(jnp.int32, (big, LANES), 1)
    srow_chunk0 = (srow // CHUNK) % 2 == 0
    diag = (srow % grp) == slane
    own = ((rown < n) == head0)

    def stack(t):
        return jnp.concatenate([jnp.where(head0, t, 0.0), jnp.where(head0, 0.0, t)], axis=0)

    def blockdiag(t):
        s = stack(t)
        return jnp.concatenate([jnp.where(srow_chunk0, s, 0.0), jnp.where(srow_chunk0, 0.0, s)], axis=1)

    def spread(t):
        return jnp.concatenate([jnp.where(srow < grp, t, 0.0), jnp.where(srow < grp, 0.0, t)], axis=1)

    def head_sum(t):
        return jnp.where(head0, jnp.sum(jnp.where(head0, t, 0.0), axis=-1, keepdims=True),
                         jnp.sum(jnp.where(head0, 0.0, t), axis=-1, keepdims=True))

    def per_chunk(t, c):
        return jnp.concatenate([t[c * CHUNK:(c + 1) * CHUNK], t[grp + c * CHUNK:grp + (c + 1) * CHUNK]], axis=0)

    def phase1(g):
        rows = slice(g * grp, (g + 1) * grp)
        r, lw, k, v = r_ref[rows, :], lw_ref[rows, :], k_ref[rows, :], v_ref[rows, :]
        kk, a = kkraw_ref[rows, :], a_ref[rows, :]
        kk = kk / jnp.maximum(jnp.sqrt(head_sum(kk * kk)), 1e-12)
        cum = _cumsum_rows(tri3, lw)
        yield
        cum_last = jnp.where(chunk0, cum[CHUNK - 1:CHUNK, :], cum[grp - 1:grp, :])
        e_neg, e_end = jnp.exp(-cum), jnp.exp(cum_last - cum)
        b_vec = kk * a
        a_t = stack(-kk * jnp.exp(cum - lw))
        r_t = stack(r * jnp.exp(cum))
        v_s = stack(v)
        a4 = _bdot_nt(jnp.concatenate([a_t, r_t], axis=0),
                      jnp.concatenate([b_vec * e_neg, k * e_neg], axis=0))
        yield
        a_ab = jnp.where(strict, spread(a4[:big, :grp]), 0.0)
        a_ak = jnp.where(strict, spread(a4[:big, grp:]), 0.0)
        a_rb = jnp.where(incl, spread(a4[big:, :grp]), 0.0)
        a_rk = jnp.where(incl, spread(a4[big:, grp:]), 0.0)
        akv = _bdot(a_ak, v_s)
        t_inv = yield from _unit_lower_inverse(a_ab, row, col)
        yield
        z = _bdot(t_inv, jnp.concatenate([a_t, akv], axis=1))
        yield
        zv = jnp.concatenate([z, jnp.concatenate([jnp.zeros_like(v_s), v_s], axis=1)], axis=0)
        gh = _bdot_tn(jnp.concatenate([blockdiag(b_vec * e_end), blockdiag(k * e_end)], axis=0), zv)
        ry = _bdot(jnp.concatenate([a_rb, a_rk], axis=1), zv)
        p_end = jnp.exp(jnp.concatenate([cum[CHUNK - 1:CHUNK, :], cum[grp - 1:grp, :]], axis=0))
        g_all = gh[:, :LANES] + jnp.where(diag, jnp.where(srow < grp, p_end[0:1, :], p_end[1:2, :]), 0.0)
        h_all = gh[:, LANES:]
        rp_all = r_t + ry[:, :LANES]
        y0_all = ry[:, LANES:]
        return g_all, h_all, rp_all, y0_all

    def emit(g, c, y):
        crow = slice(g * grp + c * CHUNK, g * grp + (c + 1) * CHUNK)
        mean = jnp.sum(y, axis=-1, keepdims=True) * (1.0 / n)
        dev = jnp.where(own, y - mean, 0.0)
        var = jnp.sum(dev * dev, axis=-1, keepdims=True) * (1.0 / n)
        yn = dev * lax.rsqrt(var + RWKV_GN_EPS)
        yn = yn[:CHUNK] + yn[CHUNK:]
        rc, kc, vc = r_ref[crow, :], k_ref[crow, :], v_ref[crow, :]
        bonus = head_sum(rc * kc * rk_ref[...])[:CHUNK] * vc
        o_ref[crow, :] = ((yn * gg_ref[...] + gb_ref[...] + bonus) * g_ref[crow, :]).astype(BF16)

    def phase2(g0, staged):
        for g, (g_all, h_all, rp_all, y0_all) in enumerate(staged, g0):
            for c in range(2):
                lhs = jnp.concatenate([g_all[c * grp:(c + 1) * grp], per_chunk(rp_all, c)], axis=0)
                both = _bdot(lhs, s_ref[...])
                yield
                s_ref[...] = both[:grp] + h_all[c * grp:(c + 1) * grp]
                emit(g, c, both[grp:] + per_chunk(y0_all, c))

    n_groups, wave = tb // grp, 4
    pending = None
    for g0 in range(0, n_groups, wave):
        gens = [phase1(g) for g in range(g0, min(g0 + wave, n_groups))]
        done = _lockstep(gens + ([pending] if pending is not None else []))
        pending = phase2(g0, done[:len(gens)])
    _lockstep([pending])


def _rwkv_scan(r, lw, k, v, kkraw, a, g, r_k, gn_g, gn_b, tb=2048):
    s, d = r.shape
    tb = min(tb, s)
    blk = pl.BlockSpec((tb, LANES), lambda p, i: (i, p))
    vec = pl.BlockSpec((1, LANES), lambda p, i: (0, p))
    return pl.pallas_call(
        _rwkv_scan_kernel,
        grid=(d // LANES, s // tb),
        in_specs=[blk] * 7 + [vec] * 3,
        out_specs=blk,
        out_shape=jax.ShapeDtypeStruct((s, d), BF16),
        scratch_shapes=[pltpu.VMEM((LANES, LANES), F32)],
        compiler_params=_params(("arbitrary", "arbitrary")),
        name="rwkv_scan",
    )(r, lw, k, v, kkraw, a, g, _row(r_k), _row(gn_g), _row(gn_b))


def _out_proj_kernel(x_ref, a_ref, w_ref, o_ref):
    o_ref[...] = x_ref[...] + _dot(a_ref[...], w_ref[...])


def _out_proj(x, a, w, tm=1024):
    s, d = x.shape
    blk = pl.BlockSpec((tm, d), lambda i: (i, 0))
    return pl.pallas_call(
        _out_proj_kernel,
        grid=(s // tm,),
        in_specs=[blk, blk, _full((d, d))],
        out_specs=blk,
        out_shape=jax.ShapeDtypeStruct((s, d), F32),
        compiler_params=_params(("arbitrary",)),
        name="out_proj",
    )(x, a, w.astype(BF16))


def _sb_qkv_kernel(x_ref, n1_ref, w_ref, q_ref, k_ref, v_ref):
    d = x_ref.shape[1]
    h = _rms(x_ref[...], n1_ref[...]).astype(BF16)
    qkv = _dot(h, w_ref[...])
    q_ref[...] = (qkv[:, :d] * SB_HEAD_DIM ** -0.5).astype(BF16)
    k_ref[...] = qkv[:, d:2 * d].astype(BF16)
    v_ref[...] = qkv[:, 2 * d:].astype(BF16)


def _sb_qkv(x, n1, w_qkv, tm=1024):
    s, d = x.shape
    blk = pl.BlockSpec((tm, d), lambda i: (i, 0))
    out = jax.ShapeDtypeStruct((s, d), BF16)
    return pl.pallas_call(
        _sb_qkv_kernel,
        grid=(s // tm,),
        in_specs=[blk, _full((1, d)), _full((d, 3 * d))],
        out_specs=[blk] * 3,
        out_shape=[out] * 3,
        compiler_params=_params(("arbitrary",)),
        name="sb_qkv",
    )(x, _row(n1), w_qkv.astype(BF16))


def _sb_attn_kernel(q_ref, k_ref, v_ref, o_ref, *, tq):
    qi = pl.program_id(1)
    head0 = lax.broadcasted_iota(jnp.int32, (1, LANES), 1) < SB_HEAD_DIM
    q = q_ref[...]
    zero = jnp.zeros_like(q)
    q2 = jnp.concatenate([jnp.where(head0, q, zero), jnp.where(head0, zero, q)], axis=0)
    row, col = _tri_masks(tq)
    before = jnp.concatenate([col < row, col < row], axis=0)
    later = (row > col).astype(BF16)
    later2 = jnp.concatenate([later, later], axis=0)

    def pair(kb, c, acc, diagonal):
        valid = kb >= 1
        ks = (pl.ds(pl.multiple_of(kb * tq, tq), tq), pl.ds(pl.multiple_of(jnp.maximum(kb - 1, 0) * tq, tq), tq))
        zs = [_dot_nt(q2, k_ref[s, :]) for s in ks]
        sps, excls = [], []
        for j, z in enumerate(zs):
            sp = jnp.maximum(z, 0.0) + jnp.log(1.0 + jnp.exp(-jnp.abs(z)))
            if diagonal and j == 0:
                sp = jnp.where(before, sp, 0.0)
            hi = sp.astype(BF16)
            sps.append(sp)
            excls.append(_dot(jnp.concatenate([hi, (sp - hi.astype(F32)).astype(BF16)], axis=1), later2))
        tot0 = excls[0][:, 0:1] + sps[0][:, 0:1]
        tot1 = jnp.where(valid, excls[1][:, 0:1] + sps[1][:, 0:1], 0.0)
        w0 = jnp.exp(zs[0] - sps[0] - excls[0] + c)
        if diagonal:
            w0 = jnp.where(before, w0, 0.0)
        w1 = jnp.where(valid, jnp.exp(zs[1] - sps[1] - excls[1] + (c - tot0)), 0.0)
        pv = _dot(jnp.concatenate([w0.astype(BF16), w1.astype(BF16)], axis=1),
                  jnp.concatenate([v_ref[ks[0], :], v_ref[ks[1], :]], axis=0))
        return c - tot0 - tot1, acc + pv

    c, acc = pair(qi, jnp.zeros((2 * tq, 1), F32), jnp.zeros((2 * tq, LANES), F32), True)

    def cond(st):
        kb, c, _ = st
        return jnp.logical_and(kb >= 0, jnp.max(c) > SB_ZERO_LOG)

    def body(st):
        kb, c, acc = st
        c, acc = pair(kb, c, acc, False)
        return kb - 2, c, acc

    _, _, acc = lax.while_loop(cond, body, (qi - 2, c, acc))
    o_ref[...] = jnp.where(head0, acc[:tq], acc[tq:]).astype(BF16)


def _sb_attn(q, k, v, tq=256):
    s, d = q.shape
    return pl.pallas_call(
        functools.partial(_sb_attn_kernel, tq=tq),
        grid=(d // LANES, s // tq),
        in_specs=[pl.BlockSpec((tq, LANES), lambda p, i: (i, p)),
                  pl.BlockSpec((s, LANES), lambda p, i: (0, p)),
                  pl.BlockSpec((s, LANES), lambda p, i: (0, p))],
        out_specs=pl.BlockSpec((tq, LANES), lambda p, i: (i, p)),
        out_shape=jax.ShapeDtypeStruct((s, d), BF16),
        compiler_params=_params(("arbitrary", "arbitrary")),
        name="sb_attn",
    )(q, k, v)


def _stick_breaking(x, n1, w_qkv, w_out):
    q, k, v = _sb_qkv(x, n1, w_qkv)
    return _out_proj(x, _sb_attn(q, k, v), w_out)


def _rwkv7(x, n1, mu, w_rkv, w0, w1, w2, a0, a1, a2, g1, g2, k_k, k_a, r_k, gn_g, gn_b, w_out):
    r, lw, k, v, kkraw, a, g = _rwkv_proj(x, n1, mu, w_rkv, w0, w1, w2, a0, a1, a2, g1, g2, k_k, k_a)
    return _out_proj(x, _rwkv_scan(r, lw, k, v, kkraw, a, g, r_k, gn_g, gn_b), w_out)


def kernel(x, l0_norm1, l0_gla_w_in, l0_gla_w_alpha_up, l0_gla_b_alpha, l0_gla_out_norm, l0_gla_w_out, l0_norm2, l0_ffn_w_up, l0_ffn_conv_w, l0_ffn_conv_b, l0_ffn_w_down, l1_norm1, l1_rwkv_mu, l1_rwkv_w_rkv, l1_rwkv_w0, l1_rwkv_w1, l1_rwkv_w2, l1_rwkv_a0, l1_rwkv_a1, l1_rwkv_a2, l1_rwkv_g1, l1_rwkv_g2, l1_rwkv_k_k, l1_rwkv_k_a, l1_rwkv_r_k, l1_rwkv_gn_g, l1_rwkv_gn_b, l1_rwkv_w_out, l1_norm2, l1_ffn_w_up, l1_ffn_conv_w, l1_ffn_conv_b, l1_ffn_w_down, l2_norm1, l2_sb_w_qkv, l2_sb_w_out, l2_norm2, l2_ffn_w_up, l2_ffn_conv_w, l2_ffn_conv_b, l2_ffn_w_down, l3_norm1, l3_ml_w_in, l3_ml_b_if, l3_ml_out_norm, l3_ml_w_out, l3_norm2, l3_ffn_w_up, l3_ffn_conv_w, l3_ffn_conv_b, l3_ffn_w_down, final_norm):
    b, s, d = x.shape
    outs = []
    for bi in range(b):
        h = x[bi]
        h = _gla(h, l0_norm1, l0_gla_w_in, l0_gla_w_alpha_up, l0_gla_b_alpha, l0_gla_out_norm, l0_gla_w_out)
        h = _ffn(h, l0_norm2, l0_ffn_w_up, l0_ffn_conv_w, l0_ffn_conv_b, l0_ffn_w_down)
        h = _rwkv7(h, l1_norm1, l1_rwkv_mu, l1_rwkv_w_rkv, l1_rwkv_w0, l1_rwkv_w1, l1_rwkv_w2,
                   l1_rwkv_a0, l1_rwkv_a1, l1_rwkv_a2, l1_rwkv_g1, l1_rwkv_g2,
                   l1_rwkv_k_k, l1_rwkv_k_a, l1_rwkv_r_k, l1_rwkv_gn_g, l1_rwkv_gn_b, l1_rwkv_w_out)
        h = _ffn(h, l1_norm2, l1_ffn_w_up, l1_ffn_conv_w, l1_ffn_conv_b, l1_ffn_w_down)
        h = _stick_breaking(h, l2_norm1, l2_sb_w_qkv, l2_sb_w_out)
        h = _ffn(h, l2_norm2, l2_ffn_w_up, l2_ffn_conv_w, l2_ffn_conv_b, l2_ffn_w_down)
        h = _mlstm(h, l3_norm1, l3_ml_w_in, l3_ml_b_if, l3_ml_out_norm, l3_ml_w_out)
        h = _ffn(h, l3_norm2, l3_ffn_w_up, l3_ffn_conv_w, l3_ffn_conv_b, l3_ffn_w_down, final_g=final_norm)
        outs.append(h)
    return jnp.stack(outs, axis=0)
```

```python
import functools

import jax
import jax.numpy as jnp
from jax import lax
from jax.experimental import pallas as pl
from jax.experimental.pallas import tpu as pltpu

F32 = jnp.float32
BF16 = jnp.bfloat16

NORM_EPS = 1e-6
CHUNK = 64
GLA_HEADS, GLA_DK, GLA_DV, GLA_TAU = 4, 128, 256, 16.0
ML_HEADS, ML_DK, ML_DV = 4, 128, 256
ML_CHUNK = 256
RWKV_HEAD = 64
RWKV_GN_EPS = 64e-5
SB_HEADS, SB_HEAD_DIM = 16, 64
CONV_WIDTH = 3
LANES = 128
SB_ZERO_LOG = -104.0


def _rms(x, g, eps=NORM_EPS):
    return x * lax.rsqrt(jnp.mean(x * x, axis=-1, keepdims=True) + eps) * g


def _softplus(x):
    return jnp.maximum(x, 0.0) + jnp.log1p(jnp.exp(-jnp.abs(x)))


def _log_sigmoid(x):
    return -_softplus(-x)


def _sigmoid(x):
    return 1.0 / (1.0 + jnp.exp(-x))


def _dot(a, b, **kw):
    return jnp.dot(a, b, preferred_element_type=F32, **kw)


def _dot_nt(a, b, **kw):
    return lax.dot_general(a, b, (((1,), (1,)), ((), ())), preferred_element_type=F32, **kw)


def _dot_tn(a, b, **kw):
    return lax.dot_general(a, b, (((0,), (0,)), ((), ())), preferred_element_type=F32, **kw)


def _bdot(a, b):
    return _dot(a.astype(BF16), b.astype(BF16))


def _bdot_nt(a, b):
    return _dot_nt(a.astype(BF16), b.astype(BF16))


def _bdot_tn(a, b):
    return _dot_tn(a.astype(BF16), b.astype(BF16))


def _split3(x, axis):
    hi = x.astype(BF16)
    r1 = x - hi.astype(F32)
    mid = r1.astype(BF16)
    lo = (r1 - mid.astype(F32)).astype(BF16)
    return jnp.concatenate([hi, mid, lo], axis=axis)


def _cumsum_rows(tri3, x):
    return _dot(tri3, _split3(x, 0))


def _lockstep(gens):
    out = [None] * len(gens)
    live = list(range(len(gens)))
    while live:
        for i in list(live):
            try:
                next(gens[i])
            except StopIteration as stop:
                out[i] = stop.value
                live.remove(i)
    return out


def _tri_masks(n):
    row = lax.broadcasted_iota(jnp.int32, (n, n), 0)
    col = lax.broadcasted_iota(jnp.int32, (n, n), 1)
    return row, col


def _full(shape):
    return pl.BlockSpec(shape, lambda *_: (0,) * len(shape), pipeline_mode=pl.Buffered(1))


def _params(sem):
    return pltpu.CompilerParams(dimension_semantics=sem, vmem_limit_bytes=56 * 1024 * 1024)


def _row(v, width=None):
    v = v.reshape(1, -1).astype(F32)
    if width is not None and v.shape[1] < width:
        v = jnp.pad(v, ((0, 0), (0, width - v.shape[1])))
    return v


def _ffn_kernel(x_ref, g_ref, wup_ref, cw_ref, wd_ref, fg_ref, o_ref, h_ref, act_ref, car_ref, *, final_norm, fc):
    tm = x_ref.shape[0]
    f = wd_ref.shape[0]

    @pl.when(pl.program_id(0) == 0)
    def _():
        car_ref[...] = jnp.zeros_like(car_ref)

    x = x_ref[...]
    h_ref[...] = _rms(x, g_ref[...]).astype(BF16)
    rowi = lax.broadcasted_iota(jnp.int32, (tm, fc), 0)

    def conv_up(cols):
        u = _dot(h_ref[...], wup_ref[:, cols])
        c2, c1 = car_ref[6:7, cols], car_ref[7:8, cols]
        s1 = jnp.where(rowi == 0, c1, pltpu.roll(u, 1, 0))
        s2 = jnp.where(rowi == 0, c2, jnp.where(rowi == 1, c1, pltpu.roll(u, 2, 0)))
        car_ref[:, cols] = u[tm - 8:, :]
        return s2 * cw_ref[0:1, cols] + s1 * cw_ref[1:2, cols] + u * cw_ref[2:3, cols] + cw_ref[3:4, cols]

    for c in range(f // fc):
        gate = conv_up(slice(c * fc, (c + 1) * fc))
        up = conv_up(slice(f + c * fc, f + (c + 1) * fc))
        act_ref[:, c * fc:(c + 1) * fc] = (gate * _sigmoid(gate) * up).astype(BF16)
    y = x + _dot(act_ref[...], wd_ref[...])
    if final_norm:
        y = _rms(y, fg_ref[...])
    o_ref[...] = y


def _ffn(x, norm_g, w_up, conv_w, conv_b, w_down, final_g=None, tm=1024, fc=256):
    s, d = x.shape
    f = w_down.shape[0]
    taps = jnp.concatenate([conv_w, conv_b[None, :], jnp.zeros((4, 2 * f), F32)], axis=0)
    fg = _row(final_g if final_g is not None else jnp.ones((d,), F32))
    return pl.pallas_call(
        functools.partial(_ffn_kernel, final_norm=final_g is not None, fc=fc),
        grid=(s // tm,),
        in_specs=[pl.BlockSpec((tm, d), lambda i: (i, 0)), _full((1, d)), _full((d, 2 * f)), _full((8, 2 * f)),
                  _full((f, d)), _full((1, d))],
        out_specs=pl.BlockSpec((tm, d), lambda i: (i, 0)),
        out_shape=jax.ShapeDtypeStruct((s, d), F32),
        scratch_shapes=[pltpu.VMEM((tm, d), BF16), pltpu.VMEM((tm, f), BF16), pltpu.VMEM((8, 2 * f), F32)],
        compiler_params=_params(("arbitrary",)),
        name="conv_ffn",
    )(x, _row(norm_g), w_up.astype(BF16), taps, w_down.astype(BF16), fg)


def _gla_kernel(x_ref, n1_ref, wm_ref, wa_ref, wup_ref, ba_ref, on_ref, wo_ref, o_ref,
                proj_ref, la_ref, ob_ref, st_ref):
    tb = x_ref.shape[0]
    hk, hv = GLA_HEADS * GLA_DK, GLA_HEADS * GLA_DV

    @pl.when(pl.program_id(0) == 0)
    def _():
        st_ref[...] = jnp.zeros_like(st_ref)

    x = x_ref[...]
    h = _rms(x, n1_ref[...]).astype(BF16)
    proj_ref[...] = _dot(h, wm_ref[...])
    a_low = _dot(h, wa_ref[...])
    la_ref[...] = _log_sigmoid(_bdot(a_low, wup_ref[...]) + ba_ref[...]) * (1.0 / GLA_TAU)
    grp = 4 * CHUNK
    row, col = _tri_masks(grp)
    causal = ((row // CHUNK) == (col // CHUNK)) & (row >= col)
    tri = causal.astype(BF16)
    tri3 = jnp.concatenate([tri, tri, tri], axis=1)
    rowc = lax.broadcasted_iota(jnp.int32, (grp, hk), 0) // CHUNK
    head_of_lane = lax.broadcasted_iota(jnp.int32, (1, hk), 1) // GLA_DK
    scale = GLA_DK ** -0.5

    def stack(t):
        return jnp.concatenate([jnp.where(head_of_lane == hh, t, 0.0) for hh in range(GLA_HEADS)], axis=0)

    for g in range(tb // grp):
        r0 = g * grp
        rows = slice(r0, r0 + grp)
        b = _cumsum_rows(tri3, la_ref[rows, :])
        b_last = b[CHUNK - 1:CHUNK, :]
        for c in range(1, 4):
            b_last = jnp.where(rowc == c, b[(c + 1) * CHUNK - 1:(c + 1) * CHUNK, :], b_last)
        q_dec = proj_ref[rows, 0:hk] * scale * jnp.exp(b)
        k = proj_ref[rows, hk:2 * hk]
        k_inv = k * jnp.exp(-b)
        k_end = k * jnp.exp(b_last - b)
        scores = [jnp.where(causal, _bdot_nt(q_dec[:, hh * GLA_DK:(hh + 1) * GLA_DK],
                                             k_inv[:, hh * GLA_DK:(hh + 1) * GLA_DK]), 0.0)
                  for hh in range(GLA_HEADS)]
        upd = []
        for c in range(4):
            cr = slice(r0 + c * CHUNK, r0 + (c + 1) * CHUNK)
            v_s = jnp.concatenate([proj_ref[cr, 2 * hk + hh * GLA_DV:2 * hk + (hh + 1) * GLA_DV]
                                   for hh in range(GLA_HEADS)], axis=0)
            upd.append(_bdot_tn(v_s, stack(k_end[c * CHUNK:(c + 1) * CHUNK, :])))
        intra = [_bdot(scores[hh], proj_ref[rows, 2 * hk + hh * GLA_DV:2 * hk + (hh + 1) * GLA_DV])
                 for hh in range(GLA_HEADS)]
        st = st_ref[...]
        inter = []
        for c in range(4):
            inter.append(_bdot_nt(stack(q_dec[c * CHUNK:(c + 1) * CHUNK, :]), st))
            st = st * jnp.exp(b[(c + 1) * CHUNK - 1:(c + 1) * CHUNK, :]) + upd[c]
        st_ref[...] = st
        for hh in range(GLA_HEADS):
            ob_ref[rows, hh * GLA_DV:(hh + 1) * GLA_DV] = intra[hh] + jnp.concatenate(
                [inter[c][hh * CHUNK:(hh + 1) * CHUNK, :] for c in range(4)], axis=0)
    normed = [_rms(ob_ref[:, hh * GLA_DV:(hh + 1) * GLA_DV], on_ref[:, hh * GLA_DV:(hh + 1) * GLA_DV])
              for hh in range(GLA_HEADS)]
    r = proj_ref[:, 2 * hk + hv:2 * hk + 2 * hv]
    y = jnp.concatenate(normed, axis=-1) * (r * _sigmoid(r))
    o_ref[...] = x + _bdot(y, wo_ref[...])


def _gla(x, n1, w_in, w_alpha_up, b_alpha, out_norm, w_out, tb=1024):
    s, d = x.shape
    hk, hv = GLA_HEADS * GLA_DK, GLA_HEADS * GLA_DV
    nm = 2 * hk + 2 * hv
    rank = w_alpha_up.shape[0]
    wm = w_in[:, :nm].astype(BF16)
    wa = jnp.pad(w_in[:, nm:], ((0, 0), (0, LANES - rank))).astype(BF16)
    wup = jnp.pad(w_alpha_up, ((0, LANES - rank), (0, 0))).astype(BF16)
    return pl.pallas_call(
        _gla_kernel,
        grid=(s // tb,),
        in_specs=[pl.BlockSpec((tb, d), lambda i: (i, 0)), _full((1, d)), _full((d, nm)),
                  _full((d, LANES)), _full((LANES, hk)), _full((1, hk)), _full((1, hv)), _full((hv, d))],
        out_specs=pl.BlockSpec((tb, d), lambda i: (i, 0)),
        out_shape=jax.ShapeDtypeStruct((s, d), F32),
        scratch_shapes=[pltpu.VMEM((tb, nm), F32), pltpu.VMEM((tb, hk), F32), pltpu.VMEM((tb, hv), F32),
                        pltpu.VMEM((GLA_DV, hk), F32)],
        compiler_params=_params(("arbitrary",)),
        name="gla_mixer",
    )(x, _row(n1), wm, wa, wup, _row(b_alpha), _row(out_norm), w_out.astype(BF16))


def _mlstm_kernel(x_ref, n1_ref, wm_ref, wif_ref, wift_ref, bif_ref, bift_ref, on_ref, wo_ref, o_ref,
                  h_ref, proj_ref, gc_ref, ob_ref, c_ref, n_ref, m_ref):
    tb = x_ref.shape[0]
    nh = ML_HEADS
    hk, hv = nh * ML_DK, nh * ML_DV

    @pl.when(pl.program_id(0) == 0)
    def _():
        c_ref[...] = jnp.zeros_like(c_ref)
        n_ref[...] = jnp.zeros_like(n_ref)
        m_ref[...] = jnp.zeros_like(m_ref)

    x = x_ref[...]
    h = _rms(x, n1_ref[...]).astype(BF16)
    h_ref[...] = h
    proj_ref[...] = _dot(h, wm_ref[...])
    gc_ref[...] = _dot(h, wif_ref[...]) + bif_ref[...]
    row, col = _tri_masks(ML_CHUNK)
    causal = row >= col
    tri = causal.astype(BF16)
    tri3 = jnp.concatenate([tri, tri, tri], axis=1)
    scale = ML_DK ** -0.5

    for c in range(tb // ML_CHUNK):
        rows = slice(c * ML_CHUNK, (c + 1) * ML_CHUNK)
        g_col = gc_ref[rows, :]
        g_row = _dot_nt(wift_ref[...], h_ref[rows, :]) + bift_ref[...]
        b_col = _cumsum_rows(tri3, _log_sigmoid(g_col))
        b_row = _dot_nt(_split3(_log_sigmoid(g_row), 1), tri3)
        def head(hh):
            bc, ic = b_col[:, nh + hh:nh + hh + 1], g_col[:, hh:hh + 1]
            br, ir = b_row[nh + hh:nh + hh + 1, :], g_row[hh:hh + 1, :]
            q = proj_ref[rows, hh * ML_DK:(hh + 1) * ML_DK]
            k = proj_ref[rows, hk + hh * ML_DK:hk + (hh + 1) * ML_DK] * scale
            v = proj_ref[rows, 2 * hk + hh * ML_DV:2 * hk + (hh + 1) * ML_DV]
            m_prev = m_ref[0:1, hh:hh + 1]
            d_log = jnp.where(causal, bc - br + ir, -jnp.inf)
            inter = bc + m_prev
            m_t = jnp.maximum(inter, jnp.max(d_log, axis=-1, keepdims=True))
            w_intra = jnp.exp(d_log - m_t)
            sc = jnp.exp(inter - m_t)
            c_st, n_st = c_ref[hh], n_ref[hh]
            qk = _bdot_nt(q, k)
            qc = _bdot(q, c_st)
            yield
            qk = qk * w_intra
            b_last = bc[ML_CHUNK - 1:ML_CHUNK, :]
            g_end = b_last - bc + ic
            m_new = jnp.maximum(b_last + m_prev, jnp.max(g_end, axis=0, keepdims=True))
            kw = k * jnp.exp(g_end - m_new)
            carry_scale = jnp.exp(b_last + m_prev - m_new)
            num = _bdot(qk, v) + sc * qc
            c_ref[hh] = carry_scale * c_st + _bdot_tn(kw, v)
            yield
            den = jnp.sum(qk, axis=-1, keepdims=True) + sc * jnp.sum(q * n_st, axis=-1, keepdims=True)
            ob_ref[rows, hh * ML_DV:(hh + 1) * ML_DV] = num / jnp.maximum(jnp.abs(den), jnp.exp(-m_t))
            n_ref[hh] = carry_scale * n_st + jnp.sum(kw, axis=0, keepdims=True)
            m_ref[0:1, hh:hh + 1] = m_new

        _lockstep([head(hh) for hh in range(nh)])
    normed = [_rms(ob_ref[:, hh * ML_DV:(hh + 1) * ML_DV], on_ref[:, hh * ML_DV:(hh + 1) * ML_DV])
              for hh in range(nh)]
    o_pre = proj_ref[:, 2 * hk + hv:2 * hk + 2 * hv]
    y = jnp.concatenate(normed, axis=-1) * _sigmoid(o_pre)
    o_ref[...] = x + _bdot(y, wo_ref[...])


def _mlstm(x, n1, w_in, b_if, out_norm, w_out, tb=1024):
    s, d = x.shape
    nh = ML_HEADS
    hk, hv = nh * ML_DK, nh * ML_DV
    nm = 2 * hk + 2 * hv
    wm = w_in[:, :nm].astype(BF16)
    w_if = w_in[:, nm:]
    wif = jnp.pad(w_if, ((0, 0), (0, LANES - 2 * nh))).astype(BF16)
    wift = w_if.T.astype(BF16)
    return pl.pallas_call(
        _mlstm_kernel,
        grid=(s // tb,),
        in_specs=[pl.BlockSpec((tb, d), lambda i: (i, 0)), _full((1, d)), _full((d, nm)),
                  _full((d, LANES)), _full((2 * nh, d)), _full((1, LANES)), _full((2 * nh, 1)),
                  _full((1, hv)), _full((hv, d))],
        out_specs=pl.BlockSpec((tb, d), lambda i: (i, 0)),
        out_shape=jax.ShapeDtypeStruct((s, d), F32),
        scratch_shapes=[pltpu.VMEM((tb, d), BF16), pltpu.VMEM((tb, nm), F32), pltpu.VMEM((tb, LANES), F32),
                        pltpu.VMEM((tb, hv), F32), pltpu.VMEM((nh, ML_DK, ML_DV), F32),
                        pltpu.VMEM((nh, 1, ML_DK), F32), pltpu.VMEM((8, LANES), F32)],
        compiler_params=_params(("arbitrary",)),
        name="mlstm_mixer",
    )(x, _row(n1), wm, wif, wift, _row(b_if, LANES), b_if.reshape(2 * nh, 1).astype(F32),
      _row(out_norm), w_out.astype(BF16))


def _rwkv_proj_kernel(x_ref, n1_ref, mu_ref, wr_ref, wk_ref, wv_ref, w0_ref, w1_ref, w2_ref,
                      a0_ref, a1_ref, a2_ref, g1_ref, g2_ref, kk_ref, ka_ref,
                      r_ref, lw_ref, k_ref, v_ref, kkraw_ref, a_ref, g_ref, car_ref):
    tm = x_ref.shape[0]

    @pl.when(pl.program_id(0) == 0)
    def _():
        car_ref[...] = jnp.zeros_like(car_ref)

    h = _rms(x_ref[...], n1_ref[...])
    rowi = lax.broadcasted_iota(jnp.int32, h.shape, 0)
    h_prev = jnp.where(rowi == 0, car_ref[7:8, :], pltpu.roll(h, 1, 0))
    car_ref[...] = h[tm - 8:, :]
    xx = h_prev - h
    xr, xw, xk, xv, xa, xg = (h + xx * mu_ref[j:j + 1, :] for j in range(6))
    r = _bdot(xr, wr_ref[...])
    k = _bdot(xk, wk_ref[...])
    v = _bdot(xv, wv_ref[...])
    w_log = -_softplus(-(w0_ref[...] + _bdot(jnp.tanh(_bdot(xw, w1_ref[...])), w2_ref[...]))) - 0.5
    a = _sigmoid(a0_ref[...] + _bdot(_bdot(xa, a1_ref[...]), a2_ref[...]))
    g = _bdot(_sigmoid(_bdot(xg, g1_ref[...])), g2_ref[...])
    r_ref[...] = r
    lw_ref[...] = -jnp.exp(w_log)
    k_ref[...] = k * (1.0 + (a - 1.0) * ka_ref[...])
    v_ref[...] = v
    kkraw_ref[...] = k * kk_ref[...]
    a_ref[...] = a
    g_ref[...] = g


def _rwkv_proj(x, n1, mu, w_rkv, w0, w1, w2, a0, a1, a2, g1, g2, k_k, k_a, tm=512):
    s, d = x.shape

    def padc(w):
        return jnp.pad(w, ((0, 0), (0, LANES - w.shape[1]))).astype(BF16) if w.shape[1] < LANES else w.astype(BF16)

    def padr(w):
        return jnp.pad(w, ((0, LANES - w.shape[0]), (0, 0))).astype(BF16) if w.shape[0] < LANES else w.astype(BF16)

    mu8 = jnp.pad(mu, ((0, 8 - mu.shape[0]), (0, 0)))
    blk = pl.BlockSpec((tm, d), lambda i: (i, 0))
    out = jax.ShapeDtypeStruct((s, d), F32)
    return pl.pallas_call(
        _rwkv_proj_kernel,
        grid=(s // tm,),
        in_specs=[blk, _full((1, d)), _full((8, d)), _full((d, d)), _full((d, d)), _full((d, d)),
                  _full((1, d)), _full((d, LANES)), _full((LANES, d)),
                  _full((1, d)), _full((d, LANES)), _full((LANES, d)),
                  _full((d, LANES)), _full((LANES, d)), _full((1, d)), _full((1, d))],
        out_specs=[blk] * 7,
        out_shape=[out] * 7,
        scratch_shapes=[pltpu.VMEM((8, d), F32)],
        compiler_params=_params(("arbitrary",)),
        name="rwkv_proj",
    )(x, _row(n1), mu8, w_rkv[0].astype(BF16), w_rkv[1].astype(BF16), w_rkv[2].astype(BF16),
      _row(w0), padc(w1), padr(w2), _row(a0), padc(a1), padr(a2), padc(g1), padr(g2), _row(k_k), _row(k_a))


def _unit_lower_inverse(a_strict, row, col):
    n = a_strict.shape[0]
    eye = (row == col).astype(F32)
    blk16 = (row // 16) == (col // 16)
    p = jnp.where(blk16, a_strict, 0.0)
    inv = eye + p
    p = _bdot(p, p)
    yield
    for _ in range(2):
        both = _bdot(jnp.concatenate([p, inv], axis=0), p)
        yield
        p, inv = both[:n], inv + both[n:]
    inv16 = inv + _bdot(inv, p)
    yield
    blk32 = (row // 32) == (col // 32)
    off32 = jnp.where(blk32 & ~blk16, a_strict, 0.0)
    off64 = jnp.where(blk32, 0.0, a_strict)
    x = _bdot(inv16, jnp.concatenate([off32, off64], axis=1))
    yield
    x32, x64 = x[:, :n], x[:, n:]
    y = _bdot(x32, jnp.concatenate([inv16, x64], axis=1))
    yield
    inv32 = inv16 + y[:, :n]
    return inv32 + _bdot(x64 + y[:, n:], inv32)


def _rwkv_scan_kernel(r_ref, lw_ref, k_ref, v_ref, kkraw_ref, a_ref, g_ref, rk_ref, gg_ref, gb_ref, o_ref, s_ref):
    tb = r_ref.shape[0]
    n = RWKV_HEAD
    grp = 2 * CHUNK
    big = 2 * grp

    @pl.when(pl.program_id(1) == 0)
    def _():
        s_ref[...] = jnp.zeros_like(s_ref)

    rown = lax.broadcasted_iota(jnp.int32, (grp, LANES), 0)
    head0 = lax.broadcasted_iota(jnp.int32, (1, LANES), 1) < n
    chunk0 = rown < CHUNK
    row, col = _tri_masks(big)
    same = (row // CHUNK) == (col // CHUNK)
    incl = same & (row >= col)
    strict = same & (row > col)
    tri = incl[:grp, :grp].astype(BF16)
    tri3 = jnp.concatenate([tri, tri, tri], axis=1)
    srow = lax.broadcasted_iota(jnp.int32, (big, LANES), 0)
    slane = lax.broadcasted_iota(jnp.int32, (big, LANES), 1)
    srow_chunk0 = (srow // CHUNK) % 2 == 0
    diag = (srow % grp) == slane
    own = ((rown < n) == head0)

    def stack(t):
        return jnp.concatenate([jnp.where(head0, t, 0.0), jnp.where(head0, 0.0, t)], axis=0)

    def blockdiag(t):
        s = stack(t)
        return jnp.concatenate([jnp.where(srow_chunk0, s, 0.0), jnp.where(srow_chunk0, 0.0, s)], axis=1)

    def spread(t):
        return jnp.concatenate([jnp.where(srow < grp, t, 0.0), jnp.where(srow < grp, 0.0, t)], axis=1)

    def head_sum(t):
        return jnp.where(head0, jnp.sum(jnp.where(head0, t, 0.0), axis=-1, keepdims=True),
                         jnp.sum(jnp.where(head0, 0.0, t), axis=-1, keepdims=True))

    def per_chunk(t, c):
        return jnp.concatenate([t[c * CHUNK:(c + 1) * CHUNK], t[grp + c * CHUNK:grp + (c + 1) * CHUNK]], axis=0)

    def phase1(g):
        rows = slice(g * grp, (g + 1) * grp)
        r, lw, k, v = r_ref[rows, :], lw_ref[rows, :], k_ref[rows, :], v_ref[rows, :]
        kk, a = kkraw_ref[rows, :], a_ref[rows, :]
        kk = kk / jnp.maximum(jnp.sqrt(head_sum(kk * kk)), 1e-12)
        cum = _cumsum_rows(tri3, lw)
        yield
        cum_last = jnp.where(chunk0, cum[CHUNK - 1:CHUNK, :], cum[grp - 1:grp, :])
        e_neg, e_end = jnp.exp(-cum), jnp.exp(cum_last - cum)
        b_vec = kk * a
        a_t = stack(-kk * jnp.exp(cum - lw))
        r_t = stack(r * jnp.exp(cum))
        v_s = stack(v)
        a4 = _bdot_nt(jnp.concatenate([a_t, r_t], axis=0),
                      jnp.concatenate([b_vec * e_neg, k * e_neg], axis=0))
        yield
        a_ab = jnp.where(strict, spread(a4[:big, :grp]), 0.0)
        a_ak = jnp.where(strict, spread(a4[:big, grp:]), 0.0)
        a_rb = jnp.where(incl, spread(a4[big:, :grp]), 0.0)
        a_rk = jnp.where(incl, spread(a4[big:, grp:]), 0.0)
        akv = _bdot(a_ak, v_s)
        t_inv = yield from _unit_lower_inverse(a_ab, row, col)
        yield
        z = _bdot(t_inv, jnp.concatenate([a_t, akv], axis=1))
        yield
        zv = jnp.concatenate([z, jnp.concatenate([jnp.zeros_like(v_s), v_s], axis=1)], axis=0)
        gh = _bdot_tn(jnp.concatenate([blockdiag(b_vec * e_end), blockdiag(k * e_end)], axis=0), zv)
        ry = _bdot(jnp.concatenate([a_rb, a_rk], axis=1), zv)
        p_end = jnp.exp(jnp.concatenate([cum[CHUNK - 1:CHUNK, :], cum[grp - 1:grp, :]], axis=0))
        g_all = gh[:, :LANES] + jnp.where(diag, jnp.where(srow < grp, p_end[0:1, :], p_end[1:2, :]), 0.0)
        h_all = gh[:, LANES:]
        rp_all = r_t + ry[:, :LANES]
        y0_all = ry[:, LANES:]
        return g_all, h_all, rp_all, y0_all

    def emit(g, c, y):
        crow = slice(g * grp + c * CHUNK, g * grp + (c + 1) * CHUNK)
        mean = jnp.sum(y, axis=-1, keepdims=True) * (1.0 / n)
        dev = jnp.where(own, y - mean, 0.0)
        var = jnp.sum(dev * dev, axis=-1, keepdims=True) * (1.0 / n)
        yn = dev * lax.rsqrt(var + RWKV_GN_EPS)
        yn = yn[:CHUNK] + yn[CHUNK:]
        rc, kc, vc = r_ref[crow, :], k_ref[crow, :], v_ref[crow, :]
        bonus = head_sum(rc * kc * rk_ref[...])[:CHUNK] * vc
        o_ref[crow, :] = ((yn * gg_ref[...] + gb_ref[...] + bonus) * g_ref[crow, :]).astype(BF16)

    def phase2(g0, staged):
        for g, (g_all, h_all, rp_all, y0_all) in enumerate(staged, g0):
            for c in range(2):
                lhs = jnp.concatenate([g_all[c * grp:(c + 1) * grp], per_chunk(rp_all, c)], axis=0)
                both = _bdot(lhs, s_ref[...])
                yield
                s_ref[...] = both[:grp] + h_all[c * grp:(c + 1) * grp]
                emit(g, c, both[grp:] + per_chunk(y0_all, c))

    n_groups, wave = tb // grp, 4
    pending = None
    for g0 in range(0, n_groups, wave):
        gens = [phase1(g) for g in range(g0, min(g0 + wave, n_groups))]
        done = _lockstep(gens + ([pending] if pending is not None else []))
        pending = phase2(g0, done[:len(gens)])
    _lockstep([pending])


def _rwkv_scan(r, lw, k, v, kkraw, a, g, r_k, gn_g, gn_b, tb=2048):
    s, d = r.shape
    tb = min(tb, s)
    blk = pl.BlockSpec((tb, LANES), lambda p, i: (i, p))
    vec = pl.BlockSpec((1, LANES), lambda p, i: (0, p))
    return pl.pallas_call(
        _rwkv_scan_kernel,
        grid=(d // LANES, s // tb),
        in_specs=[blk] * 7 + [vec] * 3,
        out_specs=blk,
        out_shape=jax.ShapeDtypeStruct((s, d), BF16),
        scratch_shapes=[pltpu.VMEM((LANES, LANES), F32)],
        compiler_params=_params(("arbitrary", "arbitrary")),
        name="rwkv_scan",
    )(r, lw, k, v, kkraw, a, g, _row(r_k), _row(gn_g), _row(gn_b))


def _out_proj_kernel(x_ref, a_ref, w_ref, o_ref):
    o_ref[...] = x_ref[...] + _dot(a_ref[...], w_ref[...])


def _out_proj(x, a, w, tm=1024):
    s, d = x.shape
    blk = pl.BlockSpec((tm, d), lambda i: (i, 0))
    return pl.pallas_call(
        _out_proj_kernel,
        grid=(s // tm,),
        in_specs=[blk, blk, _full((d, d))],
        out_specs=blk,
        out_shape=jax.ShapeDtypeStruct((s, d), F32),
        compiler_params=_params(("arbitrary",)),
        name="out_proj",
    )(x, a, w.astype(BF16))


def _sb_qkv_kernel(x_ref, n1_ref, w_ref, q_ref, k_ref, v_ref):
    d = x_ref.shape[1]
    h = _rms(x_ref[...], n1_ref[...]).astype(BF16)
    qkv = _dot(h, w_ref[...])
    q_ref[...] = (qkv[:, :d] * SB_HEAD_DIM ** -0.5).astype(BF16)
    k_ref[...] = qkv[:, d:2 * d].astype(BF16)
    v_ref[...] = qkv[:, 2 * d:].astype(BF16)


def _sb_qkv(x, n1, w_qkv, tm=1024):
    s, d = x.shape
    blk = pl.BlockSpec((tm, d), lambda i: (i, 0))
    out = jax.ShapeDtypeStruct((s, d), BF16)
    return pl.pallas_call(
        _sb_qkv_kernel,
        grid=(s // tm,),
        in_specs=[blk, _full((1, d)), _full((d, 3 * d))],
        out_specs=[blk] * 3,
        out_shape=[out] * 3,
        compiler_params=_params(("arbitrary",)),
        name="sb_qkv",
    )(x, _row(n1), w_qkv.astype(BF16))


def _sb_attn_kernel(q_ref, k_ref, v_ref, o_ref, *, tq):
    qi = pl.program_id(1)
    head0 = lax.broadcasted_iota(jnp.int32, (1, LANES), 1) < SB_HEAD_DIM
    q = q_ref[...]
    zero = jnp.zeros_like(q)
    q2 = jnp.concatenate([jnp.where(head0, q, zero), jnp.where(head0, zero, q)], axis=0)
    row, col = _tri_masks(tq)
    before = jnp.concatenate([col < row, col < row], axis=0)
    later = (row > col).astype(BF16)
    later2 = jnp.concatenate([later, later], axis=0)

    def pair(kb, c, acc, diagonal):
        valid = kb >= 1
        ks = (pl.ds(pl.multiple_of(kb * tq, tq), tq), pl.ds(pl.multiple_of(jnp.maximum(kb - 1, 0) * tq, tq), tq))
        zs = [_dot_nt(q2, k_ref[s, :]) for s in ks]
        sps, excls = [], []
        for j, z in enumerate(zs):
            sp = jnp.maximum(z, 0.0) + jnp.log(1.0 + jnp.exp(-jnp.abs(z)))
            if diagonal and j == 0:
                sp = jnp.where(before, sp, 0.0)
            hi = sp.astype(BF16)
            sps.append(sp)
            excls.append(_dot(jnp.concatenate([hi, (sp - hi.astype(F32)).astype(BF16)], axis=1), later2))
        tot0 = excls[0][:, 0:1] + sps[0][:, 0:1]
        tot1 = jnp.where(valid, excls[1][:, 0:1] + sps[1][:, 0:1], 0.0)
        w0 = jnp.exp(zs[0] - sps[0] - excls[0] + c)
        if diagonal:
            w0 = jnp.where(before, w0, 0.0)
        w1 = jnp.where(valid, jnp.exp(zs[1] - sps[1] - excls[1] + (c - tot0)), 0.0)
        pv = _dot(jnp.concatenate([w0.astype(BF16), w1.astype(BF16)], axis=1),
                  jnp.concatenate([v_ref[ks[0], :], v_ref[ks[1], :]], axis=0))
        return c - tot0 - tot1, acc + pv

    c, acc = pair(qi, jnp.zeros((2 * tq, 1), F32), jnp.zeros((2 * tq, LANES), F32), True)

    def cond(st):
        kb, c, _ = st
        return jnp.logical_and(kb >= 0, jnp.max(c) > SB_ZERO_LOG)

    def body(st):
        kb, c, acc = st
        c, acc = pair(kb, c, acc, False)
        return kb - 2, c, acc

    _, _, acc = lax.while_loop(cond, body, (qi - 2, c, acc))
    o_ref[...] = jnp.where(head0, acc[:tq], acc[tq:]).astype(BF16)


def _sb_attn(q, k, v, tq=256):
    s, d = q.shape
    return pl.pallas_call(
        functools.partial(_sb_attn_kernel, tq=tq),
        grid=(d // LANES, s // tq),
        in_specs=[pl.BlockSpec((tq, LANES), lambda p, i: (i, p)),
                  pl.BlockSpec((s, LANES), lambda p, i: (0, p)),
                  pl.BlockSpec((s, LANES), lambda p, i: (0, p))],
        out_specs=pl.BlockSpec((tq, LANES), lambda p, i: (i, p)),
        out_shape=jax.ShapeDtypeStruct((s, d), BF16),
        compiler_params=_params(("arbitrary", "arbitrary")),
        name="sb_attn",
    )(q, k, v)


def _stick_breaking(x, n1, w_qkv, w_out):
    q, k, v = _sb_qkv(x, n1, w_qkv)
    return _out_proj(x, _sb_attn(q, k, v), w_out)


def _rwkv7(x, n1, mu, w_rkv, w0, w1, w2, a0, a1, a2, g1, g2, k_k, k_a, r_k, gn_g, gn_b, w_out):
    r, lw, k, v, kkraw, a, g = _rwkv_proj(x, n1, mu, w_rkv, w0, w1, w2, a0, a1, a2, g1, g2, k_k, k_a)
    return _out_proj(x, _rwkv_scan(r, lw, k, v, kkraw, a, g, r_k, gn_g, gn_b), w_out)


def kernel(x, l0_norm1, l0_gla_w_in, l0_gla_w_alpha_up, l0_gla_b_alpha, l0_gla_out_norm, l0_gla_w_out, l0_norm2, l0_ffn_w_up, l0_ffn_conv_w, l0_ffn_conv_b, l0_ffn_w_down, l1_norm1, l1_rwkv_mu, l1_rwkv_w_rkv, l1_rwkv_w0, l1_rwkv_w1, l1_rwkv_w2, l1_rwkv_a0, l1_rwkv_a1, l1_rwkv_a2, l1_rwkv_g1, l1_rwkv_g2, l1_rwkv_k_k, l1_rwkv_k_a, l1_rwkv_r_k, l1_rwkv_gn_g, l1_rwkv_gn_b, l1_rwkv_w_out, l1_norm2, l1_ffn_w_up, l1_ffn_conv_w, l1_ffn_conv_b, l1_ffn_w_down, l2_norm1, l2_sb_w_qkv, l2_sb_w_out, l2_norm2, l2_ffn_w_up, l2_ffn_conv_w, l2_ffn_conv_b, l2_ffn_w_down, l3_norm1, l3_ml_w_in, l3_ml_b_if, l3_ml_out_norm, l3_ml_w_out, l3_norm2, l3_ffn_w_up, l3_ffn_conv_w, l3_ffn_conv_b, l3_ffn_w_down, final_norm):
    b, s, d = x.shape
    outs = []
    for bi in range(b):
        h = x[bi]
        h = _gla(h, l0_norm1, l0_gla_w_in, l0_gla_w_alpha_up, l0_gla_b_alpha, l0_gla_out_norm, l0_gla_w_out)
        h = _ffn(h, l0_norm2, l0_ffn_w_up, l0_ffn_conv_w, l0_ffn_conv_b, l0_ffn_w_down)
        h = _rwkv7(h, l1_norm1, l1_rwkv_mu, l1_rwkv_w_rkv, l1_rwkv_w0, l1_rwkv_w1, l1_rwkv_w2,
                   l1_rwkv_a0, l1_rwkv_a1, l1_rwkv_a2, l1_rwkv_g1, l1_rwkv_g2,
                   l1_rwkv_k_k, l1_rwkv_k_a, l1_rwkv_r_k, l1_rwkv_gn_g, l1_rwkv_gn_b, l1_rwkv_w_out)
        h = _ffn(h, l1_norm2, l1_ffn_w_up, l1_ffn_conv_w, l1_ffn_conv_b, l1_ffn_w_down)
        h = _stick_breaking(h, l2_norm1, l2_sb_w_qkv, l2_sb_w_out)
        h = _ffn(h, l2_norm2, l2_ffn_w_up, l2_ffn_conv_w, l2_ffn_conv_b, l2_ffn_w_down)
        h = _mlstm(h, l3_norm1, l3_ml_w_in, l3_ml_b_if, l3_ml_out_norm, l3_ml_w_out)
        h = _ffn(h, l3_norm2, l3_ffn_w_up, l3_ffn_conv_w, l3_ffn_conv_b, l3_ffn_w_down, final_g=final_norm)
        outs.append(h)
    return jnp.stack(outs, axis=0)
```

```python
import functools

import jax
import jax.numpy as jnp
from jax import lax
from jax.experimental import pallas as pl
from jax.experimental.pallas import tpu as pltpu

F32 = jnp.float32
BF16 = jnp.bfloat16

NORM_EPS = 1e-6
CHUNK = 64
GLA_HEADS, GLA_DK, GLA_DV, GLA_TAU = 4, 128, 256, 16.0
ML_HEADS, ML_DK, ML_DV = 4, 128, 256
ML_CHUNK = 256
RWKV_HEAD = 64
RWKV_GN_EPS = 64e-5
SB_HEADS, SB_HEAD_DIM = 16, 64
CONV_WIDTH = 3
LANES = 128
SB_ZERO_LOG = -104.0


def _rms(x, g, eps=NORM_EPS):
    return x * lax.rsqrt(jnp.mean(x * x, axis=-1, keepdims=True) + eps) * g


def _softplus(x):
    return jnp.maximum(x, 0.0) + jnp.log1p(jnp.exp(-jnp.abs(x)))


def _log_sigmoid(x):
    return -_softplus(-x)


def _sigmoid(x):
    return 1.0 / (1.0 + jnp.exp(-x))


def _dot(a, b, **kw):
    return jnp.dot(a, b, preferred_element_type=F32, **kw)


def _dot_nt(a, b, **kw):
    return lax.dot_general(a, b, (((1,), (1,)), ((), ())), preferred_element_type=F32, **kw)


def _dot_tn(a, b, **kw):
    return lax.dot_general(a, b, (((0,), (0,)), ((), ())), preferred_element_type=F32, **kw)


def _bdot(a, b):
    return _dot(a.astype(BF16), b.astype(BF16))


def _bdot_nt(a, b):
    return _dot_nt(a.astype(BF16), b.astype(BF16))


def _bdot_tn(a, b):
    return _dot_tn(a.astype(BF16), b.astype(BF16))


def _split3(x, axis):
    hi = x.astype(BF16)
    r1 = x - hi.astype(F32)
    mid = r1.astype(BF16)
    lo = (r1 - mid.astype(F32)).astype(BF16)
    return jnp.concatenate([hi, mid, lo], axis=axis)


def _cumsum_rows(tri3, x):
    return _dot(tri3, _split3(x, 0))


def _lockstep(gens):
    out = [None] * len(gens)
    live = list(range(len(gens)))
    while live:
        for i in list(live):
            try:
                next(gens[i])
            except StopIteration as stop:
                out[i] = stop.value
                live.remove(i)
    return out


def _tri_masks(n):
    row = lax.broadcasted_iota(jnp.int32, (n, n), 0)
    col = lax.broadcasted_iota(jnp.int32, (n, n), 1)
    return row, col


def _full(shape):
    return pl.BlockSpec(shape, lambda *_: (0,) * len(shape), pipeline_mode=pl.Buffered(1))


def _params(sem):
    return pltpu.CompilerParams(dimension_semantics=sem, vmem_limit_bytes=56 * 1024 * 1024)


def _row(v, width=None):
    v = v.reshape(1, -1).astype(F32)
    if width is not None and v.shape[1] < width:
        v = jnp.pad(v, ((0, 0), (0, width - v.shape[1])))
    return v


def _ffn_kernel(x_ref, g_ref, wup_ref, cw_ref, wd_ref, fg_ref, o_ref, h_ref, act_ref, car_ref, *, final_norm, fc):
    tm = x_ref.shape[0]
    f = wd_ref.shape[0]

    @pl.when(pl.program_id(0) == 0)
    def _():
        car_ref[...] = jnp.zeros_like(car_ref)

    x = x_ref[...]
    h_ref[...] = _rms(x, g_ref[...]).astype(BF16)
    rowi = lax.broadcasted_iota(jnp.int32, (tm, fc), 0)

    def conv_up(cols):
        u = _dot(h_ref[...], wup_ref[:, cols])
        c2, c1 = car_ref[6:7, cols], car_ref[7:8, cols]
        s1 = jnp.where(rowi == 0, c1, pltpu.roll(u, 1, 0))
        s2 = jnp.where(rowi == 0, c2, jnp.where(rowi == 1, c1, pltpu.roll(u, 2, 0)))
        car_ref[:, cols] = u[tm - 8:, :]
        return s2 * cw_ref[0:1, cols] + s1 * cw_ref[1:2, cols] + u * cw_ref[2:3, cols] + cw_ref[3:4, cols]

    for c in range(f // fc):
        gate = conv_up(slice(c * fc, (c + 1) * fc))
        up = conv_up(slice(f + c * fc, f + (c + 1) * fc))
        act_ref[:, c * fc:(c + 1) * fc] = (gate * _sigmoid(gate) * up).astype(BF16)
    y = x + _dot(act_ref[...], wd_ref[...])
    if final_norm:
        y = _rms(y, fg_ref[...])
    o_ref[...] = y


def _ffn(x, norm_g, w_up, conv_w, conv_b, w_down, final_g=None, tm=1024, fc=256):
    s, d = x.shape
    f = w_down.shape[0]
    taps = jnp.concatenate([conv_w, conv_b[None, :], jnp.zeros((4, 2 * f), F32)], axis=0)
    fg = _row(final_g if final_g is not None else jnp.ones((d,), F32))
    return pl.pallas_call(
        functools.partial(_ffn_kernel, final_norm=final_g is not None, fc=fc),
        grid=(s // tm,),
        in_specs=[pl.BlockSpec((tm, d), lambda i: (i, 0)), _full((1, d)), _full((d, 2 * f)), _full((8, 2 * f)),
                  _full((f, d)), _full((1, d))],
        out_specs=pl.BlockSpec((tm, d), lambda i: (i, 0)),
        out_shape=jax.ShapeDtypeStruct((s, d), F32),
        scratch_shapes=[pltpu.VMEM((tm, d), BF16), pltpu.VMEM((tm, f), BF16), pltpu.VMEM((8, 2 * f), F32)],
        compiler_params=_params(("arbitrary",)),
        name="conv_ffn",
    )(x, _row(norm_g), w_up.astype(BF16), taps, w_down.astype(BF16), fg)


def _gla_kernel(x_ref, n1_ref, w_ref, wup_ref, ba_ref, on_ref, wo_ref, o_ref,
                proj_ref, la_ref, ob_ref, st_ref):
    tb = x_ref.shape[0]
    hk, hv = GLA_HEADS * GLA_DK, GLA_HEADS * GLA_DV
    nm = 2 * hk + 2 * hv

    @pl.when(pl.program_id(0) == 0)
    def _():
        st_ref[...] = jnp.zeros_like(st_ref)

    x = x_ref[...]
    h = _rms(x, n1_ref[...]).astype(BF16)
    proj_ref[...] = _dot(h, w_ref[:, :nm])
    a_low = _dot(h, w_ref[:, nm:])
    la_ref[...] = _log_sigmoid(_bdot(a_low, wup_ref[...]) + ba_ref[...]) * (1.0 / GLA_TAU)
    grp = 4 * CHUNK
    row, col = _tri_masks(grp)
    causal = ((row // CHUNK) == (col // CHUNK)) & (row >= col)
    tri = causal.astype(BF16)
    tri3 = jnp.concatenate([tri, tri, tri], axis=1)
    rowc = lax.broadcasted_iota(jnp.int32, (grp, hk), 0) // CHUNK
    head_of_lane = lax.broadcasted_iota(jnp.int32, (1, hk), 1) // GLA_DK
    scale = GLA_DK ** -0.5

    def stack(t):
        return jnp.concatenate([jnp.where(head_of_lane == hh, t, 0.0) for hh in range(GLA_HEADS)], axis=0)

    for g in range(tb // grp):
        r0 = g * grp
        rows = slice(r0, r0 + grp)
        b = _cumsum_rows(tri3, la_ref[rows, :])
        b_last = b[CHUNK - 1:CHUNK, :]
        for c in range(1, 4):
            b_last = jnp.where(rowc == c, b[(c + 1) * CHUNK - 1:(c + 1) * CHUNK, :], b_last)
        q_dec = proj_ref[rows, 0:hk] * scale * jnp.exp(b)
        k = proj_ref[rows, hk:2 * hk]
        k_inv = k * jnp.exp(-b)
        k_end = k * jnp.exp(b_last - b)
        scores = [jnp.where(causal, _bdot_nt(q_dec[:, hh * GLA_DK:(hh + 1) * GLA_DK],
                                             k_inv[:, hh * GLA_DK:(hh + 1) * GLA_DK]), 0.0)
                  for hh in range(GLA_HEADS)]
        upd = []
        for c in range(4):
            cr = slice(r0 + c * CHUNK, r0 + (c + 1) * CHUNK)
            v_s = jnp.concatenate([proj_ref[cr, 2 * hk + hh * GLA_DV:2 * hk + (hh + 1) * GLA_DV]
                                   for hh in range(GLA_HEADS)], axis=0)
            upd.append(_bdot_tn(v_s, stack(k_end[c * CHUNK:(c + 1) * CHUNK, :])))
        intra = [_bdot(scores[hh], proj_ref[rows, 2 * hk + hh * GLA_DV:2 * hk + (hh + 1) * GLA_DV])
                 for hh in range(GLA_HEADS)]
        st = st_ref[...]
        inter = []
        for c in range(4):
            inter.append(_bdot_nt(stack(q_dec[c * CHUNK:(c + 1) * CHUNK, :]), st))
            st = st * jnp.exp(b[(c + 1) * CHUNK - 1:(c + 1) * CHUNK, :]) + upd[c]
        st_ref[...] = st
        for hh in range(GLA_HEADS):
            ob_ref[rows, hh * GLA_DV:(hh + 1) * GLA_DV] = intra[hh] + jnp.concatenate(
                [inter[c][hh * CHUNK:(hh + 1) * CHUNK, :] for c in range(4)], axis=0)
    normed = [_rms(ob_ref[:, hh * GLA_DV:(hh + 1) * GLA_DV], on_ref[:, hh * GLA_DV:(hh + 1) * GLA_DV])
              for hh in range(GLA_HEADS)]
    r = proj_ref[:, 2 * hk + hv:2 * hk + 2 * hv]
    y = jnp.concatenate(normed, axis=-1) * (r * _sigmoid(r))
    o_ref[...] = x + _bdot(y, wo_ref[...])


def _gla(x, n1, w_in, w_alpha_up, b_alpha, out_norm, w_out, tb=1024):
    s, d = x.shape
    hk, hv = GLA_HEADS * GLA_DK, GLA_HEADS * GLA_DV
    nm = 2 * hk + 2 * hv
    rank = w_alpha_up.shape[0]
    w = jnp.pad(w_in, ((0, 0), (0, LANES - rank))).astype(BF16)
    wup = jnp.pad(w_alpha_up, ((0, LANES - rank), (0, 0))).astype(BF16)
    return pl.pallas_call(
        _gla_kernel,
        grid=(s // tb,),
        in_specs=[pl.BlockSpec((tb, d), lambda i: (i, 0)), _full((1, d)), _full((d, nm + LANES)),
                  _full((LANES, hk)), _full((1, hk)), _full((1, hv)), _full((hv, d))],
        out_specs=pl.BlockSpec((tb, d), lambda i: (i, 0)),
        out_shape=jax.ShapeDtypeStruct((s, d), F32),
        scratch_shapes=[pltpu.VMEM((tb, nm), F32), pltpu.VMEM((tb, hk), F32), pltpu.VMEM((tb, hv), F32),
                        pltpu.VMEM((GLA_DV, hk), F32)],
        compiler_params=_params(("arbitrary",)),
        name="gla_mixer",
    )(x, _row(n1), w, wup, _row(b_alpha), _row(out_norm), w_out.astype(BF16))


def _mlstm_kernel(x_ref, n1_ref, w_ref, wift_ref, bif_ref, bift_ref, on_ref, wo_ref, o_ref,
                  h_ref, proj_ref, gc_ref, ob_ref, c_ref, n_ref, m_ref):
    tb = x_ref.shape[0]
    nh = ML_HEADS
    hk, hv = nh * ML_DK, nh * ML_DV

    @pl.when(pl.program_id(0) == 0)
    def _():
        c_ref[...] = jnp.zeros_like(c_ref)
        n_ref[...] = jnp.zeros_like(n_ref)
        m_ref[...] = jnp.zeros_like(m_ref)

    x = x_ref[...]
    h = _rms(x, n1_ref[...]).astype(BF16)
    h_ref[...] = h
    nm = 2 * hk + 2 * hv
    proj_ref[...] = _dot(h, w_ref[:, :nm])
    gc_ref[...] = _dot(h, w_ref[:, nm:]) + bif_ref[...]
    row, col = _tri_masks(ML_CHUNK)
    causal = row >= col
    tri = causal.astype(BF16)
    tri3 = jnp.concatenate([tri, tri, tri], axis=1)
    scale = ML_DK ** -0.5

    for c in range(tb // ML_CHUNK):
        rows = slice(c * ML_CHUNK, (c + 1) * ML_CHUNK)
        g_col = gc_ref[rows, :]
        g_row = _dot_nt(wift_ref[...], h_ref[rows, :]) + bift_ref[...]
        b_col = _cumsum_rows(tri3, _log_sigmoid(g_col))
        b_row = _dot_nt(_split3(_log_sigmoid(g_row), 1), tri3)
        def head(hh):
            bc, ic = b_col[:, nh + hh:nh + hh + 1], g_col[:, hh:hh + 1]
            br, ir = b_row[nh + hh:nh + hh + 1, :], g_row[hh:hh + 1, :]
            q = proj_ref[rows, hh * ML_DK:(hh + 1) * ML_DK]
            k = proj_ref[rows, hk + hh * ML_DK:hk + (hh + 1) * ML_DK] * scale
            v = proj_ref[rows, 2 * hk + hh * ML_DV:2 * hk + (hh + 1) * ML_DV]
            m_prev = m_ref[0:1, hh:hh + 1]
            d_log = jnp.where(causal, bc - br + ir, -jnp.inf)
            inter = bc + m_prev
            m_t = jnp.maximum(inter, jnp.max(d_log, axis=-1, keepdims=True))
            w_intra = jnp.exp(d_log - m_t)
            sc = jnp.exp(inter - m_t)
            c_st, n_st = c_ref[hh], n_ref[hh]
            qk = _bdot_nt(q, k)
            qc = _bdot(q, c_st)
            yield
            qk = qk * w_intra
            b_last = bc[ML_CHUNK - 1:ML_CHUNK, :]
            g_end = b_last - bc + ic
            m_new = jnp.maximum(b_last + m_prev, jnp.max(g_end, axis=0, keepdims=True))
            kw = k * jnp.exp(g_end - m_new)
            carry_scale = jnp.exp(b_last + m_prev - m_new)
            num = _bdot(qk, v) + sc * qc
            c_ref[hh] = carry_scale * c_st + _bdot_tn(kw, v)
            yield
            den = jnp.sum(qk, axis=-1, keepdims=True) + sc * jnp.sum(q * n_st, axis=-1, keepdims=True)
            ob_ref[rows, hh * ML_DV:(hh + 1) * ML_DV] = num / jnp.maximum(jnp.abs(den), jnp.exp(-m_t))
            n_ref[hh] = carry_scale * n_st + jnp.sum(kw, axis=0, keepdims=True)
            m_ref[0:1, hh:hh + 1] = m_new

        _lockstep([head(hh) for hh in range(nh)])
    normed = [_rms(ob_ref[:, hh * ML_DV:(hh + 1) * ML_DV], on_ref[:, hh * ML_DV:(hh + 1) * ML_DV])
              for hh in range(nh)]
    o_pre = proj_ref[:, 2 * hk + hv:2 * hk + 2 * hv]
    y = jnp.concatenate(normed, axis=-1) * _sigmoid(o_pre)
    o_ref[...] = x + _bdot(y, wo_ref[...])


def _mlstm(x, n1, w_in, b_if, out_norm, w_out, tb=1024):
    s, d = x.shape
    nh = ML_HEADS
    hk, hv = nh * ML_DK, nh * ML_DV
    nm = 2 * hk + 2 * hv
    w = jnp.pad(w_in, ((0, 0), (0, LANES - 2 * nh))).astype(BF16)
    wift = w_in[:, nm:].T.astype(BF16)
    return pl.pallas_call(
        _mlstm_kernel,
        grid=(s // tb,),
        in_specs=[pl.BlockSpec((tb, d), lambda i: (i, 0)), _full((1, d)), _full((d, nm + LANES)),
                  _full((2 * nh, d)), _full((1, LANES)), _full((2 * nh, 1)),
                  _full((1, hv)), _full((hv, d))],
        out_specs=pl.BlockSpec((tb, d), lambda i: (i, 0)),
        out_shape=jax.ShapeDtypeStruct((s, d), F32),
        scratch_shapes=[pltpu.VMEM((tb, d), BF16), pltpu.VMEM((tb, nm), F32), pltpu.VMEM((tb, LANES), F32),
                        pltpu.VMEM((tb, hv), F32), pltpu.VMEM((nh, ML_DK, ML_DV), F32),
                        pltpu.VMEM((nh, 1, ML_DK), F32), pltpu.VMEM((8, LANES), F32)],
        compiler_params=_params(("arbitrary",)),
        name="mlstm_mixer",
    )(x, _row(n1), w, wift, _row(b_if, LANES), b_if.reshape(2 * nh, 1).astype(F32),
      _row(out_norm), w_out.astype(BF16))


def _rwkv_proj_kernel(x_ref, n1_ref, mu_ref, wr_ref, wk_ref, wv_ref, w0_ref, w1_ref, w2_ref,
                      a0_ref, a1_ref, a2_ref, g1_ref, g2_ref, kk_ref, ka_ref,
                      r_ref, lw_ref, k_ref, v_ref, kkraw_ref, a_ref, g_ref, car_ref):
    tm = x_ref.shape[0]

    @pl.when(pl.program_id(0) == 0)
    def _():
        car_ref[...] = jnp.zeros_like(car_ref)

    h = _rms(x_ref[...], n1_ref[...])
    rowi = lax.broadcasted_iota(jnp.int32, h.shape, 0)
    h_prev = jnp.where(rowi == 0, car_ref[7:8, :], pltpu.roll(h, 1, 0))
    car_ref[...] = h[tm - 8:, :]
    xx = h_prev - h
    xr, xw, xk, xv, xa, xg = (h + xx * mu_ref[j:j + 1, :] for j in range(6))
    r = _bdot(xr, wr_ref[...])
    k = _bdot(xk, wk_ref[...])
    v = _bdot(xv, wv_ref[...])
    w_log = -_softplus(-(w0_ref[...] + _bdot(jnp.tanh(_bdot(xw, w1_ref[...])), w2_ref[...]))) - 0.5
    a = _sigmoid(a0_ref[...] + _bdot(_bdot(xa, a1_ref[...]), a2_ref[...]))
    g = _bdot(_sigmoid(_bdot(xg, g1_ref[...])), g2_ref[...])
    r_ref[...] = r
    lw_ref[...] = -jnp.exp(w_log)
    k_ref[...] = k * (1.0 + (a - 1.0) * ka_ref[...])
    v_ref[...] = v
    kkraw_ref[...] = k * kk_ref[...]
    a_ref[...] = a
    g_ref[...] = g


def _rwkv_proj(x, n1, mu, w_rkv, w0, w1, w2, a0, a1, a2, g1, g2, k_k, k_a, tm=512):
    s, d = x.shape

    def padc(w):
        return jnp.pad(w, ((0, 0), (0, LANES - w.shape[1]))).astype(BF16) if w.shape[1] < LANES else w.astype(BF16)

    def padr(w):
        return jnp.pad(w, ((0, LANES - w.shape[0]), (0, 0))).astype(BF16) if w.shape[0] < LANES else w.astype(BF16)

    mu8 = jnp.pad(mu, ((0, 8 - mu.shape[0]), (0, 0)))
    blk = pl.BlockSpec((tm, d), lambda i: (i, 0))
    out = jax.ShapeDtypeStruct((s, d), F32)
    return pl.pallas_call(
        _rwkv_proj_kernel,
        grid=(s // tm,),
        in_specs=[blk, _full((1, d)), _full((8, d)), _full((d, d)), _full((d, d)), _full((d, d)),
                  _full((1, d)), _full((d, LANES)), _full((LANES, d)),
                  _full((1, d)), _full((d, LANES)), _full((LANES, d)),
                  _full((d, LANES)), _full((LANES, d)), _full((1, d)), _full((1, d))],
        out_specs=[blk] * 7,
        out_shape=[out] * 7,
        scratch_shapes=[pltpu.VMEM((8, d), F32)],
        compiler_params=_params(("arbitrary",)),
        name="rwkv_proj",
    )(x, _row(n1), mu8, w_rkv[0].astype(BF16), w_rkv[1].astype(BF16), w_rkv[2].astype(BF16),
      _row(w0), padc(w1), padr(w2), _row(a0), padc(a1), padr(a2), padc(g1), padr(g2), _row(k_k), _row(k_a))


def _unit_lower_inverse(a_strict, row, col):
    n = a_strict.shape[0]
    eye = (row == col).astype(F32)
    blk16 = (row // 16) == (col // 16)
    p = jnp.where(blk16, a_strict, 0.0)
    inv = eye + p
    p = _bdot(p, p)
    yield
    for _ in range(2):
        both = _bdot(jnp.concatenate([p, inv], axis=0), p)
        yield
        p, inv = both[:n], inv + both[n:]
    inv16 = inv + _bdot(inv, p)
    yield
    blk32 = (row // 32) == (col // 32)
    off32 = jnp.where(blk32 & ~blk16, a_strict, 0.0)
    off64 = jnp.where(blk32, 0.0, a_strict)
    x = _bdot(inv16, jnp.concatenate([off32, off64], axis=1))
    yield
    x32, x64 = x[:, :n], x[:, n:]
    y = _bdot(x32, jnp.concatenate([inv16, x64], axis=1))
    yield
    inv32 = inv16 + y[:, :n]
    return inv32 + _bdot(x64 + y[:, n:], inv32)


def _rwkv_scan_kernel(r_ref, lw_ref, k_ref, v_ref, kkraw_ref, a_ref, g_ref, rk_ref, gg_ref, gb_ref, o_ref, s_ref):
    tb = r_ref.shape[0]
    n = RWKV_HEAD
    grp = 2 * CHUNK
    big = 2 * grp

    @pl.when(pl.program_id(1) == 0)
    def _():
        s_ref[...] = jnp.zeros_like(s_ref)

    rown = lax.broadcasted_iota(jnp.int32, (grp, LANES), 0)
    head0 = lax.broadcasted_iota(jnp.int32, (1, LANES), 1) < n
    chunk0 = rown < CHUNK
    row, col = _tri_masks(big)
    same = (row // CHUNK) == (col // CHUNK)
    incl = same & (row >= col)
    strict = same & (row > col)
    tri = incl[:grp, :grp].astype(BF16)
    tri3 = jnp.concatenate([tri, tri, tri], axis=1)
    srow = lax.broadcasted_iota(jnp.int32, (big, LANES), 0)
    slane = lax.broadcasted_iota(jnp.int32, (big, LANES), 1)
    srow_chunk0 = (srow // CHUNK) % 2 == 0
    diag = (srow % grp) == slane
    own = ((rown < n) == head0)

    def stack(t):
        return jnp.concatenate([jnp.where(head0, t, 0.0), jnp.where(head0, 0.0, t)], axis=0)

    def blockdiag(t):
        s = stack(t)
        return jnp.concatenate([jnp.where(srow_chunk0, s, 0.0), jnp.where(srow_chunk0, 0.0, s)], axis=1)

    def spread(t):
        return jnp.concatenate([jnp.where(srow < grp, t, 0.0), jnp.where(srow < grp, 0.0, t)], axis=1)

    def head_sum(t):
        return jnp.where(head0, jnp.sum(jnp.where(head0, t, 0.0), axis=-1, keepdims=True),
                         jnp.sum(jnp.where(head0, 0.0, t), axis=-1, keepdims=True))

    def per_chunk(t, c):
        return jnp.concatenate([t[c * CHUNK:(c + 1) * CHUNK], t[grp + c * CHUNK:grp + (c + 1) * CHUNK]], axis=0)

    def phase1(g):
        rows = slice(g * grp, (g + 1) * grp)
        r, lw, k, v = r_ref[rows, :], lw_ref[rows, :], k_ref[rows, :], v_ref[rows, :]
        kk, a = kkraw_ref[rows, :], a_ref[rows, :]
        kk = kk / jnp.maximum(jnp.sqrt(head_sum(kk * kk)), 1e-12)
        cum = _cumsum_rows(tri3, lw)
        yield
        cum_last = jnp.where(chunk0, cum[CHUNK - 1:CHUNK, :], cum[grp - 1:grp, :])
        e_neg, e_end = jnp.exp(-cum), jnp.exp(cum_last - cum)
        b_vec = kk * a
        a_t = stack(-kk * jnp.exp(cum - lw))
        r_t = stack(r * jnp.exp(cum))
        v_s = stack(v)
        a4 = _bdot_nt(jnp.concatenate([a_t, r_t], axis=0),
                      jnp.concatenate([b_vec * e_neg, k * e_neg], axis=0))
        yield
        a_ab = jnp.where(strict, spread(a4[:big, :grp]), 0.0)
        a_ak = jnp.where(strict, spread(a4[:big, grp:]), 0.0)
        a_rb = jnp.where(incl, spread(a4[big:, :grp]), 0.0)
        a_rk = jnp.where(incl, spread(a4[big:, grp:]), 0.0)
        akv = _bdot(a_ak, v_s)
        t_inv = yield from _unit_lower_inverse(a_ab, row, col)
        yield
        z = _bdot(t_inv, jnp.concatenate([a_t, akv], axis=1))
        yield
        zv = jnp.concatenate([z, jnp.concatenate([jnp.zeros_like(v_s), v_s], axis=1)], axis=0)
        gh = _bdot_tn(jnp.concatenate([blockdiag(b_vec * e_end), blockdiag(k * e_end)], axis=0), zv)
        ry = _bdot(jnp.concatenate([a_rb, a_rk], axis=1), zv)
        p_end = jnp.exp(jnp.concatenate([cum[CHUNK - 1:CHUNK, :], cum[grp - 1:grp, :]], axis=0))
        g_all = gh[:, :LANES] + jnp.where(diag, jnp.where(srow < grp, p_end[0:1, :], p_end[1:2, :]), 0.0)
        h_all = gh[:, LANES:]
        rp_all = r_t + ry[:, :LANES]
        y0_all = ry[:, LANES:]
        return g_all, h_all, rp_all, y0_all

    def emit(g, c, y):
        crow = slice(g * grp + c * CHUNK, g * grp + (c + 1) * CHUNK)
        mean = jnp.sum(y, axis=-1, keepdims=True) * (1.0 / n)
        dev = jnp.where(own, y - mean, 0.0)
        var = jnp.sum(dev * dev, axis=-1, keepdims=True) * (1.0 / n)
        yn = dev * lax.rsqrt(var + RWKV_GN_EPS)
        yn = yn[:CHUNK] + yn[CHUNK:]
        rc, kc, vc = r_ref[crow, :], k_ref[crow, :], v_ref[crow, :]
        bonus = head_sum(rc * kc * rk_ref[...])[:CHUNK] * vc
        o_ref[crow, :] = ((yn * gg_ref[...] + gb_ref[...] + bonus) * g_ref[crow, :]).astype(BF16)

    def phase2(g0, staged):
        for g, (g_all, h_all, rp_all, y0_all) in enumerate(staged, g0):
            for c in range(2):
                lhs = jnp.concatenate([g_all[c * grp:(c + 1) * grp], per_chunk(rp_all, c)], axis=0)
                both = _bdot(lhs, s_ref[...])
                yield
                s_ref[...] = both[:grp] + h_all[c * grp:(c + 1) * grp]
                emit(g, c, both[grp:] + per_chunk(y0_all, c))

    n_groups, wave = tb // grp, 4
    pending = None
    for g0 in range(0, n_groups, wave):
        gens = [phase1(g) for g in range(g0, min(g0 + wave, n_groups))]
        done = _lockstep(gens + ([pending] if pending is not None else []))
        pending = phase2(g0, done[:len(gens)])
    _lockstep([pending])


def _rwkv_scan(r, lw, k, v, kkraw, a, g, r_k, gn_g, gn_b, tb=2048):
    s, d = r.shape
    tb = min(tb, s)
    blk = pl.BlockSpec((tb, LANES), lambda p, i: (i, p))
    vec = pl.BlockSpec((1, LANES), lambda p, i: (0, p))
    return pl.pallas_call(
        _rwkv_scan_kernel,
        grid=(d // LANES, s // tb),
        in_specs=[blk] * 7 + [vec] * 3,
        out_specs=blk,
        out_shape=jax.ShapeDtypeStruct((s, d), BF16),
        scratch_shapes=[pltpu.VMEM((LANES, LANES), F32)],
        compiler_params=_params(("arbitrary", "arbitrary")),
        name="rwkv_scan",
    )(r, lw, k, v, kkraw, a, g, _row(r_k), _row(gn_g), _row(gn_b))


def _out_proj_kernel(x_ref, a_ref, w_ref, o_ref):
    o_ref[...] = x_ref[...] + _dot(a_ref[...], w_ref[...])


def _out_proj(x, a, w, tm=1024):
    s, d = x.shape
    blk = pl.BlockSpec((tm, d), lambda i: (i, 0))
    return pl.pallas_call(
        _out_proj_kernel,
        grid=(s // tm,),
        in_specs=[blk, blk, _full((d, d))],
        out_specs=blk,
        out_shape=jax.ShapeDtypeStruct((s, d), F32),
        compiler_params=_params(("arbitrary",)),
        name="out_proj",
    )(x, a, w.astype(BF16))


def _sb_qkv_kernel(x_ref, n1_ref, w_ref, q_ref, k_ref, v_ref):
    d = x_ref.shape[1]
    h = _rms(x_ref[...], n1_ref[...]).astype(BF16)
    qkv = _dot(h, w_ref[...])
    q_ref[...] = (qkv[:, :d] * SB_HEAD_DIM ** -0.5).astype(BF16)
    k_ref[...] = qkv[:, d:2 * d].astype(BF16)
    v_ref[...] = qkv[:, 2 * d:].astype(BF16)


def _sb_qkv(x, n1, w_qkv, tm=1024):
    s, d = x.shape
    blk = pl.BlockSpec((tm, d), lambda i: (i, 0))
    out = jax.ShapeDtypeStruct((s, d), BF16)
    return pl.pallas_call(
        _sb_qkv_kernel,
        grid=(s // tm,),
        in_specs=[blk, _full((1, d)), _full((d, 3 * d))],
        out_specs=[blk] * 3,
        out_shape=[out] * 3,
        compiler_params=_params(("arbitrary",)),
        name="sb_qkv",
    )(x, _row(n1), w_qkv.astype(BF16))


def _sb_attn_kernel(q_ref, k_ref, v_ref, o_ref, *, tq):
    qi = pl.program_id(1)
    head0 = lax.broadcasted_iota(jnp.int32, (1, LANES), 1) < SB_HEAD_DIM
    q = q_ref[...]
    zero = jnp.zeros_like(q)
    q2 = jnp.concatenate([jnp.where(head0, q, zero), jnp.where(head0, zero, q)], axis=0)
    row, col = _tri_masks(tq)
    before = jnp.concatenate([col < row, col < row], axis=0)
    later = (row > col).astype(BF16)
    later2 = jnp.concatenate([later, later], axis=0)

    def pair(kb, c, acc, diagonal):
        valid = kb >= 1
        ks = (pl.ds(pl.multiple_of(kb * tq, tq), tq), pl.ds(pl.multiple_of(jnp.maximum(kb - 1, 0) * tq, tq), tq))
        zs = [_dot_nt(q2, k_ref[s, :]) for s in ks]
        sps, excls = [], []
        for j, z in enumerate(zs):
            sp = jnp.maximum(z, 0.0) + jnp.log(1.0 + jnp.exp(-jnp.abs(z)))
            if diagonal and j == 0:
                sp = jnp.where(before, sp, 0.0)
            hi = sp.astype(BF16)
            sps.append(sp)
            excls.append(_dot(jnp.concatenate([hi, (sp - hi.astype(F32)).astype(BF16)], axis=1), later2))
        tot0 = excls[0][:, 0:1] + sps[0][:, 0:1]
        tot1 = jnp.where(valid, excls[1][:, 0:1] + sps[1][:, 0:1], 0.0)
        w0 = jnp.exp(zs[0] - sps[0] - excls[0] + c)
        if diagonal:
            w0 = jnp.where(before, w0, 0.0)
        w1 = jnp.where(valid, jnp.exp(zs[1] - sps[1] - excls[1] + (c - tot0)), 0.0)
        pv = _dot(jnp.concatenate([w0.astype(BF16), w1.astype(BF16)], axis=1),
                  jnp.concatenate([v_ref[ks[0], :], v_ref[ks[1], :]], axis=0))
        return c - tot0 - tot1, acc + pv

    c, acc = pair(qi, jnp.zeros((2 * tq, 1), F32), jnp.zeros((2 * tq, LANES), F32), True)

    def cond(st):
        kb, c, _ = st
        return jnp.logical_and(kb >= 0, jnp.max(c) > SB_ZERO_LOG)

    def body(st):
        kb, c, acc = st
        c, acc = pair(kb, c, acc, False)
        return kb - 2, c, acc

    _, _, acc = lax.while_loop(cond, body, (qi - 2, c, acc))
    o_ref[...] = jnp.where(head0, acc[:tq], acc[tq:]).astype(BF16)


def _sb_attn(q, k, v, tq=256):
    s, d = q.shape
    return pl.pallas_call(
        functools.partial(_sb_attn_kernel, tq=tq),
        grid=(d // LANES, s // tq),
        in_specs=[pl.BlockSpec((tq, LANES), lambda p, i: (i, p)),
                  pl.BlockSpec((s, LANES), lambda p, i: (0, p)),
                  pl.BlockSpec((s, LANES), lambda p, i: (0, p))],
        out_specs=pl.BlockSpec((tq, LANES), lambda p, i: (i, p)),
        out_shape=jax.ShapeDtypeStruct((s, d), BF16),
        compiler_params=_params(("arbitrary", "arbitrary")),
        name="sb_attn",
    )(q, k, v)


def _stick_breaking(x, n1, w_qkv, w_out):
    q, k, v = _sb_qkv(x, n1, w_qkv)
    return _out_proj(x, _sb_attn(q, k, v), w_out)


def _rwkv7(x, n1, mu, w_rkv, w0, w1, w2, a0, a1, a2, g1, g2, k_k, k_a, r_k, gn_g, gn_b, w_out):
    r, lw, k, v, kkraw, a, g = _rwkv_proj(x, n1, mu, w_rkv, w0, w1, w2, a0, a1, a2, g1, g2, k_k, k_a)
    return _out_proj(x, _rwkv_scan(r, lw, k, v, kkraw, a, g, r_k, gn_g, gn_b), w_out)


def kernel(x, l0_norm1, l0_gla_w_in, l0_gla_w_alpha_up, l0_gla_b_alpha, l0_gla_out_norm, l0_gla_w_out, l0_norm2, l0_ffn_w_up, l0_ffn_conv_w, l0_ffn_conv_b, l0_ffn_w_down, l1_norm1, l1_rwkv_mu, l1_rwkv_w_rkv, l1_rwkv_w0, l1_rwkv_w1, l1_rwkv_w2, l1_rwkv_a0, l1_rwkv_a1, l1_rwkv_a2, l1_rwkv_g1, l1_rwkv_g2, l1_rwkv_k_k, l1_rwkv_k_a, l1_rwkv_r_k, l1_rwkv_gn_g, l1_rwkv_gn_b, l1_rwkv_w_out, l1_norm2, l1_ffn_w_up, l1_ffn_conv_w, l1_ffn_conv_b, l1_ffn_w_down, l2_norm1, l2_sb_w_qkv, l2_sb_w_out, l2_norm2, l2_ffn_w_up, l2_ffn_conv_w, l2_ffn_conv_b, l2_ffn_w_down, l3_norm1, l3_ml_w_in, l3_ml_b_if, l3_ml_out_norm, l3_ml_w_out, l3_norm2, l3_ffn_w_up, l3_ffn_conv_w, l3_ffn_conv_b, l3_ffn_w_down, final_norm):
    b, s, d = x.shape
    outs = []
    for bi in range(b):
        h = x[bi]
        h = _gla(h, l0_norm1, l0_gla_w_in, l0_gla_w_alpha_up, l0_gla_b_alpha, l0_gla_out_norm, l0_gla_w_out)
        h = _ffn(h, l0_norm2, l0_ffn_w_up, l0_ffn_conv_w, l0_ffn_conv_b, l0_ffn_w_down)
        h = _rwkv7(h, l1_norm1, l1_rwkv_mu, l1_rwkv_w_rkv, l1_rwkv_w0, l1_rwkv_w1, l1_rwkv_w2,
                   l1_rwkv_a0, l1_rwkv_a1, l1_rwkv_a2, l1_rwkv_g1, l1_rwkv_g2,
                   l1_rwkv_k_k, l1_rwkv_k_a, l1_rwkv_r_k, l1_rwkv_gn_g, l1_rwkv_gn_b, l1_rwkv_w_out)
        h = _ffn(h, l1_norm2, l1_ffn_w_up, l1_ffn_conv_w, l1_ffn_conv_b, l1_ffn_w_down)
        h = _stick_breaking(h, l2_norm1, l2_sb_w_qkv, l2_sb_w_out)
        h = _ffn(h, l2_norm2, l2_ffn_w_up, l2_ffn_conv_w, l2_ffn_conv_b, l2_ffn_w_down)
        h = _mlstm(h, l3_norm1, l3_ml_w_in, l3_ml_b_if, l3_ml_out_norm, l3_ml_w_out)
        h = _ffn(h, l3_norm2, l3_ffn_w_up, l3_ffn_conv_w, l3_ffn_conv_b, l3_ffn_w_down, final_g=final_norm)
        outs.append(h)
    return jnp.stack(outs, axis=0)
```

```python
import functools

import jax
import jax.numpy as jnp
from jax import lax
from jax.experimental import pallas as pl
from jax.experimental.pallas import tpu as pltpu

F32 = jnp.float32
BF16 = jnp.bfloat16

NORM_EPS = 1e-6
CHUNK = 64
GLA_HEADS, GLA_DK, GLA_DV, GLA_TAU = 4, 128, 256, 16.0
ML_HEADS, ML_DK, ML_DV = 4, 128, 256
ML_CHUNK = 256
RWKV_HEAD = 64
RWKV_GN_EPS = 64e-5
SB_HEAD_DIM = 64
LANES = 128
V7X_VMEM_BYTES = 64 * 1024 * 1024
VMEM_LIMIT_BYTES = V7X_VMEM_BYTES * 7 // 8
SB_ZERO_LOG = -104.0


def _rms(x, g, eps=NORM_EPS):
    return x * lax.rsqrt(jnp.mean(x * x, axis=-1, keepdims=True) + eps) * g


def _softplus(x):
    return jnp.maximum(x, 0.0) + jnp.log(1.0 + jnp.exp(-jnp.abs(x)))


def _log_sigmoid(x):
    return -_softplus(-x)


def _sigmoid(x):
    return 1.0 / (1.0 + jnp.exp(-x))


def _dot(a, b, **kw):
    return jnp.dot(a, b, preferred_element_type=F32, **kw)


def _dot_nt(a, b, **kw):
    return lax.dot_general(a, b, (((1,), (1,)), ((), ())), preferred_element_type=F32, **kw)


def _dot_tn(a, b, **kw):
    return lax.dot_general(a, b, (((0,), (0,)), ((), ())), preferred_element_type=F32, **kw)


def _bdot(a, b):
    return _dot(a.astype(BF16), b.astype(BF16))


def _bdot_nt(a, b):
    return _dot_nt(a.astype(BF16), b.astype(BF16))


def _bdot_tn(a, b):
    return _dot_tn(a.astype(BF16), b.astype(BF16))


def _split3(x, axis):
    hi = x.astype(BF16)
    r1 = x - hi.astype(F32)
    mid = r1.astype(BF16)
    lo = (r1 - mid.astype(F32)).astype(BF16)
    return jnp.concatenate([hi, mid, lo], axis=axis)


def _cumsum_rows(tri3, x):
    return _dot(tri3, _split3(x, 0))


def _lockstep(gens):
    out = [None] * len(gens)
    live = list(range(len(gens)))
    while live:
        for i in list(live):
            try:
                next(gens[i])
            except StopIteration as stop:
                out[i] = stop.value
                live.remove(i)
    return out


def _tri_masks(n):
    row = lax.broadcasted_iota(jnp.int32, (n, n), 0)
    col = lax.broadcasted_iota(jnp.int32, (n, n), 1)
    return row, col


def _full(shape):
    return pl.BlockSpec(shape, lambda *_: (0,) * len(shape), pipeline_mode=pl.Buffered(1))


def _params(sem):
    return pltpu.CompilerParams(dimension_semantics=sem, vmem_limit_bytes=VMEM_LIMIT_BYTES)


def _row(v, width=None):
    v = v.reshape(1, -1).astype(F32)
    if width is not None and v.shape[1] < width:
        v = jnp.pad(v, ((0, 0), (0, width - v.shape[1])))
    return v


def _ffn_kernel(x_ref, g_ref, wup_ref, cw_ref, wd_ref, fg_ref, o_ref, h_ref, act_ref, car_ref, *, final_norm, fc):
    tm = x_ref.shape[0]
    f = wd_ref.shape[0]

    @pl.when(pl.program_id(0) == 0)
    def _():
        car_ref[...] = jnp.zeros_like(car_ref)

    x = x_ref[...]
    h_ref[...] = _rms(x, g_ref[...]).astype(BF16)
    rowi = lax.broadcasted_iota(jnp.int32, (tm, fc), 0)

    def conv_up(cols):
        u = _dot(h_ref[...], wup_ref[:, cols])
        c2, c1 = car_ref[6:7, cols], car_ref[7:8, cols]
        s1 = jnp.where(rowi == 0, c1, pltpu.roll(u, 1, 0))
        s2 = jnp.where(rowi == 0, c2, jnp.where(rowi == 1, c1, pltpu.roll(u, 2, 0)))
        car_ref[:, cols] = u[tm - 8:, :]
        return s2 * cw_ref[0:1, cols] + s1 * cw_ref[1:2, cols] + u * cw_ref[2:3, cols] + cw_ref[3:4, cols]

    for c in range(f // fc):
        gate = conv_up(slice(c * fc, (c + 1) * fc))
        up = conv_up(slice(f + c * fc, f + (c + 1) * fc))
        act_ref[:, c * fc:(c + 1) * fc] = (gate * _sigmoid(gate) * up).astype(BF16)
    y = x + _dot(act_ref[...], wd_ref[...])
    if final_norm:
        y = _rms(y, fg_ref[...])
    o_ref[...] = y


def _ffn(x, norm_g, w_up, conv_w, conv_b, w_down, final_g=None, tm=1024, fc=256):
    s, d = x.shape
    f = w_down.shape[0]
    taps = jnp.concatenate([conv_w, conv_b[None, :], jnp.zeros((4, 2 * f), F32)], axis=0)
    fg = _row(final_g if final_g is not None else jnp.ones((d,), F32))
    return pl.pallas_call(
        functools.partial(_ffn_kernel, final_norm=final_g is not None, fc=fc),
        grid=(s // tm,),
        in_specs=[pl.BlockSpec((tm, d), lambda i: (i, 0)), _full((1, d)), _full((d, 2 * f)), _full((8, 2 * f)),
                  _full((f, d)), _full((1, d))],
        out_specs=pl.BlockSpec((tm, d), lambda i: (i, 0)),
        out_shape=jax.ShapeDtypeStruct((s, d), F32),
        scratch_shapes=[pltpu.VMEM((tm, d), BF16), pltpu.VMEM((tm, f), BF16), pltpu.VMEM((8, 2 * f), F32)],
        compiler_params=_params(("arbitrary",)),
        name="conv_ffn",
    )(x, _row(norm_g), w_up.astype(BF16), taps, w_down.astype(BF16), fg)


def _gla_kernel(x_ref, n1_ref, wm_ref, wa_ref, wup_ref, ba_ref, on_ref, wo_ref, o_ref,
                proj_ref, la_ref, ob_ref, st_ref):
    tb = x_ref.shape[0]
    hk, hv = GLA_HEADS * GLA_DK, GLA_HEADS * GLA_DV

    @pl.when(pl.program_id(0) == 0)
    def _():
        st_ref[...] = jnp.zeros_like(st_ref)

    x = x_ref[...]
    h = _rms(x, n1_ref[...]).astype(BF16)
    proj_ref[...] = _dot(h, wm_ref[...])
    a_low = _dot(h, wa_ref[...])
    la_ref[...] = _log_sigmoid(_bdot(a_low, wup_ref[...]) + ba_ref[...]) * (1.0 / GLA_TAU)
    grp = 4 * CHUNK
    row, col = _tri_masks(grp)
    causal = ((row // CHUNK) == (col // CHUNK)) & (row >= col)
    tri = causal.astype(BF16)
    tri3 = jnp.concatenate([tri, tri, tri], axis=1)
    rowc = lax.broadcasted_iota(jnp.int32, (grp, hk), 0) // CHUNK
    head_of_lane = lax.broadcasted_iota(jnp.int32, (1, hk), 1) // GLA_DK
    scale = GLA_DK ** -0.5

    def stack(t):
        return jnp.concatenate([jnp.where(head_of_lane == hh, t, 0.0) for hh in range(GLA_HEADS)], axis=0)

    for g in range(tb // grp):
        r0 = g * grp
        rows = slice(r0, r0 + grp)
        b = _cumsum_rows(tri3, la_ref[rows, :])
        b_last = b[CHUNK - 1:CHUNK, :]
        for c in range(1, 4):
            b_last = jnp.where(rowc == c, b[(c + 1) * CHUNK - 1:(c + 1) * CHUNK, :], b_last)
        q_dec = proj_ref[rows, 0:hk] * scale * jnp.exp(b)
        k = proj_ref[rows, hk:2 * hk]
        k_inv = k * jnp.exp(-b)
        k_end = k * jnp.exp(b_last - b)
        scores = [jnp.where(causal, _bdot_nt(q_dec[:, hh * GLA_DK:(hh + 1) * GLA_DK],
                                             k_inv[:, hh * GLA_DK:(hh + 1) * GLA_DK]), 0.0)
                  for hh in range(GLA_HEADS)]
        upd = []
        for c in range(4):
            cr = slice(r0 + c * CHUNK, r0 + (c + 1) * CHUNK)
            v_s = jnp.concatenate([proj_ref[cr, 2 * hk + hh * GLA_DV:2 * hk + (hh + 1) * GLA_DV]
                                   for hh in range(GLA_HEADS)], axis=0)
            upd.append(_bdot_tn(v_s, stack(k_end[c * CHUNK:(c + 1) * CHUNK, :])))
        intra = [_bdot(scores[hh], proj_ref[rows, 2 * hk + hh * GLA_DV:2 * hk + (hh + 1) * GLA_DV])
                 for hh in range(GLA_HEADS)]
        st = st_ref[...]
        inter = []
        for c in range(4):
            inter.append(_bdot_nt(stack(q_dec[c * CHUNK:(c + 1) * CHUNK, :]), st))
            st = st * jnp.exp(b[(c + 1) * CHUNK - 1:(c + 1) * CHUNK, :]) + upd[c]
        st_ref[...] = st
        for hh in range(GLA_HEADS):
            ob_ref[rows, hh * GLA_DV:(hh + 1) * GLA_DV] = intra[hh] + jnp.concatenate(
                [inter[c][hh * CHUNK:(hh + 1) * CHUNK, :] for c in range(4)], axis=0)
    normed = [_rms(ob_ref[:, hh * GLA_DV:(hh + 1) * GLA_DV], on_ref[:, hh * GLA_DV:(hh + 1) * GLA_DV])
              for hh in range(GLA_HEADS)]
    r = proj_ref[:, 2 * hk + hv:2 * hk + 2 * hv]
    y = jnp.concatenate(normed, axis=-1) * (r * _sigmoid(r))
    o_ref[...] = x + _bdot(y, wo_ref[...])


def _gla(x, n1, w_in, w_alpha_up, b_alpha, out_norm, w_out, tb=1024):
    s, d = x.shape
    hk, hv = GLA_HEADS * GLA_DK, GLA_HEADS * GLA_DV
    nm = 2 * hk + 2 * hv
    rank = w_alpha_up.shape[0]
    wm = w_in[:, :nm].astype(BF16)
    wa = jnp.pad(w_in[:, nm:], ((0, 0), (0, LANES - rank))).astype(BF16)
    wup = jnp.pad(w_alpha_up, ((0, LANES - rank), (0, 0))).astype(BF16)
    return pl.pallas_call(
        _gla_kernel,
        grid=(s // tb,),
        in_specs=[pl.BlockSpec((tb, d), lambda i: (i, 0)), _full((1, d)), _full((d, nm)),
                  _full((d, LANES)), _full((LANES, hk)), _full((1, hk)), _full((1, hv)), _full((hv, d))],
        out_specs=pl.BlockSpec((tb, d), lambda i: (i, 0)),
        out_shape=jax.ShapeDtypeStruct((s, d), F32),
        scratch_shapes=[pltpu.VMEM((tb, nm), F32), pltpu.VMEM((tb, hk), F32), pltpu.VMEM((tb, hv), F32),
                        pltpu.VMEM((GLA_DV, hk), F32)],
        compiler_params=_params(("arbitrary",)),
        name="gla_mixer",
    )(x, _row(n1), wm, wa, wup, _row(b_alpha), _row(out_norm), w_out.astype(BF16))


def _mlstm_kernel(x_ref, n1_ref, wm_ref, wif_ref, wift_ref, bif_ref, bift_ref, on_ref, wo_ref, o_ref,
                  h_ref, proj_ref, gc_ref, ob_ref, c_ref, n_ref, m_ref):
    tb = x_ref.shape[0]
    nh = ML_HEADS
    hk, hv = nh * ML_DK, nh * ML_DV

    @pl.when(pl.program_id(0) == 0)
    def _():
        c_ref[...] = jnp.zeros_like(c_ref)
        n_ref[...] = jnp.zeros_like(n_ref)
        m_ref[...] = jnp.zeros_like(m_ref)

    x = x_ref[...]
    h = _rms(x, n1_ref[...]).astype(BF16)
    h_ref[...] = h
    proj_ref[...] = _dot(h, wm_ref[...])
    gc_ref[...] = _dot(h, wif_ref[...]) + bif_ref[...]
    row, col = _tri_masks(ML_CHUNK)
    causal = row >= col
    tri = causal.astype(BF16)
    tri3 = jnp.concatenate([tri, tri, tri], axis=1)
    scale = ML_DK ** -0.5

    for c in range(tb // ML_CHUNK):
        rows = slice(c * ML_CHUNK, (c + 1) * ML_CHUNK)
        g_col = gc_ref[rows, :]
        g_row = _dot_nt(wift_ref[...], h_ref[rows, :]) + bift_ref[...]
        b_col = _cumsum_rows(tri3, _log_sigmoid(g_col))
        b_row = _dot_nt(_split3(_log_sigmoid(g_row), 1), tri3)
        def head(hh):
            bc, ic = b_col[:, nh + hh:nh + hh + 1], g_col[:, hh:hh + 1]
            br, ir = b_row[nh + hh:nh + hh + 1, :], g_row[hh:hh + 1, :]
            q = proj_ref[rows, hh * ML_DK:(hh + 1) * ML_DK]
            k = proj_ref[rows, hk + hh * ML_DK:hk + (hh + 1) * ML_DK] * scale
            v = proj_ref[rows, 2 * hk + hh * ML_DV:2 * hk + (hh + 1) * ML_DV]
            m_prev = m_ref[0:1, hh:hh + 1]
            d_log = jnp.where(causal, bc - br + ir, -jnp.inf)
            inter = bc + m_prev
            m_t = jnp.maximum(inter, jnp.max(d_log, axis=-1, keepdims=True))
            w_intra = jnp.exp(d_log - m_t)
            sc = jnp.exp(inter - m_t)
            c_st, n_st = c_ref[hh], n_ref[hh]
            qk = _bdot_nt(q, k)
            qc = _bdot(q, c_st)
            yield
            qk = qk * w_intra
            b_last = bc[ML_CHUNK - 1:ML_CHUNK, :]
            g_end = b_last - bc + ic
            m_new = jnp.maximum(b_last + m_prev, jnp.max(g_end, axis=0, keepdims=True))
            kw = k * jnp.exp(g_end - m_new)
            carry_scale = jnp.exp(b_last + m_prev - m_new)
            num = _bdot(qk, v) + sc * qc
            c_ref[hh] = carry_scale * c_st + _bdot_tn(kw, v)
            yield
            den = jnp.sum(qk, axis=-1, keepdims=True) + sc * jnp.sum(q * n_st, axis=-1, keepdims=True)
            ob_ref[rows, hh * ML_DV:(hh + 1) * ML_DV] = num / jnp.maximum(jnp.abs(den), jnp.exp(-m_t))
            n_ref[hh] = carry_scale * n_st + jnp.sum(kw, axis=0, keepdims=True)
            m_ref[0:1, hh:hh + 1] = m_new

        _lockstep([head(hh) for hh in range(nh)])
    normed = [_rms(ob_ref[:, hh * ML_DV:(hh + 1) * ML_DV], on_ref[:, hh * ML_DV:(hh + 1) * ML_DV])
              for hh in range(nh)]
    o_pre = proj_ref[:, 2 * hk + hv:2 * hk + 2 * hv]
    y = jnp.concatenate(normed, axis=-1) * _sigmoid(o_pre)
    o_ref[...] = x + _bdot(y, wo_ref[...])


def _mlstm(x, n1, w_in, b_if, out_norm, w_out, tb=1024):
    s, d = x.shape
    nh = ML_HEADS
    hk, hv = nh * ML_DK, nh * ML_DV
    nm = 2 * hk + 2 * hv
    wm = w_in[:, :nm].astype(BF16)
    w_if = w_in[:, nm:]
    wif = jnp.pad(w_if, ((0, 0), (0, LANES - 2 * nh))).astype(BF16)
    wift = w_if.T.astype(BF16)
    return pl.pallas_call(
        _mlstm_kernel,
        grid=(s // tb,),
        in_specs=[pl.BlockSpec((tb, d), lambda i: (i, 0)), _full((1, d)), _full((d, nm)),
                  _full((d, LANES)), _full((2 * nh, d)), _full((1, LANES)), _full((2 * nh, 1)),
                  _full((1, hv)), _full((hv, d))],
        out_specs=pl.BlockSpec((tb, d), lambda i: (i, 0)),
        out_shape=jax.ShapeDtypeStruct((s, d), F32),
        scratch_shapes=[pltpu.VMEM((tb, d), BF16), pltpu.VMEM((tb, nm), F32), pltpu.VMEM((tb, LANES), F32),
                        pltpu.VMEM((tb, hv), F32), pltpu.VMEM((nh, ML_DK, ML_DV), F32),
                        pltpu.VMEM((nh, 1, ML_DK), F32), pltpu.VMEM((8, LANES), F32)],
        compiler_params=_params(("arbitrary",)),
        name="mlstm_mixer",
    )(x, _row(n1), wm, wif, wift, _row(b_if, LANES), b_if.reshape(2 * nh, 1).astype(F32),
      _row(out_norm), w_out.astype(BF16))


def _rwkv_proj_kernel(x_ref, n1_ref, mu_ref, wr_ref, wk_ref, wv_ref, w0_ref, w1_ref, w2_ref,
                      a0_ref, a1_ref, a2_ref, g1_ref, g2_ref, kk_ref, ka_ref,
                      r_ref, lw_ref, k_ref, v_ref, kkraw_ref, a_ref, g_ref, car_ref):
    tm = x_ref.shape[0]

    @pl.when(pl.program_id(0) == 0)
    def _():
        car_ref[...] = jnp.zeros_like(car_ref)

    h = _rms(x_ref[...], n1_ref[...])
    rowi = lax.broadcasted_iota(jnp.int32, h.shape, 0)
    h_prev = jnp.where(rowi == 0, car_ref[7:8, :], pltpu.roll(h, 1, 0))
    car_ref[...] = h[tm - 8:, :]
    xx = h_prev - h
    xr, xw, xk, xv, xa, xg = (h + xx * mu_ref[j:j + 1, :] for j in range(6))
    r = _bdot(xr, wr_ref[...])
    k = _bdot(xk, wk_ref[...])
    v = _bdot(xv, wv_ref[...])
    w_log = -_softplus(-(w0_ref[...] + _bdot(jnp.tanh(_bdot(xw, w1_ref[...])), w2_ref[...]))) - 0.5
    a = _sigmoid(a0_ref[...] + _bdot(_bdot(xa, a1_ref[...]), a2_ref[...]))
    g = _bdot(_sigmoid(_bdot(xg, g1_ref[...])), g2_ref[...])
    r_ref[...] = r
    lw_ref[...] = -jnp.exp(w_log)
    k_ref[...] = k * (1.0 + (a - 1.0) * ka_ref[...])
    v_ref[...] = v
    kkraw_ref[...] = k * kk_ref[...]
    a_ref[...] = a
    g_ref[...] = g


def _rwkv_proj(x, n1, mu, w_rkv, w0, w1, w2, a0, a1, a2, g1, g2, k_k, k_a, tm=512):
    s, d = x.shape

    def padc(w):
        return jnp.pad(w, ((0, 0), (0, LANES - w.shape[1]))).astype(BF16) if w.shape[1] < LANES else w.astype(BF16)

    def padr(w):
        return jnp.pad(w, ((0, LANES - w.shape[0]), (0, 0))).astype(BF16) if w.shape[0] < LANES else w.astype(BF16)

    mu8 = jnp.pad(mu, ((0, 8 - mu.shape[0]), (0, 0)))
    blk = pl.BlockSpec((tm, d), lambda i: (i, 0))
    out = jax.ShapeDtypeStruct((s, d), F32)
    return pl.pallas_call(
        _rwkv_proj_kernel,
        grid=(s // tm,),
        in_specs=[blk, _full((1, d)), _full((8, d)), _full((d, d)), _full((d, d)), _full((d, d)),
                  _full((1, d)), _full((d, LANES)), _full((LANES, d)),
                  _full((1, d)), _full((d, LANES)), _full((LANES, d)),
                  _full((d, LANES)), _full((LANES, d)), _full((1, d)), _full((1, d))],
        out_specs=[blk] * 7,
        out_shape=[out] * 7,
        scratch_shapes=[pltpu.VMEM((8, d), F32)],
        compiler_params=_params(("arbitrary",)),
        name="rwkv_proj",
    )(x, _row(n1), mu8, w_rkv[0].astype(BF16), w_rkv[1].astype(BF16), w_rkv[2].astype(BF16),
      _row(w0), padc(w1), padr(w2), _row(a0), padc(a1), padr(a2), padc(g1), padr(g2), _row(k_k), _row(k_a))


def _unit_lower_inverse(a_strict, row, col):
    n = a_strict.shape[0]
    eye = (row == col).astype(F32)
    blk16 = (row // 16) == (col // 16)
    p = jnp.where(blk16, a_strict, 0.0)
    inv = eye + p
    p = _bdot(p, p)
    yield
    for _ in range(2):
        both = _bdot(jnp.concatenate([p, inv], axis=0), p)
        yield
        p, inv = both[:n], inv + both[n:]
    inv16 = inv + _bdot(inv, p)
    yield
    blk32 = (row // 32) == (col // 32)
    off32 = jnp.where(blk32 & ~blk16, a_strict, 0.0)
    off64 = jnp.where(blk32, 0.0, a_strict)
    x = _bdot(inv16, jnp.concatenate([off32, off64], axis=1))
    yield
    x32, x64 = x[:, :n], x[:, n:]
    y = _bdot(x32, jnp.concatenate([inv16, x64], axis=1))
    yield
    inv32 = inv16 + y[:, :n]
    return inv32 + _bdot(x64 + y[:, n:], inv32)


def _rwkv_scan_kernel(r_ref, lw_ref, k_ref, v_ref, kkraw_ref, a_ref, g_ref, rk_ref, gg_ref, gb_ref, o_ref, s_ref):
    tb = r_ref.shape[0]
    n = RWKV_HEAD
    grp = 2 * CHUNK
    big = 2 * grp

    @pl.when(pl.program_id(1) == 0)
    def _():
        s_ref[...] = jnp.zeros_like(s_ref)

    rown = lax.broadcasted_iota(jnp.int32, (grp, LANES), 0)
    head0 = lax.broadcasted_iota(jnp.int32, (1, LANES), 1) < n
    chunk0 = rown < CHUNK
    row, col = _tri_masks(big)
    same = (row // CHUNK) == (col // CHUNK)
    incl = same & (row >= col)
    strict = same & (row > col)
    tri = incl[:grp, :grp].astype(BF16)
    tri3 = jnp.concatenate([tri, tri, tri], axis=1)
    srow = lax.broadcasted_iota(jnp.int32, (big, LANES), 0)
    slane = lax.broadcasted_iota(jnp.int32, (big, LANES), 1)
    srow_chunk0 = (srow // CHUNK) % 2 == 0
    diag = (srow % grp) == slane
    own = ((rown < n) == head0)

    def stack(t):
        return jnp.concatenate([jnp.where(head0, t, 0.0), jnp.where(head0, 0.0, t)], axis=0)

    def blockdiag(t):
        s = stack(t)
        return jnp.concatenate([jnp.where(srow_chunk0, s, 0.0), jnp.where(srow_chunk0, 0.0, s)], axis=1)

    def spread(t):
        return jnp.concatenate([jnp.where(srow < grp, t, 0.0), jnp.where(srow < grp, 0.0, t)], axis=1)

    def head_sum(t):
        return jnp.where(head0, jnp.sum(jnp.where(head0, t, 0.0), axis=-1, keepdims=True),
                         jnp.sum(jnp.where(head0, 0.0, t), axis=-1, keepdims=True))

    def per_chunk(t, c):
        return jnp.concatenate([t[c * CHUNK:(c + 1) * CHUNK], t[grp + c * CHUNK:grp + (c + 1) * CHUNK]], axis=0)

    def phase1(g):
        rows = slice(g * grp, (g + 1) * grp)
        r, lw, k, v = r_ref[rows, :], lw_ref[rows, :], k_ref[rows, :], v_ref[rows, :]
        kk, a = kkraw_ref[rows, :], a_ref[rows, :]
        kk = kk / jnp.maximum(jnp.sqrt(head_sum(kk * kk)), 1e-12)
        cum = _cumsum_rows(tri3, lw)
        yield
        cum_last = jnp.where(chunk0, cum[CHUNK - 1:CHUNK, :], cum[grp - 1:grp, :])
        e_neg, e_end = jnp.exp(-cum), jnp.exp(cum_last - cum)
        b_vec = kk * a
        a_t = stack(-kk * jnp.exp(cum - lw))
        r_t = stack(r * jnp.exp(cum))
        v_s = stack(v)
        a4 = _bdot_nt(jnp.concatenate([a_t, r_t], axis=0),
                      jnp.concatenate([b_vec * e_neg, k * e_neg], axis=0))
        yield
        a_ab = jnp.where(strict, spread(a4[:big, :grp]), 0.0)
        a_ak = jnp.where(strict, spread(a4[:big, grp:]), 0.0)
        a_rb = jnp.where(incl, spread(a4[big:, :grp]), 0.0)
        a_rk = jnp.where(incl, spread(a4[big:, grp:]), 0.0)
        akv = _bdot(a_ak, v_s)
        t_inv = yield from _unit_lower_inverse(a_ab, row, col)
        yield
        z = _bdot(t_inv, jnp.concatenate([a_t, akv], axis=1))
        yield
        zv = jnp.concatenate([z, jnp.concatenate([jnp.zeros_like(v_s), v_s], axis=1)], axis=0)
        gh = _bdot_tn(jnp.concatenate([blockdiag(b_vec * e_end), blockdiag(k * e_end)], axis=0), zv)
        ry = _bdot(jnp.concatenate([a_rb, a_rk], axis=1), zv)
        p_end = jnp.exp(jnp.concatenate([cum[CHUNK - 1:CHUNK, :], cum[grp - 1:grp, :]], axis=0))
        g_all = gh[:, :LANES] + jnp.where(diag, jnp.where(srow < grp, p_end[0:1, :], p_end[1:2, :]), 0.0)
        h_all = gh[:, LANES:]
        rp_all = r_t + ry[:, :LANES]
        y0_all = ry[:, LANES:]
        return g_all, h_all, rp_all, y0_all

    def emit(g, c, y):
        crow = slice(g * grp + c * CHUNK, g * grp + (c + 1) * CHUNK)
        mean = jnp.sum(y, axis=-1, keepdims=True) * (1.0 / n)
        dev = jnp.where(own, y - mean, 0.0)
        var = jnp.sum(dev * dev, axis=-1, keepdims=True) * (1.0 / n)
        yn = dev * lax.rsqrt(var + RWKV_GN_EPS)
        yn = yn[:CHUNK] + yn[CHUNK:]
        rc, kc, vc = r_ref[crow, :], k_ref[crow, :], v_ref[crow, :]
        bonus = head_sum(rc * kc * rk_ref[...])[:CHUNK] * vc
        o_ref[crow, :] = ((yn * gg_ref[...] + gb_ref[...] + bonus) * g_ref[crow, :]).astype(BF16)

    def phase2(g0, staged):
        for g, (g_all, h_all, rp_all, y0_all) in enumerate(staged, g0):
            for c in range(2):
                lhs = jnp.concatenate([g_all[c * grp:(c + 1) * grp], per_chunk(rp_all, c)], axis=0)
                both = _bdot(lhs, s_ref[...])
                yield
                s_ref[...] = both[:grp] + h_all[c * grp:(c + 1) * grp]
                emit(g, c, both[grp:] + per_chunk(y0_all, c))

    n_groups, wave = tb // grp, 4
    pending = None
    for g0 in range(0, n_groups, wave):
        gens = [phase1(g) for g in range(g0, min(g0 + wave, n_groups))]
        done = _lockstep(gens + ([pending] if pending is not None else []))
        pending = phase2(g0, done[:len(gens)])
    _lockstep([pending])


def _rwkv_scan(r, lw, k, v, kkraw, a, g, r_k, gn_g, gn_b, tb=2048):
    s, d = r.shape
    tb = min(tb, s)
    blk = pl.BlockSpec((tb, LANES), lambda p, i: (i, p))
    vec = pl.BlockSpec((1, LANES), lambda p, i: (0, p))
    return pl.pallas_call(
        _rwkv_scan_kernel,
        grid=(d // LANES, s // tb),
        in_specs=[blk] * 7 + [vec] * 3,
        out_specs=blk,
        out_shape=jax.ShapeDtypeStruct((s, d), BF16),
        scratch_shapes=[pltpu.VMEM((LANES, LANES), F32)],
        compiler_params=_params(("arbitrary", "arbitrary")),
        name="rwkv_scan",
    )(r, lw, k, v, kkraw, a, g, _row(r_k), _row(gn_g), _row(gn_b))


def _out_proj_kernel(x_ref, a_ref, w_ref, o_ref):
    o_ref[...] = x_ref[...] + _dot(a_ref[...], w_ref[...])


def _out_proj(x, a, w, tm=1024):
    s, d = x.shape
    blk = pl.BlockSpec((tm, d), lambda i: (i, 0))
    return pl.pallas_call(
        _out_proj_kernel,
        grid=(s // tm,),
        in_specs=[blk, blk, _full((d, d))],
        out_specs=blk,
        out_shape=jax.ShapeDtypeStruct((s, d), F32),
        compiler_params=_params(("arbitrary",)),
        name="out_proj",
    )(x, a, w.astype(BF16))


def _sb_qkv_kernel(x_ref, n1_ref, w_ref, q_ref, k_ref, v_ref):
    d = x_ref.shape[1]
    h = _rms(x_ref[...], n1_ref[...]).astype(BF16)
    qkv = _dot(h, w_ref[...])
    q_ref[...] = (qkv[:, :d] * SB_HEAD_DIM ** -0.5).astype(BF16)
    k_ref[...] = qkv[:, d:2 * d].astype(BF16)
    v_ref[...] = qkv[:, 2 * d:].astype(BF16)


def _sb_qkv(x, n1, w_qkv, tm=1024):
    s, d = x.shape
    blk = pl.BlockSpec((tm, d), lambda i: (i, 0))
    out = jax.ShapeDtypeStruct((s, d), BF16)
    return pl.pallas_call(
        _sb_qkv_kernel,
        grid=(s // tm,),
        in_specs=[blk, _full((1, d)), _full((d, 3 * d))],
        out_specs=[blk] * 3,
        out_shape=[out] * 3,
        compiler_params=_params(("arbitrary",)),
        name="sb_qkv",
    )(x, _row(n1), w_qkv.astype(BF16))


def _sb_attn_kernel(q_ref, k_ref, v_ref, o_ref, *, tq):
    qi = pl.program_id(1)
    head0 = lax.broadcasted_iota(jnp.int32, (1, LANES), 1) < SB_HEAD_DIM
    q = q_ref[...]
    zero = jnp.zeros_like(q)
    q2 = jnp.concatenate([jnp.where(head0, q, zero), jnp.where(head0, zero, q)], axis=0)
    row, col = _tri_masks(tq)
    before = jnp.concatenate([col < row, col < row], axis=0)
    later = (row > col).astype(BF16)
    later2 = jnp.concatenate([later, later], axis=0)

    def pair(kb, c, acc, diagonal):
        valid = kb >= 1
        ks = (pl.ds(pl.multiple_of(kb * tq, tq), tq), pl.ds(pl.multiple_of(jnp.maximum(kb - 1, 0) * tq, tq), tq))
        zs = [_dot_nt(q2, k_ref[s, :]) for s in ks]
        sps, excls = [], []
        for j, z in enumerate(zs):
            sp = jnp.maximum(z, 0.0) + jnp.log(1.0 + jnp.exp(-jnp.abs(z)))
            if diagonal and j == 0:
                sp = jnp.where(before, sp, 0.0)
            hi = sp.astype(BF16)
            sps.append(sp)
            excls.append(_dot(jnp.concatenate([hi, (sp - hi.astype(F32)).astype(BF16)], axis=1), later2))
        tot0 = excls[0][:, 0:1] + sps[0][:, 0:1]
        tot1 = jnp.where(valid, excls[1][:, 0:1] + sps[1][:, 0:1], 0.0)
        w0 = jnp.exp(zs[0] - sps[0] - excls[0] + c)
        if diagonal:
            w0 = jnp.where(before, w0, 0.0)
        w1 = jnp.where(valid, jnp.exp(zs[1] - sps[1] - excls[1] + (c - tot0)), 0.0)
        pv = _dot(jnp.concatenate([w0.astype(BF16), w1.astype(BF16)], axis=1),
                  jnp.concatenate([v_ref[ks[0], :], v_ref[ks[1], :]], axis=0))
        return c - tot0 - tot1, acc + pv

    c, acc = pair(qi, jnp.zeros((2 * tq, 1), F32), jnp.zeros((2 * tq, LANES), F32), True)

    def cond(st):
        kb, c, _ = st
        return jnp.logical_and(kb >= 0, jnp.max(c) > SB_ZERO_LOG)

    def body(st):
        kb, c, acc = st
        c, acc = pair(kb, c, acc, False)
        return kb - 2, c, acc

    _, _, acc = lax.while_loop(cond, body, (qi - 2, c, acc))
    o_ref[...] = jnp.where(head0, acc[:tq], acc[tq:]).astype(BF16)


def _sb_attn(q, k, v, tq=256):
    s, d = q.shape
    return pl.pallas_call(
        functools.partial(_sb_attn_kernel, tq=tq),
        grid=(d // LANES, s // tq),
        in_specs=[pl.BlockSpec((tq, LANES), lambda p, i: (i, p)),
                  pl.BlockSpec((s, LANES), lambda p, i: (0, p)),
                  pl.BlockSpec((s, LANES), lambda p, i: (0, p))],
        out_specs=pl.BlockSpec((tq, LANES), lambda p, i: (i, p)),
        out_shape=jax.ShapeDtypeStruct((s, d), BF16),
        compiler_params=_params(("arbitrary", "arbitrary")),
        name="sb_attn",
    )(q, k, v)


def _stick_breaking(x, n1, w_qkv, w_out):
    q, k, v = _sb_qkv(x, n1, w_qkv)
    return _out_proj(x, _sb_attn(q, k, v), w_out)


def _rwkv7(x, n1, mu, w_rkv, w0, w1, w2, a0, a1, a2, g1, g2, k_k, k_a, r_k, gn_g, gn_b, w_out):
    r, lw, k, v, kkraw, a, g = _rwkv_proj(x, n1, mu, w_rkv, w0, w1, w2, a0, a1, a2, g1, g2, k_k, k_a)
    return _out_proj(x, _rwkv_scan(r, lw, k, v, kkraw, a, g, r_k, gn_g, gn_b), w_out)


def kernel(x, l0_norm1, l0_gla_w_in, l0_gla_w_alpha_up, l0_gla_b_alpha, l0_gla_out_norm, l0_gla_w_out, l0_norm2, l0_ffn_w_up, l0_ffn_conv_w, l0_ffn_conv_b, l0_ffn_w_down, l1_norm1, l1_rwkv_mu, l1_rwkv_w_rkv, l1_rwkv_w0, l1_rwkv_w1, l1_rwkv_w2, l1_rwkv_a0, l1_rwkv_a1, l1_rwkv_a2, l1_rwkv_g1, l1_rwkv_g2, l1_rwkv_k_k, l1_rwkv_k_a, l1_rwkv_r_k, l1_rwkv_gn_g, l1_rwkv_gn_b, l1_rwkv_w_out, l1_norm2, l1_ffn_w_up, l1_ffn_conv_w, l1_ffn_conv_b, l1_ffn_w_down, l2_norm1, l2_sb_w_qkv, l2_sb_w_out, l2_norm2, l2_ffn_w_up, l2_ffn_conv_w, l2_ffn_conv_b, l2_ffn_w_down, l3_norm1, l3_ml_w_in, l3_ml_b_if, l3_ml_out_norm, l3_ml_w_out, l3_norm2, l3_ffn_w_up, l3_ffn_conv_w, l3_ffn_conv_b, l3_ffn_w_down, final_norm):
    b, s, d = x.shape
    outs = []
    for bi in range(b):
        h = x[bi]
        h = _gla(h, l0_norm1, l0_gla_w_in, l0_gla_w_alpha_up, l0_gla_b_alpha, l0_gla_out_norm, l0_gla_w_out)
        h = _ffn(h, l0_norm2, l0_ffn_w_up, l0_ffn_conv_w, l0_ffn_conv_b, l0_ffn_w_down)
        h = _rwkv7(h, l1_norm1, l1_rwkv_mu, l1_rwkv_w_rkv, l1_rwkv_w0, l1_rwkv_w1, l1_rwkv_w2,
                   l1_rwkv_a0, l1_rwkv_a1, l1_rwkv_a2, l1_rwkv_g1, l1_rwkv_g2,
                   l1_rwkv_k_k, l1_rwkv_k_a, l1_rwkv_r_k, l1_rwkv_gn_g, l1_rwkv_gn_b, l1_rwkv_w_out)
        h = _ffn(h, l1_norm2, l1_ffn_w_up, l1_ffn_conv_w, l1_ffn_conv_b, l1_ffn_w_down)
        h = _stick_breaking(h, l2_norm1, l2_sb_w_qkv, l2_sb_w_out)
        h = _ffn(h, l2_norm2, l2_ffn_w_up, l2_ffn_conv_w, l2_ffn_conv_b, l2_ffn_w_down)
        h = _mlstm(h, l3_norm1, l3_ml_w_in, l3_ml_b_if, l3_ml_out_norm, l3_ml_w_out)
        h = _ffn(h, l3_norm2, l3_ffn_w_up, l3_ffn_conv_w, l3_ffn_conv_b, l3_ffn_w_down, final_g=final_norm)
        outs.append(h)
    return jnp.stack(outs, axis=0)
```

```python
import functools

import jax
import jax.numpy as jnp
from jax import lax
from jax.experimental import pallas as pl
from jax.experimental.pallas import tpu as pltpu

F32 = jnp.float32
BF16 = jnp.bfloat16

NORM_EPS = 1e-6
CHUNK = 64
GLA_HEADS, GLA_DK, GLA_DV, GLA_TAU = 4, 128, 256, 16.0
ML_HEADS, ML_DK, ML_DV = 4, 128, 256
ML_CHUNK = 256
RWKV_HEAD = 64
RWKV_GN_EPS = 64e-5
SB_HEAD_DIM = 64
LANES = 128
V7X_VMEM_BYTES = 64 * 1024 * 1024
VMEM_LIMIT_BYTES = V7X_VMEM_BYTES * 7 // 8
SB_ZERO_LOG = -104.0


def _rms(x, g, eps=NORM_EPS):
    return x * lax.rsqrt(jnp.mean(x * x, axis=-1, keepdims=True) + eps) * g


def _softplus(x):
    return jnp.maximum(x, 0.0) + jnp.log(1.0 + jnp.exp(-jnp.abs(x)))


def _log_sigmoid(x):
    return -_softplus(-x)


def _sigmoid(x):
    return 1.0 / (1.0 + jnp.exp(-x))


def _dot(a, b, **kw):
    return jnp.dot(a, b, preferred_element_type=F32, **kw)


def _dot_nt(a, b, **kw):
    return lax.dot_general(a, b, (((1,), (1,)), ((), ())), preferred_element_type=F32, **kw)


def _dot_tn(a, b, **kw):
    return lax.dot_general(a, b, (((0,), (0,)), ((), ())), preferred_element_type=F32, **kw)


def _bdot(a, b):
    return _dot(a.astype(BF16), b.astype(BF16))


def _bdot_nt(a, b):
    return _dot_nt(a.astype(BF16), b.astype(BF16))


def _bdot_tn(a, b):
    return _dot_tn(a.astype(BF16), b.astype(BF16))


def _split3(x, axis):
    hi = x.astype(BF16)
    r1 = x - hi.astype(F32)
    mid = r1.astype(BF16)
    lo = (r1 - mid.astype(F32)).astype(BF16)
    return jnp.concatenate([hi, mid, lo], axis=axis)


def _cumsum_rows(tri3, x):
    return _dot(tri3, _split3(x, 0))


def _lockstep(gens):
    out = [None] * len(gens)
    live = list(range(len(gens)))
    while live:
        for i in list(live):
            try:
                next(gens[i])
            except StopIteration as stop:
                out[i] = stop.value
                live.remove(i)
    return out


def _tri_masks(n):
    row = lax.broadcasted_iota(jnp.int32, (n, n), 0)
    col = lax.broadcasted_iota(jnp.int32, (n, n), 1)
    return row, col


def _full(shape):
    return pl.BlockSpec(shape, lambda *_: (0,) * len(shape), pipeline_mode=pl.Buffered(1))


def _params(sem):
    return pltpu.CompilerParams(dimension_semantics=sem, vmem_limit_bytes=VMEM_LIMIT_BYTES)


def _row(v, width=None):
    v = v.reshape(1, -1).astype(F32)
    if width is not None and v.shape[1] < width:
        v = jnp.pad(v, ((0, 0), (0, width - v.shape[1])))
    return v


def _ffn_kernel(*refs, final_norm, fc, mixer):
    if mixer:
        a_ref, wo_ref, *refs = refs
    x_ref, g_ref, wup_ref, cw_ref, wd_ref, fg_ref, o_ref, h_ref, act_ref, car_ref = refs
    tm = x_ref.shape[0]
    f = wd_ref.shape[0]

    @pl.when(pl.program_id(0) == 0)
    def _():
        car_ref[...] = jnp.zeros_like(car_ref)

    x = x_ref[...]
    if mixer:
        x = x + _dot(a_ref[...], wo_ref[...])
    h_ref[...] = _rms(x, g_ref[...]).astype(BF16)
    rowi = lax.broadcasted_iota(jnp.int32, (tm, fc), 0)

    def conv_up(cols):
        u = _dot(h_ref[...], wup_ref[:, cols])
        c2, c1 = car_ref[6:7, cols], car_ref[7:8, cols]
        s1 = jnp.where(rowi == 0, c1, pltpu.roll(u, 1, 0))
        s2 = jnp.where(rowi == 0, c2, jnp.where(rowi == 1, c1, pltpu.roll(u, 2, 0)))
        car_ref[:, cols] = u[tm - 8:, :]
        return s2 * cw_ref[0:1, cols] + s1 * cw_ref[1:2, cols] + u * cw_ref[2:3, cols] + cw_ref[3:4, cols]

    for c in range(f // fc):
        gate = conv_up(slice(c * fc, (c + 1) * fc))
        up = conv_up(slice(f + c * fc, f + (c + 1) * fc))
        act_ref[:, c * fc:(c + 1) * fc] = (gate * _sigmoid(gate) * up).astype(BF16)
    y = x + _dot(act_ref[...], wd_ref[...])
    if final_norm:
        y = _rms(y, fg_ref[...])
    o_ref[...] = y


def _ffn(x, norm_g, w_up, conv_w, conv_b, w_down, final_g=None, mixer=None, tm=1024, fc=256):
    s, d = x.shape
    f = w_down.shape[0]
    taps = jnp.concatenate([conv_w, conv_b[None, :], jnp.zeros((4, 2 * f), F32)], axis=0)
    fg = _row(final_g if final_g is not None else jnp.ones((d,), F32))
    blk = pl.BlockSpec((tm, d), lambda i: (i, 0))
    pre_specs, pre_args = ([blk, _full((d, d))], (mixer[0], mixer[1].astype(BF16))) if mixer else ([], ())
    return pl.pallas_call(
        functools.partial(_ffn_kernel, final_norm=final_g is not None, fc=fc, mixer=mixer is not None),
        grid=(s // tm,),
        in_specs=pre_specs + [blk, _full((1, d)), _full((d, 2 * f)), _full((8, 2 * f)), _full((f, d)), _full((1, d))],
        out_specs=blk,
        out_shape=jax.ShapeDtypeStruct((s, d), F32),
        scratch_shapes=[pltpu.VMEM((tm, d), BF16), pltpu.VMEM((tm, f), BF16), pltpu.VMEM((8, 2 * f), F32)],
        compiler_params=_params(("arbitrary",)),
        name="conv_ffn",
    )(*pre_args, x, _row(norm_g), w_up.astype(BF16), taps, w_down.astype(BF16), fg)


def _gla_kernel(x_ref, n1_ref, wm_ref, wa_ref, wup_ref, ba_ref, on_ref, wo_ref, o_ref,
                proj_ref, la_ref, ob_ref, st_ref):
    tb = x_ref.shape[0]
    hk, hv = GLA_HEADS * GLA_DK, GLA_HEADS * GLA_DV

    @pl.when(pl.program_id(0) == 0)
    def _():
        st_ref[...] = jnp.zeros_like(st_ref)

    x = x_ref[...]
    h = _rms(x, n1_ref[...]).astype(BF16)
    proj_ref[...] = _dot(h, wm_ref[...])
    a_low = _dot(h, wa_ref[...])
    la_ref[...] = _log_sigmoid(_bdot(a_low, wup_ref[...]) + ba_ref[...]) * (1.0 / GLA_TAU)
    grp = 4 * CHUNK
    row, col = _tri_masks(grp)
    causal = ((row // CHUNK) == (col // CHUNK)) & (row >= col)
    tri = causal.astype(BF16)
    tri3 = jnp.concatenate([tri, tri, tri], axis=1)
    rowc = lax.broadcasted_iota(jnp.int32, (grp, hk), 0) // CHUNK
    head_of_lane = lax.broadcasted_iota(jnp.int32, (1, hk), 1) // GLA_DK
    scale = GLA_DK ** -0.5

    def stack(t):
        return jnp.concatenate([jnp.where(head_of_lane == hh, t, 0.0) for hh in range(GLA_HEADS)], axis=0)

    for g in range(tb // grp):
        r0 = g * grp
        rows = slice(r0, r0 + grp)
        b = _cumsum_rows(tri3, la_ref[rows, :])
        b_last = b[CHUNK - 1:CHUNK, :]
        for c in range(1, 4):
            b_last = jnp.where(rowc == c, b[(c + 1) * CHUNK - 1:(c + 1) * CHUNK, :], b_last)
        q_dec = proj_ref[rows, 0:hk] * scale * jnp.exp(b)
        k = proj_ref[rows, hk:2 * hk]
        k_inv = k * jnp.exp(-b)
        k_end = k * jnp.exp(b_last - b)
        scores = [jnp.where(causal, _bdot_nt(q_dec[:, hh * GLA_DK:(hh + 1) * GLA_DK],
                                             k_inv[:, hh * GLA_DK:(hh + 1) * GLA_DK]), 0.0)
                  for hh in range(GLA_HEADS)]
        upd = []
        for c in range(4):
            cr = slice(r0 + c * CHUNK, r0 + (c + 1) * CHUNK)
            v_s = jnp.concatenate([proj_ref[cr, 2 * hk + hh * GLA_DV:2 * hk + (hh + 1) * GLA_DV]
                                   for hh in range(GLA_HEADS)], axis=0)
            upd.append(_bdot_tn(v_s, stack(k_end[c * CHUNK:(c + 1) * CHUNK, :])))
        intra = [_bdot(scores[hh], proj_ref[rows, 2 * hk + hh * GLA_DV:2 * hk + (hh + 1) * GLA_DV])
                 for hh in range(GLA_HEADS)]
        st = st_ref[...]
        inter = []
        for c in range(4):
            inter.append(_bdot_nt(stack(q_dec[c * CHUNK:(c + 1) * CHUNK, :]), st))
            st = st * jnp.exp(b[(c + 1) * CHUNK - 1:(c + 1) * CHUNK, :]) + upd[c]
        st_ref[...] = st
        for hh in range(GLA_HEADS):
            ob_ref[rows, hh * GLA_DV:(hh + 1) * GLA_DV] = intra[hh] + jnp.concatenate(
                [inter[c][hh * CHUNK:(hh + 1) * CHUNK, :] for c in range(4)], axis=0)
    normed = [_rms(ob_ref[:, hh * GLA_DV:(hh + 1) * GLA_DV], on_ref[:, hh * GLA_DV:(hh + 1) * GLA_DV])
              for hh in range(GLA_HEADS)]
    r = proj_ref[:, 2 * hk + hv:2 * hk + 2 * hv]
    y = jnp.concatenate(normed, axis=-1) * (r * _sigmoid(r))
    o_ref[...] = x + _bdot(y, wo_ref[...])


def _gla(x, n1, w_in, w_alpha_up, b_alpha, out_norm, w_out, tb=1024):
    s, d = x.shape
    hk, hv = GLA_HEADS * GLA_DK, GLA_HEADS * GLA_DV
    nm = 2 * hk + 2 * hv
    rank = w_alpha_up.shape[0]
    wm = w_in[:, :nm].astype(BF16)
    wa = jnp.pad(w_in[:, nm:], ((0, 0), (0, LANES - rank))).astype(BF16)
    wup = jnp.pad(w_alpha_up, ((0, LANES - rank), (0, 0))).astype(BF16)
    return pl.pallas_call(
        _gla_kernel,
        grid=(s // tb,),
        in_specs=[pl.BlockSpec((tb, d), lambda i: (i, 0)), _full((1, d)), _full((d, nm)),
                  _full((d, LANES)), _full((LANES, hk)), _full((1, hk)), _full((1, hv)), _full((hv, d))],
        out_specs=pl.BlockSpec((tb, d), lambda i: (i, 0)),
        out_shape=jax.ShapeDtypeStruct((s, d), F32),
        scratch_shapes=[pltpu.VMEM((tb, nm), F32), pltpu.VMEM((tb, hk), F32), pltpu.VMEM((tb, hv), F32),
                        pltpu.VMEM((GLA_DV, hk), F32)],
        compiler_params=_params(("arbitrary",)),
        name="gla_mixer",
    )(x, _row(n1), wm, wa, wup, _row(b_alpha), _row(out_norm), w_out.astype(BF16))


def _mlstm_kernel(x_ref, n1_ref, wm_ref, wif_ref, wift_ref, bif_ref, bift_ref, on_ref, wo_ref, o_ref,
                  h_ref, proj_ref, gc_ref, ob_ref, c_ref, n_ref, m_ref):
    tb = x_ref.shape[0]
    nh = ML_HEADS
    hk, hv = nh * ML_DK, nh * ML_DV

    @pl.when(pl.program_id(0) == 0)
    def _():
        c_ref[...] = jnp.zeros_like(c_ref)
        n_ref[...] = jnp.zeros_like(n_ref)
        m_ref[...] = jnp.zeros_like(m_ref)

    x = x_ref[...]
    h = _rms(x, n1_ref[...]).astype(BF16)
    h_ref[...] = h
    proj_ref[...] = _dot(h, wm_ref[...])
    gc_ref[...] = _dot(h, wif_ref[...]) + bif_ref[...]
    row, col = _tri_masks(ML_CHUNK)
    causal = row >= col
    tri = causal.astype(BF16)
    tri3 = jnp.concatenate([tri, tri, tri], axis=1)
    scale = ML_DK ** -0.5

    for c in range(tb // ML_CHUNK):
        rows = slice(c * ML_CHUNK, (c + 1) * ML_CHUNK)
        g_col = gc_ref[rows, :]
        g_row = _dot_nt(wift_ref[...], h_ref[rows, :]) + bift_ref[...]
        b_col = _cumsum_rows(tri3, _log_sigmoid(g_col))
        b_row = _dot_nt(_split3(_log_sigmoid(g_row), 1), tri3)
        def head(hh):
            bc, ic = b_col[:, nh + hh:nh + hh + 1], g_col[:, hh:hh + 1]
            br, ir = b_row[nh + hh:nh + hh + 1, :], g_row[hh:hh + 1, :]
            q = proj_ref[rows, hh * ML_DK:(hh + 1) * ML_DK]
            k = proj_ref[rows, hk + hh * ML_DK:hk + (hh + 1) * ML_DK] * scale
            v = proj_ref[rows, 2 * hk + hh * ML_DV:2 * hk + (hh + 1) * ML_DV]
            m_prev = m_ref[0:1, hh:hh + 1]
            d_log = jnp.where(causal, bc - br + ir, -jnp.inf)
            inter = bc + m_prev
            m_t = jnp.maximum(inter, jnp.max(d_log, axis=-1, keepdims=True))
            w_intra = jnp.exp(d_log - m_t)
            sc = jnp.exp(inter - m_t)
            c_st, n_st = c_ref[hh], n_ref[hh]
            qk = _bdot_nt(q, k)
            qc = _bdot(q, c_st)
            yield
            qk = qk * w_intra
            b_last = bc[ML_CHUNK - 1:ML_CHUNK, :]
            g_end = b_last - bc + ic
            m_new = jnp.maximum(b_last + m_prev, jnp.max(g_end, axis=0, keepdims=True))
            kw = k * jnp.exp(g_end - m_new)
            carry_scale = jnp.exp(b_last + m_prev - m_new)
            num = _bdot(qk, v) + sc * qc
            c_ref[hh] = carry_scale * c_st + _bdot_tn(kw, v)
            yield
            den = jnp.sum(qk, axis=-1, keepdims=True) + sc * jnp.sum(q * n_st, axis=-1, keepdims=True)
            ob_ref[rows, hh * ML_DV:(hh + 1) * ML_DV] = num / jnp.maximum(jnp.abs(den), jnp.exp(-m_t))
            n_ref[hh] = carry_scale * n_st + jnp.sum(kw, axis=0, keepdims=True)
            m_ref[0:1, hh:hh + 1] = m_new

        _lockstep([head(hh) for hh in range(nh)])
    normed = [_rms(ob_ref[:, hh * ML_DV:(hh + 1) * ML_DV], on_ref[:, hh * ML_DV:(hh + 1) * ML_DV])
              for hh in range(nh)]
    o_pre = proj_ref[:, 2 * hk + hv:2 * hk + 2 * hv]
    y = jnp.concatenate(normed, axis=-1) * _sigmoid(o_pre)
    o_ref[...] = x + _bdot(y, wo_ref[...])


def _mlstm(x, n1, w_in, b_if, out_norm, w_out, tb=1024):
    s, d = x.shape
    nh = ML_HEADS
    hk, hv = nh * ML_DK, nh * ML_DV
    nm = 2 * hk + 2 * hv
    wm = w_in[:, :nm].astype(BF16)
    w_if = w_in[:, nm:]
    wif = jnp.pad(w_if, ((0, 0), (0, LANES - 2 * nh))).astype(BF16)
    wift = w_if.T.astype(BF16)
    return pl.pallas_call(
        _mlstm_kernel,
        grid=(s // tb,),
        in_specs=[pl.BlockSpec((tb, d), lambda i: (i, 0)), _full((1, d)), _full((d, nm)),
                  _full((d, LANES)), _full((2 * nh, d)), _full((1, LANES)), _full((2 * nh, 1)),
                  _full((1, hv)), _full((hv, d))],
        out_specs=pl.BlockSpec((tb, d), lambda i: (i, 0)),
        out_shape=jax.ShapeDtypeStruct((s, d), F32),
        scratch_shapes=[pltpu.VMEM((tb, d), BF16), pltpu.VMEM((tb, nm), F32), pltpu.VMEM((tb, LANES), F32),
                        pltpu.VMEM((tb, hv), F32), pltpu.VMEM((nh, ML_DK, ML_DV), F32),
                        pltpu.VMEM((nh, 1, ML_DK), F32), pltpu.VMEM((8, LANES), F32)],
        compiler_params=_params(("arbitrary",)),
        name="mlstm_mixer",
    )(x, _row(n1), wm, wif, wift, _row(b_if, LANES), b_if.reshape(2 * nh, 1).astype(F32),
      _row(out_norm), w_out.astype(BF16))


def _rwkv_proj_kernel(x_ref, n1_ref, mu_ref, wr_ref, wk_ref, wv_ref, w0_ref, w1_ref, w2_ref,
                      a0_ref, a1_ref, a2_ref, g1_ref, g2_ref, kk_ref, ka_ref,
                      r_ref, lw_ref, k_ref, v_ref, kkraw_ref, a_ref, g_ref, car_ref):
    tm = x_ref.shape[0]

    @pl.when(pl.program_id(0) == 0)
    def _():
        car_ref[...] = jnp.zeros_like(car_ref)

    h = _rms(x_ref[...], n1_ref[...])
    rowi = lax.broadcasted_iota(jnp.int32, h.shape, 0)
    h_prev = jnp.where(rowi == 0, car_ref[7:8, :], pltpu.roll(h, 1, 0))
    car_ref[...] = h[tm - 8:, :]
    xx = h_prev - h
    xr, xw, xk, xv, xa, xg = (h + xx * mu_ref[j:j + 1, :] for j in range(6))
    r = _bdot(xr, wr_ref[...])
    k = _bdot(xk, wk_ref[...])
    v = _bdot(xv, wv_ref[...])
    w_log = -_softplus(-(w0_ref[...] + _bdot(jnp.tanh(_bdot(xw, w1_ref[...])), w2_ref[...]))) - 0.5
    a = _sigmoid(a0_ref[...] + _bdot(_bdot(xa, a1_ref[...]), a2_ref[...]))
    g = _bdot(_sigmoid(_bdot(xg, g1_ref[...])), g2_ref[...])
    r_ref[...] = r
    lw_ref[...] = -jnp.exp(w_log)
    k_ref[...] = k * (1.0 + (a - 1.0) * ka_ref[...])
    v_ref[...] = v
    kkraw_ref[...] = k * kk_ref[...]
    a_ref[...] = a
    g_ref[...] = g


def _rwkv_proj(x, n1, mu, w_rkv, w0, w1, w2, a0, a1, a2, g1, g2, k_k, k_a, tm=512):
    s, d = x.shape

    def padc(w):
        return jnp.pad(w, ((0, 0), (0, LANES - w.shape[1]))).astype(BF16) if w.shape[1] < LANES else w.astype(BF16)

    def padr(w):
        return jnp.pad(w, ((0, LANES - w.shape[0]), (0, 0))).astype(BF16) if w.shape[0] < LANES else w.astype(BF16)

    mu8 = jnp.pad(mu, ((0, 8 - mu.shape[0]), (0, 0)))
    blk = pl.BlockSpec((tm, d), lambda i: (i, 0))
    out = jax.ShapeDtypeStruct((s, d), F32)
    return pl.pallas_call(
        _rwkv_proj_kernel,
        grid=(s // tm,),
        in_specs=[blk, _full((1, d)), _full((8, d)), _full((d, d)), _full((d, d)), _full((d, d)),
                  _full((1, d)), _full((d, LANES)), _full((LANES, d)),
                  _full((1, d)), _full((d, LANES)), _full((LANES, d)),
                  _full((d, LANES)), _full((LANES, d)), _full((1, d)), _full((1, d))],
        out_specs=[blk] * 7,
        out_shape=[out] * 7,
        scratch_shapes=[pltpu.VMEM((8, d), F32)],
        compiler_params=_params(("arbitrary",)),
        name="rwkv_proj",
    )(x, _row(n1), mu8, w_rkv[0].astype(BF16), w_rkv[1].astype(BF16), w_rkv[2].astype(BF16),
      _row(w0), padc(w1), padr(w2), _row(a0), padc(a1), padr(a2), padc(g1), padr(g2), _row(k_k), _row(k_a))


def _unit_lower_inverse(a_strict, row, col):
    n = a_strict.shape[0]
    eye = (row == col).astype(F32)
    blk16 = (row // 16) == (col // 16)
    p = jnp.where(blk16, a_strict, 0.0)
    inv = eye + p
    p = _bdot(p, p)
    yield
    for _ in range(2):
        both = _bdot(jnp.concatenate([p, inv], axis=0), p)
        yield
        p, inv = both[:n], inv + both[n:]
    inv16 = inv + _bdot(inv, p)
    yield
    blk32 = (row // 32) == (col // 32)
    off32 = jnp.where(blk32 & ~blk16, a_strict, 0.0)
    off64 = jnp.where(blk32, 0.0, a_strict)
    x = _bdot(inv16, jnp.concatenate([off32, off64], axis=1))
    yield
    x32, x64 = x[:, :n], x[:, n:]
    y = _bdot(x32, jnp.concatenate([inv16, x64], axis=1))
    yield
    inv32 = inv16 + y[:, :n]
    return inv32 + _bdot(x64 + y[:, n:], inv32)


def _rwkv_scan_kernel(r_ref, lw_ref, k_ref, v_ref, kkraw_ref, a_ref, g_ref, rk_ref, gg_ref, gb_ref, o_ref, s_ref):
    tb = r_ref.shape[0]
    n = RWKV_HEAD
    grp = 2 * CHUNK
    big = 2 * grp

    @pl.when(pl.program_id(1) == 0)
    def _():
        s_ref[...] = jnp.zeros_like(s_ref)

    rown = lax.broadcasted_iota(jnp.int32, (grp, LANES), 0)
    head0 = lax.broadcasted_iota(jnp.int32, (1, LANES), 1) < n
    chunk0 = rown < CHUNK
    row, col = _tri_masks(big)
    same = (row // CHUNK) == (col // CHUNK)
    incl = same & (row >= col)
    strict = same & (row > col)
    tri = incl[:grp, :grp].astype(BF16)
    tri3 = jnp.concatenate([tri, tri, tri], axis=1)
    srow = lax.broadcasted_iota(jnp.int32, (big, LANES), 0)
    slane = lax.broadcasted_iota(jnp.int32, (big, LANES), 1)
    srow_chunk0 = (srow // CHUNK) % 2 == 0
    diag = (srow % grp) == slane
    own = ((rown < n) == head0)

    def stack(t):
        return jnp.concatenate([jnp.where(head0, t, 0.0), jnp.where(head0, 0.0, t)], axis=0)

    def blockdiag(t):
        s = stack(t)
        return jnp.concatenate([jnp.where(srow_chunk0, s, 0.0), jnp.where(srow_chunk0, 0.0, s)], axis=1)

    def spread(t):
        return jnp.concatenate([jnp.where(srow < grp, t, 0.0), jnp.where(srow < grp, 0.0, t)], axis=1)

    def head_sum(t):
        return jnp.where(head0, jnp.sum(jnp.where(head0, t, 0.0), axis=-1, keepdims=True),
                         jnp.sum(jnp.where(head0, 0.0, t), axis=-1, keepdims=True))

    def per_chunk(t, c):
        return jnp.concatenate([t[c * CHUNK:(c + 1) * CHUNK], t[grp + c * CHUNK:grp + (c + 1) * CHUNK]], axis=0)

    def phase1(g):
        rows = slice(g * grp, (g + 1) * grp)
        r, lw, k, v = r_ref[rows, :], lw_ref[rows, :], k_ref[rows, :], v_ref[rows, :]
        kk, a = kkraw_ref[rows, :], a_ref[rows, :]
        kk = kk / jnp.maximum(jnp.sqrt(head_sum(kk * kk)), 1e-12)
        cum = _cumsum_rows(tri3, lw)
        yield
        cum_last = jnp.where(chunk0, cum[CHUNK - 1:CHUNK, :], cum[grp - 1:grp, :])
        e_neg, e_end = jnp.exp(-cum), jnp.exp(cum_last - cum)
        b_vec = kk * a
        a_t = stack(-kk * jnp.exp(cum - lw))
        r_t = stack(r * jnp.exp(cum))
        v_s = stack(v)
        a4 = _bdot_nt(jnp.concatenate([a_t, r_t], axis=0),
                      jnp.concatenate([b_vec * e_neg, k * e_neg], axis=0))
        yield
        a_ab = jnp.where(strict, spread(a4[:big, :grp]), 0.0)
        a_ak = jnp.where(strict, spread(a4[:big, grp:]), 0.0)
        a_rb = jnp.where(incl, spread(a4[big:, :grp]), 0.0)
        a_rk = jnp.where(incl, spread(a4[big:, grp:]), 0.0)
        akv = _bdot(a_ak, v_s)
        t_inv = yield from _unit_lower_inverse(a_ab, row, col)
        yield
        z = _bdot(t_inv, jnp.concatenate([a_t, akv], axis=1))
        yield
        zv = jnp.concatenate([z, jnp.concatenate([jnp.zeros_like(v_s), v_s], axis=1)], axis=0)
        gh = _bdot_tn(jnp.concatenate([blockdiag(b_vec * e_end), blockdiag(k * e_end)], axis=0), zv)
        ry = _bdot(jnp.concatenate([a_rb, a_rk], axis=1), zv)
        p_end = jnp.exp(jnp.concatenate([cum[CHUNK - 1:CHUNK, :], cum[grp - 1:grp, :]], axis=0))
        g_all = gh[:, :LANES] + jnp.where(diag, jnp.where(srow < grp, p_end[0:1, :], p_end[1:2, :]), 0.0)
        h_all = gh[:, LANES:]
        rp_all = r_t + ry[:, :LANES]
        y0_all = ry[:, LANES:]
        return g_all, h_all, rp_all, y0_all

    def emit(g, c, y):
        crow = slice(g * grp + c * CHUNK, g * grp + (c + 1) * CHUNK)
        mean = jnp.sum(y, axis=-1, keepdims=True) * (1.0 / n)
        dev = jnp.where(own, y - mean, 0.0)
        var = jnp.sum(dev * dev, axis=-1, keepdims=True) * (1.0 / n)
        yn = dev * lax.rsqrt(var + RWKV_GN_EPS)
        yn = yn[:CHUNK] + yn[CHUNK:]
        rc, kc, vc = r_ref[crow, :], k_ref[crow, :], v_ref[crow, :]
        bonus = head_sum(rc * kc * rk_ref[...])[:CHUNK] * vc
        o_ref[crow, :] = ((yn * gg_ref[...] + gb_ref[...] + bonus) * g_ref[crow, :]).astype(BF16)

    def phase2(g0, staged):
        for g, (g_all, h_all, rp_all, y0_all) in enumerate(staged, g0):
            for c in range(2):
                lhs = jnp.concatenate([g_all[c * grp:(c + 1) * grp], per_chunk(rp_all, c)], axis=0)
                both = _bdot(lhs, s_ref[...])
                yield
                s_ref[...] = both[:grp] + h_all[c * grp:(c + 1) * grp]
                emit(g, c, both[grp:] + per_chunk(y0_all, c))

    n_groups, wave = tb // grp, 4
    pending = None
    for g0 in range(0, n_groups, wave):
        gens = [phase1(g) for g in range(g0, min(g0 + wave, n_groups))]
        done = _lockstep(gens + ([pending] if pending is not None else []))
        pending = phase2(g0, done[:len(gens)])
    _lockstep([pending])


def _rwkv_scan(r, lw, k, v, kkraw, a, g, r_k, gn_g, gn_b, tb=2048):
    s, d = r.shape
    tb = min(tb, s)
    blk = pl.BlockSpec((tb, LANES), lambda p, i: (i, p))
    vec = pl.BlockSpec((1, LANES), lambda p, i: (0, p))
    return pl.pallas_call(
        _rwkv_scan_kernel,
        grid=(d // LANES, s // tb),
        in_specs=[blk] * 7 + [vec] * 3,
        out_specs=blk,
        out_shape=jax.ShapeDtypeStruct((s, d), BF16),
        scratch_shapes=[pltpu.VMEM((LANES, LANES), F32)],
        compiler_params=_params(("arbitrary", "arbitrary")),
        name="rwkv_scan",
    )(r, lw, k, v, kkraw, a, g, _row(r_k), _row(gn_g), _row(gn_b))


def _sb_qkv_kernel(x_ref, n1_ref, w_ref, q_ref, k_ref, v_ref):
    d = x_ref.shape[1]
    h = _rms(x_ref[...], n1_ref[...]).astype(BF16)
    qkv = _dot(h, w_ref[...])
    q_ref[...] = (qkv[:, :d] * SB_HEAD_DIM ** -0.5).astype(BF16)
    k_ref[...] = qkv[:, d:2 * d].astype(BF16)
    v_ref[...] = qkv[:, 2 * d:].astype(BF16)


def _sb_qkv(x, n1, w_qkv, tm=1024):
    s, d = x.shape
    blk = pl.BlockSpec((tm, d), lambda i: (i, 0))
    out = jax.ShapeDtypeStruct((s, d), BF16)
    return pl.pallas_call(
        _sb_qkv_kernel,
        grid=(s // tm,),
        in_specs=[blk, _full((1, d)), _full((d, 3 * d))],
        out_specs=[blk] * 3,
        out_shape=[out] * 3,
        compiler_params=_params(("arbitrary",)),
        name="sb_qkv",
    )(x, _row(n1), w_qkv.astype(BF16))


def _sb_attn_kernel(q_ref, k_ref, v_ref, o_ref, *, tq):
    qi = pl.program_id(1)
    head0 = lax.broadcasted_iota(jnp.int32, (1, LANES), 1) < SB_HEAD_DIM
    q = q_ref[...]
    zero = jnp.zeros_like(q)
    q2 = jnp.concatenate([jnp.where(head0, q, zero), jnp.where(head0, zero, q)], axis=0)
    row, col = _tri_masks(tq)
    before = jnp.concatenate([col < row, col < row], axis=0)
    later = (row > col).astype(BF16)
    later2 = jnp.concatenate([later, later], axis=0)

    def pair(kb, c, acc, diagonal):
        valid = kb >= 1
        ks = (pl.ds(pl.multiple_of(kb * tq, tq), tq), pl.ds(pl.multiple_of(jnp.maximum(kb - 1, 0) * tq, tq), tq))
        zs = [_dot_nt(q2, k_ref[s, :]) for s in ks]
        sps, excls = [], []
        for j, z in enumerate(zs):
            sp = jnp.maximum(z, 0.0) + jnp.log(1.0 + jnp.exp(-jnp.abs(z)))
            if diagonal and j == 0:
                sp = jnp.where(before, sp, 0.0)
            hi = sp.astype(BF16)
            sps.append(sp)
            excls.append(_dot(jnp.concatenate([hi, (sp - hi.astype(F32)).astype(BF16)], axis=1), later2))
        tot0 = excls[0][:, 0:1] + sps[0][:, 0:1]
        tot1 = jnp.where(valid, excls[1][:, 0:1] + sps[1][:, 0:1], 0.0)
        w0 = jnp.exp(zs[0] - sps[0] - excls[0] + c)
        if diagonal:
            w0 = jnp.where(before, w0, 0.0)
        w1 = jnp.where(valid, jnp.exp(zs[1] - sps[1] - excls[1] + (c - tot0)), 0.0)
        pv = _dot(jnp.concatenate([w0.astype(BF16), w1.astype(BF16)], axis=1),
                  jnp.concatenate([v_ref[ks[0], :], v_ref[ks[1], :]], axis=0))
        return c - tot0 - tot1, acc + pv

    c, acc = pair(qi, jnp.zeros((2 * tq, 1), F32), jnp.zeros((2 * tq, LANES), F32), True)

    def cond(st):
        kb, c, _ = st
        return jnp.logical_and(kb >= 0, jnp.max(c) > SB_ZERO_LOG)

    def body(st):
        kb, c, acc = st
        c, acc = pair(kb, c, acc, False)
        return kb - 2, c, acc

    _, _, acc = lax.while_loop(cond, body, (qi - 2, c, acc))
    o_ref[...] = jnp.where(head0, acc[:tq], acc[tq:]).astype(BF16)


def _sb_attn(q, k, v, tq=256):
    s, d = q.shape
    return pl.pallas_call(
        functools.partial(_sb_attn_kernel, tq=tq),
        grid=(d // LANES, s // tq),
        in_specs=[pl.BlockSpec((tq, LANES), lambda p, i: (i, p)),
                  pl.BlockSpec((s, LANES), lambda p, i: (0, p)),
                  pl.BlockSpec((s, LANES), lambda p, i: (0, p))],
        out_specs=pl.BlockSpec((tq, LANES), lambda p, i: (i, p)),
        out_shape=jax.ShapeDtypeStruct((s, d), BF16),
        compiler_params=_params(("arbitrary", "arbitrary")),
        name="sb_attn",
    )(q, k, v)


def _stick_breaking(x, n1, w_qkv):
    q, k, v = _sb_qkv(x, n1, w_qkv)
    return _sb_attn(q, k, v)


def _rwkv7(x, n1, mu, w_rkv, w0, w1, w2, a0, a1, a2, g1, g2, k_k, k_a, r_k, gn_g, gn_b):
    r, lw, k, v, kkraw, a, g = _rwkv_proj(x, n1, mu, w_rkv, w0, w1, w2, a0, a1, a2, g1, g2, k_k, k_a)
    return _rwkv_scan(r, lw, k, v, kkraw, a, g, r_k, gn_g, gn_b)


def kernel(x, l0_norm1, l0_gla_w_in, l0_gla_w_alpha_up, l0_gla_b_alpha, l0_gla_out_norm, l0_gla_w_out, l0_norm2, l0_ffn_w_up, l0_ffn_conv_w, l0_ffn_conv_b, l0_ffn_w_down, l1_norm1, l1_rwkv_mu, l1_rwkv_w_rkv, l1_rwkv_w0, l1_rwkv_w1, l1_rwkv_w2, l1_rwkv_a0, l1_rwkv_a1, l1_rwkv_a2, l1_rwkv_g1, l1_rwkv_g2, l1_rwkv_k_k, l1_rwkv_k_a, l1_rwkv_r_k, l1_rwkv_gn_g, l1_rwkv_gn_b, l1_rwkv_w_out, l1_norm2, l1_ffn_w_up, l1_ffn_conv_w, l1_ffn_conv_b, l1_ffn_w_down, l2_norm1, l2_sb_w_qkv, l2_sb_w_out, l2_norm2, l2_ffn_w_up, l2_ffn_conv_w, l2_ffn_conv_b, l2_ffn_w_down, l3_norm1, l3_ml_w_in, l3_ml_b_if, l3_ml_out_norm, l3_ml_w_out, l3_norm2, l3_ffn_w_up, l3_ffn_conv_w, l3_ffn_conv_b, l3_ffn_w_down, final_norm):
    b, s, d = x.shape
    outs = []
    for bi in range(b):
        h = x[bi]
        h = _gla(h, l0_norm1, l0_gla_w_in, l0_gla_w_alpha_up, l0_gla_b_alpha, l0_gla_out_norm, l0_gla_w_out)
        h = _ffn(h, l0_norm2, l0_ffn_w_up, l0_ffn_conv_w, l0_ffn_conv_b, l0_ffn_w_down)
        a = _rwkv7(h, l1_norm1, l1_rwkv_mu, l1_rwkv_w_rkv, l1_rwkv_w0, l1_rwkv_w1, l1_rwkv_w2,
                   l1_rwkv_a0, l1_rwkv_a1, l1_rwkv_a2, l1_rwkv_g1, l1_rwkv_g2,
                   l1_rwkv_k_k, l1_rwkv_k_a, l1_rwkv_r_k, l1_rwkv_gn_g, l1_rwkv_gn_b)
        h = _ffn(h, l1_norm2, l1_ffn_w_up, l1_ffn_conv_w, l1_ffn_conv_b, l1_ffn_w_down, mixer=(a, l1_rwkv_w_out))
        a = _stick_breaking(h, l2_norm1, l2_sb_w_qkv)
        h = _ffn(h, l2_norm2, l2_ffn_w_up, l2_ffn_conv_w, l2_ffn_conv_b, l2_ffn_w_down, mixer=(a, l2_sb_w_out))
        h = _mlstm(h, l3_norm1, l3_ml_w_in, l3_ml_b_if, l3_ml_out_norm, l3_ml_w_out)
        h = _ffn(h, l3_norm2, l3_ffn_w_up, l3_ffn_conv_w, l3_ffn_conv_b, l3_ffn_w_down, final_g=final_norm)
        outs.append(h)
    return jnp.stack(outs, axis=0)
```

```python
import functools

import jax
import jax.numpy as jnp
from jax import lax
from jax.experimental import pallas as pl
from jax.experimental.pallas import tpu as pltpu

F32 = jnp.float32
BF16 = jnp.bfloat16

NORM_EPS = 1e-6
LOG2E = 1.4426950408889634
CHUNK = 64
GLA_HEADS, GLA_DK, GLA_DV, GLA_TAU = 4, 128, 256, 16.0
ML_HEADS, ML_DK, ML_DV = 4, 128, 256
ML_CHUNK = 256
RWKV_HEAD = 64
RWKV_GN_EPS = 64e-5
SB_HEAD_DIM = 64
LANES = 128
V7X_VMEM_BYTES = 64 * 1024 * 1024
VMEM_LIMIT_BYTES = V7X_VMEM_BYTES * 7 // 8
SB_ZERO_LOG = -104.0


def _rms(x, g, eps=NORM_EPS):
    return x * lax.rsqrt(jnp.mean(x * x, axis=-1, keepdims=True) + eps) * g


def _softplus(x):
    return jnp.maximum(x, 0.0) + jnp.log(1.0 + jnp.exp2(jnp.abs(x) * -LOG2E))


def _log_sigmoid(x):
    return -_softplus(-x)


def _sigmoid(x):
    return 1.0 / (1.0 + jnp.exp2(x * -LOG2E))


def _dot(a, b, **kw):
    return jnp.dot(a, b, preferred_element_type=F32, **kw)


def _dot_nt(a, b, **kw):
    return lax.dot_general(a, b, (((1,), (1,)), ((), ())), preferred_element_type=F32, **kw)


def _dot_tn(a, b, **kw):
    return lax.dot_general(a, b, (((0,), (0,)), ((), ())), preferred_element_type=F32, **kw)


def _bdot(a, b):
    return _dot(a.astype(BF16), b.astype(BF16))


def _bdot_nt(a, b):
    return _dot_nt(a.astype(BF16), b.astype(BF16))


def _bdot_tn(a, b):
    return _dot_tn(a.astype(BF16), b.astype(BF16))


def _split3(x, axis):
    hi = x.astype(BF16)
    r1 = x - hi.astype(F32)
    mid = r1.astype(BF16)
    lo = (r1 - mid.astype(F32)).astype(BF16)
    return jnp.concatenate([hi, mid, lo], axis=axis)


def _cumsum_rows(tri3, x):
    return _dot(tri3, _split3(x, 0))


def _lockstep(gens):
    out = [None] * len(gens)
    live = list(range(len(gens)))
    while live:
        for i in list(live):
            try:
                next(gens[i])
            except StopIteration as stop:
                out[i] = stop.value
                live.remove(i)
    return out


def _tri_masks(n):
    row = lax.broadcasted_iota(jnp.int32, (n, n), 0)
    col = lax.broadcasted_iota(jnp.int32, (n, n), 1)
    return row, col


def _full(shape):
    return pl.BlockSpec(shape, lambda *_: (0,) * len(shape), pipeline_mode=pl.Buffered(1))


def _params(sem):
    return pltpu.CompilerParams(dimension_semantics=sem, vmem_limit_bytes=VMEM_LIMIT_BYTES)


def _row(v, width=None):
    v = v.reshape(1, -1).astype(F32)
    if width is not None and v.shape[1] < width:
        v = jnp.pad(v, ((0, 0), (0, width - v.shape[1])))
    return v


def _ffn_kernel(*refs, final_norm, fc, mixer):
    if mixer:
        a_ref, wo_ref, *refs = refs
    x_ref, g_ref, wup_ref, cw_ref, wd_ref, fg_ref, o_ref, h_ref, act_ref, car_ref = refs
    tm = x_ref.shape[0]
    f = wd_ref.shape[0]

    @pl.when(pl.program_id(0) == 0)
    def _():
        car_ref[...] = jnp.zeros_like(car_ref)

    x = x_ref[...]
    if mixer:
        x = x + _dot(a_ref[...], wo_ref[...])
    h_ref[...] = _rms(x, g_ref[...]).astype(BF16)
    rowi = lax.broadcasted_iota(jnp.int32, (tm, fc), 0)

    def conv_up(cols):
        u = _dot(h_ref[...], wup_ref[:, cols])
        c2, c1 = car_ref[6:7, cols], car_ref[7:8, cols]
        s1 = jnp.where(rowi == 0, c1, pltpu.roll(u, 1, 0))
        s2 = jnp.where(rowi == 0, c2, jnp.where(rowi == 1, c1, pltpu.roll(u, 2, 0)))
        car_ref[:, cols] = u[tm - 8:, :]
        return s2 * cw_ref[0:1, cols] + s1 * cw_ref[1:2, cols] + u * cw_ref[2:3, cols] + cw_ref[3:4, cols]

    for c in range(f // fc):
        gate = conv_up(slice(c * fc, (c + 1) * fc))
        up = conv_up(slice(f + c * fc, f + (c + 1) * fc))
        act_ref[:, c * fc:(c + 1) * fc] = (gate * _sigmoid(gate) * up).astype(BF16)
    y = x + _dot(act_ref[...], wd_ref[...])
    if final_norm:
        y = _rms(y, fg_ref[...])
    o_ref[...] = y


def _ffn(x, norm_g, w_up, conv_w, conv_b, w_down, final_g=None, mixer=None, tm=1024, fc=256):
    s, d = x.shape
    f = w_down.shape[0]
    taps = jnp.concatenate([conv_w, conv_b[None, :], jnp.zeros((4, 2 * f), F32)], axis=0)
    fg = _row(final_g if final_g is not None else jnp.ones((d,), F32))
    blk = pl.BlockSpec((tm, d), lambda i: (i, 0))
    pre_specs, pre_args = ([blk, _full((d, d))], (mixer[0], mixer[1].astype(BF16))) if mixer else ([], ())
    return pl.pallas_call(
        functools.partial(_ffn_kernel, final_norm=final_g is not None, fc=fc, mixer=mixer is not None),
        grid=(s // tm,),
        in_specs=pre_specs + [blk, _full((1, d)), _full((d, 2 * f)), _full((8, 2 * f)), _full((f, d)), _full((1, d))],
        out_specs=blk,
        out_shape=jax.ShapeDtypeStruct((s, d), F32),
        scratch_shapes=[pltpu.VMEM((tm, d), BF16), pltpu.VMEM((tm, f), BF16), pltpu.VMEM((8, 2 * f), F32)],
        compiler_params=_params(("arbitrary",)),
        name="conv_ffn",
    )(*pre_args, x, _row(norm_g), w_up.astype(BF16), taps, w_down.astype(BF16), fg)


def _gla_kernel(x_ref, n1_ref, wm_ref, wa_ref, wup_ref, ba_ref, on_ref, wo_ref, o_ref,
                proj_ref, la_ref, ob_ref, st_ref):
    tb = x_ref.shape[0]
    hk, hv = GLA_HEADS * GLA_DK, GLA_HEADS * GLA_DV

    @pl.when(pl.program_id(0) == 0)
    def _():
        st_ref[...] = jnp.zeros_like(st_ref)

    x = x_ref[...]
    h = _rms(x, n1_ref[...]).astype(BF16)
    proj_ref[...] = _dot(h, wm_ref[...])
    a_low = _dot(h, wa_ref[...])
    la_ref[...] = _log_sigmoid(_bdot(a_low, wup_ref[...]) + ba_ref[...]) * (1.0 / GLA_TAU)
    grp = 4 * CHUNK
    row, col = _tri_masks(grp)
    causal = ((row // CHUNK) == (col // CHUNK)) & (row >= col)
    tri = causal.astype(BF16)
    tri3 = jnp.concatenate([tri, tri, tri], axis=1)
    rowc = lax.broadcasted_iota(jnp.int32, (grp, hk), 0) // CHUNK
    head_of_lane = lax.broadcasted_iota(jnp.int32, (1, hk), 1) // GLA_DK
    scale = GLA_DK ** -0.5

    def stack(t):
        return jnp.concatenate([jnp.where(head_of_lane == hh, t, 0.0) for hh in range(GLA_HEADS)], axis=0)

    for g in range(tb // grp):
        r0 = g * grp
        rows = slice(r0, r0 + grp)
        b = _cumsum_rows(tri3, la_ref[rows, :])
        b_last = b[CHUNK - 1:CHUNK, :]
        for c in range(1, 4):
            b_last = jnp.where(rowc == c, b[(c + 1) * CHUNK - 1:(c + 1) * CHUNK, :], b_last)
        q_dec = proj_ref[rows, 0:hk] * scale * jnp.exp(b)
        k = proj_ref[rows, hk:2 * hk]
        k_inv = k * jnp.exp(-b)
        k_end = k * jnp.exp(b_last - b)
        scores = [jnp.where(causal, _bdot_nt(q_dec[:, hh * GLA_DK:(hh + 1) * GLA_DK],
                                             k_inv[:, hh * GLA_DK:(hh + 1) * GLA_DK]), 0.0)
                  for hh in range(GLA_HEADS)]
        upd = []
        for c in range(4):
            cr = slice(r0 + c * CHUNK, r0 + (c + 1) * CHUNK)
            v_s = jnp.concatenate([proj_ref[cr, 2 * hk + hh * GLA_DV:2 * hk + (hh + 1) * GLA_DV]
                                   for hh in range(GLA_HEADS)], axis=0)
            upd.append(_bdot_tn(v_s, stack(k_end[c * CHUNK:(c + 1) * CHUNK, :])))
        intra = [_bdot(scores[hh], proj_ref[rows, 2 * hk + hh * GLA_DV:2 * hk + (hh + 1) * GLA_DV])
                 for hh in range(GLA_HEADS)]
        st = st_ref[...]
        inter = []
        for c in range(4):
            inter.append(_bdot_nt(stack(q_dec[c * CHUNK:(c + 1) * CHUNK, :]), st))
            st = st * jnp.exp(b[(c + 1) * CHUNK - 1:(c + 1) * CHUNK, :]) + upd[c]
        st_ref[...] = st
        for hh in range(GLA_HEADS):
            ob_ref[rows, hh * GLA_DV:(hh + 1) * GLA_DV] = intra[hh] + jnp.concatenate(
                [inter[c][hh * CHUNK:(hh + 1) * CHUNK, :] for c in range(4)], axis=0)
    normed = [_rms(ob_ref[:, hh * GLA_DV:(hh + 1) * GLA_DV], on_ref[:, hh * GLA_DV:(hh + 1) * GLA_DV])
              for hh in range(GLA_HEADS)]
    r = proj_ref[:, 2 * hk + hv:2 * hk + 2 * hv]
    y = jnp.concatenate(normed, axis=-1) * (r * _sigmoid(r))
    o_ref[...] = x + _bdot(y, wo_ref[...])


def _gla(x, n1, w_in, w_alpha_up, b_alpha, out_norm, w_out, tb=1024):
    s, d = x.shape
    hk, hv = GLA_HEADS * GLA_DK, GLA_HEADS * GLA_DV
    nm = 2 * hk + 2 * hv
    rank = w_alpha_up.shape[0]
    wm = w_in[:, :nm].astype(BF16)
    wa = jnp.pad(w_in[:, nm:], ((0, 0), (0, LANES - rank))).astype(BF16)
    wup = jnp.pad(w_alpha_up, ((0, LANES - rank), (0, 0))).astype(BF16)
    return pl.pallas_call(
        _gla_kernel,
        grid=(s // tb,),
        in_specs=[pl.BlockSpec((tb, d), lambda i: (i, 0)), _full((1, d)), _full((d, nm)),
                  _full((d, LANES)), _full((LANES, hk)), _full((1, hk)), _full((1, hv)), _full((hv, d))],
        out_specs=pl.BlockSpec((tb, d), lambda i: (i, 0)),
        out_shape=jax.ShapeDtypeStruct((s, d), F32),
        scratch_shapes=[pltpu.VMEM((tb, nm), F32), pltpu.VMEM((tb, hk), F32), pltpu.VMEM((tb, hv), F32),
                        pltpu.VMEM((GLA_DV, hk), F32)],
        compiler_params=_params(("arbitrary",)),
        name="gla_mixer",
    )(x, _row(n1), wm, wa, wup, _row(b_alpha), _row(out_norm), w_out.astype(BF16))


def _mlstm_kernel(x_ref, n1_ref, wm_ref, wif_ref, wift_ref, bif_ref, bift_ref, on_ref, wo_ref, o_ref,
                  h_ref, proj_ref, gc_ref, ob_ref, c_ref, n_ref, m_ref):
    tb = x_ref.shape[0]
    nh = ML_HEADS
    hk, hv = nh * ML_DK, nh * ML_DV

    @pl.when(pl.program_id(0) == 0)
    def _():
        c_ref[...] = jnp.zeros_like(c_ref)
        n_ref[...] = jnp.zeros_like(n_ref)
        m_ref[...] = jnp.zeros_like(m_ref)

    x = x_ref[...]
    h = _rms(x, n1_ref[...]).astype(BF16)
    h_ref[...] = h
    proj_ref[...] = _dot(h, wm_ref[...])
    gc_ref[...] = _dot(h, wif_ref[...]) + bif_ref[...]
    row, col = _tri_masks(ML_CHUNK)
    causal = row >= col
    tri = causal.astype(BF16)
    tri3 = jnp.concatenate([tri, tri, tri], axis=1)
    scale = ML_DK ** -0.5

    for c in range(tb // ML_CHUNK):
        rows = slice(c * ML_CHUNK, (c + 1) * ML_CHUNK)
        g_col = gc_ref[rows, :]
        g_row = _dot_nt(wift_ref[...], h_ref[rows, :]) + bift_ref[...]
        b_col = _cumsum_rows(tri3, _log_sigmoid(g_col))
        b_row = _dot_nt(_split3(_log_sigmoid(g_row), 1), tri3)
        def head(hh):
            bc, ic = b_col[:, nh + hh:nh + hh + 1], g_col[:, hh:hh + 1]
            br, ir = b_row[nh + hh:nh + hh + 1, :], g_row[hh:hh + 1, :]
            q = proj_ref[rows, hh * ML_DK:(hh + 1) * ML_DK]
            k = proj_ref[rows, hk + hh * ML_DK:hk + (hh + 1) * ML_DK] * scale
            v = proj_ref[rows, 2 * hk + hh * ML_DV:2 * hk + (hh + 1) * ML_DV]
            m_prev = m_ref[0:1, hh:hh + 1]
            d_log = jnp.where(causal, bc - br + ir, -jnp.inf)
            inter = bc + m_prev
            m_t = jnp.maximum(inter, jnp.max(d_log, axis=-1, keepdims=True))
            w_intra = jnp.exp(d_log - m_t)
            sc = jnp.exp(inter - m_t)
            c_st, n_st = c_ref[hh], n_ref[hh]
            qk = _bdot_nt(q, k)
            qc = _bdot(q, c_st)
            yield
            qk = qk * w_intra
            b_last = bc[ML_CHUNK - 1:ML_CHUNK, :]
            g_end = b_last - bc + ic
            m_new = jnp.maximum(b_last + m_prev, jnp.max(g_end, axis=0, keepdims=True))
            kw = k * jnp.exp(g_end - m_new)
            carry_scale = jnp.exp(b_last + m_prev - m_new)
            num = _bdot(qk, v) + sc * qc
            c_ref[hh] = carry_scale * c_st + _bdot_tn(kw, v)
            yield
            den = jnp.sum(qk, axis=-1, keepdims=True) + sc * jnp.sum(q * n_st, axis=-1, keepdims=True)
            ob_ref[rows, hh * ML_DV:(hh + 1) * ML_DV] = num / jnp.maximum(jnp.abs(den), jnp.exp(-m_t))
            n_ref[hh] = carry_scale * n_st + jnp.sum(kw, axis=0, keepdims=True)
            m_ref[0:1, hh:hh + 1] = m_new

        _lockstep([head(hh) for hh in range(nh)])
    normed = [_rms(ob_ref[:, hh * ML_DV:(hh + 1) * ML_DV], on_ref[:, hh * ML_DV:(hh + 1) * ML_DV])
              for hh in range(nh)]
    o_pre = proj_ref[:, 2 * hk + hv:2 * hk + 2 * hv]
    y = jnp.concatenate(normed, axis=-1) * _sigmoid(o_pre)
    o_ref[...] = x + _bdot(y, wo_ref[...])


def _mlstm(x, n1, w_in, b_if, out_norm, w_out, tb=1024):
    s, d = x.shape
    nh = ML_HEADS
    hk, hv = nh * ML_DK, nh * ML_DV
    nm = 2 * hk + 2 * hv
    wm = w_in[:, :nm].astype(BF16)
    w_if = w_in[:, nm:]
    wif = jnp.pad(w_if, ((0, 0), (0, LANES - 2 * nh))).astype(BF16)
    wift = w_if.T.astype(BF16)
    return pl.pallas_call(
        _mlstm_kernel,
        grid=(s // tb,),
        in_specs=[pl.BlockSpec((tb, d), lambda i: (i, 0)), _full((1, d)), _full((d, nm)),
                  _full((d, LANES)), _full((2 * nh, d)), _full((1, LANES)), _full((2 * nh, 1)),
                  _full((1, hv)), _full((hv, d))],
        out_specs=pl.BlockSpec((tb, d), lambda i: (i, 0)),
        out_shape=jax.ShapeDtypeStruct((s, d), F32),
        scratch_shapes=[pltpu.VMEM((tb, d), BF16), pltpu.VMEM((tb, nm), F32), pltpu.VMEM((tb, LANES), F32),
                        pltpu.VMEM((tb, hv), F32), pltpu.VMEM((nh, ML_DK, ML_DV), F32),
                        pltpu.VMEM((nh, 1, ML_DK), F32), pltpu.VMEM((8, LANES), F32)],
        compiler_params=_params(("arbitrary",)),
        name="mlstm_mixer",
    )(x, _row(n1), wm, wif, wift, _row(b_if, LANES), b_if.reshape(2 * nh, 1).astype(F32),
      _row(out_norm), w_out.astype(BF16))


def _rwkv_proj_kernel(x_ref, n1_ref, mu_ref, wr_ref, wk_ref, wv_ref, w0_ref, w1_ref, w2_ref,
                      a0_ref, a1_ref, a2_ref, g1_ref, g2_ref, kk_ref, ka_ref,
                      r_ref, lw_ref, k_ref, v_ref, kkraw_ref, a_ref, g_ref, car_ref):
    tm = x_ref.shape[0]

    @pl.when(pl.program_id(0) == 0)
    def _():
        car_ref[...] = jnp.zeros_like(car_ref)

    h = _rms(x_ref[...], n1_ref[...])
    rowi = lax.broadcasted_iota(jnp.int32, h.shape, 0)
    h_prev = jnp.where(rowi == 0, car_ref[7:8, :], pltpu.roll(h, 1, 0))
    car_ref[...] = h[tm - 8:, :]
    xx = h_prev - h
    xr, xw, xk, xv, xa, xg = (h + xx * mu_ref[j:j + 1, :] for j in range(6))
    r = _bdot(xr, wr_ref[...])
    k = _bdot(xk, wk_ref[...])
    v = _bdot(xv, wv_ref[...])
    w_log = -_softplus(-(w0_ref[...] + _bdot(jnp.tanh(_bdot(xw, w1_ref[...])), w2_ref[...]))) - 0.5
    a = _sigmoid(a0_ref[...] + _bdot(_bdot(xa, a1_ref[...]), a2_ref[...]))
    g = _bdot(_sigmoid(_bdot(xg, g1_ref[...])), g2_ref[...])
    r_ref[...] = r
    lw_ref[...] = -jnp.exp(w_log)
    k_ref[...] = k * (1.0 + (a - 1.0) * ka_ref[...])
    v_ref[...] = v
    kkraw_ref[...] = k * kk_ref[...]
    a_ref[...] = a
    g_ref[...] = g


def _rwkv_proj(x, n1, mu, w_rkv, w0, w1, w2, a0, a1, a2, g1, g2, k_k, k_a, tm=512):
    s, d = x.shape

    def padc(w):
        return jnp.pad(w, ((0, 0), (0, LANES - w.shape[1]))).astype(BF16) if w.shape[1] < LANES else w.astype(BF16)

    def padr(w):
        return jnp.pad(w, ((0, LANES - w.shape[0]), (0, 0))).astype(BF16) if w.shape[0] < LANES else w.astype(BF16)

    mu8 = jnp.pad(mu, ((0, 8 - mu.shape[0]), (0, 0)))
    blk = pl.BlockSpec((tm, d), lambda i: (i, 0))
    out = jax.ShapeDtypeStruct((s, d), F32)
    return pl.pallas_call(
        _rwkv_proj_kernel,
        grid=(s // tm,),
        in_specs=[blk, _full((1, d)), _full((8, d)), _full((d, d)), _full((d, d)), _full((d, d)),
                  _full((1, d)), _full((d, LANES)), _full((LANES, d)),
                  _full((1, d)), _full((d, LANES)), _full((LANES, d)),
                  _full((d, LANES)), _full((LANES, d)), _full((1, d)), _full((1, d))],
        out_specs=[blk] * 7,
        out_shape=[out] * 7,
        scratch_shapes=[pltpu.VMEM((8, d), F32)],
        compiler_params=_params(("arbitrary",)),
        name="rwkv_proj",
    )(x, _row(n1), mu8, w_rkv[0].astype(BF16), w_rkv[1].astype(BF16), w_rkv[2].astype(BF16),
      _row(w0), padc(w1), padr(w2), _row(a0), padc(a1), padr(a2), padc(g1), padr(g2), _row(k_k), _row(k_a))


def _unit_lower_inverse(a_strict, row, col):
    n = a_strict.shape[0]
    eye = (row == col).astype(F32)
    blk16 = (row // 16) == (col // 16)
    p = jnp.where(blk16, a_strict, 0.0)
    inv = eye + p
    p = _bdot(p, p)
    yield
    for _ in range(2):
        both = _bdot(jnp.concatenate([p, inv], axis=0), p)
        yield
        p, inv = both[:n], inv + both[n:]
    inv16 = inv + _bdot(inv, p)
    yield
    blk32 = (row // 32) == (col // 32)
    off32 = jnp.where(blk32 & ~blk16, a_strict, 0.0)
    off64 = jnp.where(blk32, 0.0, a_strict)
    x = _bdot(inv16, jnp.concatenate([off32, off64], axis=1))
    yield
    x32, x64 = x[:, :n], x[:, n:]
    y = _bdot(x32, jnp.concatenate([inv16, x64], axis=1))
    yield
    inv32 = inv16 + y[:, :n]
    return inv32 + _bdot(x64 + y[:, n:], inv32)


def _rwkv_scan_kernel(r_ref, lw_ref, k_ref, v_ref, kkraw_ref, a_ref, g_ref, rk_ref, gg_ref, gb_ref, o_ref, s_ref):
    tb = r_ref.shape[0]
    n = RWKV_HEAD
    grp = 2 * CHUNK
    big = 2 * grp

    @pl.when(pl.program_id(1) == 0)
    def _():
        s_ref[...] = jnp.zeros_like(s_ref)

    rown = lax.broadcasted_iota(jnp.int32, (grp, LANES), 0)
    head0 = lax.broadcasted_iota(jnp.int32, (1, LANES), 1) < n
    chunk0 = rown < CHUNK
    row, col = _tri_masks(big)
    same = (row // CHUNK) == (col // CHUNK)
    incl = same & (row >= col)
    strict = same & (row > col)
    tri = incl[:grp, :grp].astype(BF16)
    tri3 = jnp.concatenate([tri, tri, tri], axis=1)
    srow = lax.broadcasted_iota(jnp.int32, (big, LANES), 0)
    slane = lax.broadcasted_iota(jnp.int32, (big, LANES), 1)
    srow_chunk0 = (srow // CHUNK) % 2 == 0
    diag = (srow % grp) == slane
    own = ((rown < n) == head0)

    def stack(t):
        return jnp.concatenate([jnp.where(head0, t, 0.0), jnp.where(head0, 0.0, t)], axis=0)

    def blockdiag(t):
        s = stack(t)
        return jnp.concatenate([jnp.where(srow_chunk0, s, 0.0), jnp.where(srow_chunk0, 0.0, s)], axis=1)

    def spread(t):
        return jnp.concatenate([jnp.where(srow < grp, t, 0.0), jnp.where(srow < grp, 0.0, t)], axis=1)

    def head_sum(t):
        return jnp.where(head0, jnp.sum(jnp.where(head0, t, 0.0), axis=-1, keepdims=True),
                         jnp.sum(jnp.where(head0, 0.0, t), axis=-1, keepdims=True))

    def per_chunk(t, c):
        return jnp.concatenate([t[c * CHUNK:(c + 1) * CHUNK], t[grp + c * CHUNK:grp + (c + 1) * CHUNK]], axis=0)

    def phase1(g):
        rows = slice(g * grp, (g + 1) * grp)
        r, lw, k, v = r_ref[rows, :], lw_ref[rows, :], k_ref[rows, :], v_ref[rows, :]
        kk, a = kkraw_ref[rows, :], a_ref[rows, :]
        kk = kk / jnp.maximum(jnp.sqrt(head_sum(kk * kk)), 1e-12)
        cum = _cumsum_rows(tri3, lw)
        yield
        cum_last = jnp.where(chunk0, cum[CHUNK - 1:CHUNK, :], cum[grp - 1:grp, :])
        e_neg, e_end = jnp.exp(-cum), jnp.exp(cum_last - cum)
        b_vec = kk * a
        a_t = stack(-kk * jnp.exp(cum - lw))
        r_t = stack(r * jnp.exp(cum))
        v_s = stack(v)
        a4 = _bdot_nt(jnp.concatenate([a_t, r_t], axis=0),
                      jnp.concatenate([b_vec * e_neg, k * e_neg], axis=0))
        yield
        a_ab = jnp.where(strict, spread(a4[:big, :grp]), 0.0)
        a_ak = jnp.where(strict, spread(a4[:big, grp:]), 0.0)
        a_rb = jnp.where(incl, spread(a4[big:, :grp]), 0.0)
        a_rk = jnp.where(incl, spread(a4[big:, grp:]), 0.0)
        akv = _bdot(a_ak, v_s)
        t_inv = yield from _unit_lower_inverse(a_ab, row, col)
        yield
        z = _bdot(t_inv, jnp.concatenate([a_t, akv], axis=1))
        yield
        zv = jnp.concatenate([z, jnp.concatenate([jnp.zeros_like(v_s), v_s], axis=1)], axis=0)
        gh = _bdot_tn(jnp.concatenate([blockdiag(b_vec * e_end), blockdiag(k * e_end)], axis=0), zv)
        ry = _bdot(jnp.concatenate([a_rb, a_rk], axis=1), zv)
        p_end = jnp.exp(jnp.concatenate([cum[CHUNK - 1:CHUNK, :], cum[grp - 1:grp, :]], axis=0))
        g_all = gh[:, :LANES] + jnp.where(diag, jnp.where(srow < grp, p_end[0:1, :], p_end[1:2, :]), 0.0)
        h_all = gh[:, LANES:]
        rp_all = r_t + ry[:, :LANES]
        y0_all = ry[:, LANES:]
        return g_all, h_all, rp_all, y0_all

    def emit(g, c, y):
        crow = slice(g * grp + c * CHUNK, g * grp + (c + 1) * CHUNK)
        mean = jnp.sum(y, axis=-1, keepdims=True) * (1.0 / n)
        dev = jnp.where(own, y - mean, 0.0)
        var = jnp.sum(dev * dev, axis=-1, keepdims=True) * (1.0 / n)
        yn = dev * lax.rsqrt(var + RWKV_GN_EPS)
        yn = yn[:CHUNK] + yn[CHUNK:]
        rc, kc, vc = r_ref[crow, :], k_ref[crow, :], v_ref[crow, :]
        bonus = head_sum(rc * kc * rk_ref[...])[:CHUNK] * vc
        o_ref[crow, :] = ((yn * gg_ref[...] + gb_ref[...] + bonus) * g_ref[crow, :]).astype(BF16)

    def phase2(g0, staged):
        for g, (g_all, h_all, rp_all, y0_all) in enumerate(staged, g0):
            for c in range(2):
                lhs = jnp.concatenate([g_all[c * grp:(c + 1) * grp], per_chunk(rp_all, c)], axis=0)
                both = _bdot(lhs, s_ref[...])
                yield
                s_ref[...] = both[:grp] + h_all[c * grp:(c + 1) * grp]
                emit(g, c, both[grp:] + per_chunk(y0_all, c))

    n_groups, wave = tb // grp, 4
    pending = None
    for g0 in range(0, n_groups, wave):
        gens = [phase1(g) for g in range(g0, min(g0 + wave, n_groups))]
        done = _lockstep(gens + ([pending] if pending is not None else []))
        pending = phase2(g0, done[:len(gens)])
    _lockstep([pending])


def _rwkv_scan(r, lw, k, v, kkraw, a, g, r_k, gn_g, gn_b, tb=2048):
    s, d = r.shape
    tb = min(tb, s)
    blk = pl.BlockSpec((tb, LANES), lambda p, i: (i, p))
    vec = pl.BlockSpec((1, LANES), lambda p, i: (0, p))
    return pl.pallas_call(
        _rwkv_scan_kernel,
        grid=(d // LANES, s // tb),
        in_specs=[blk] * 7 + [vec] * 3,
        out_specs=blk,
        out_shape=jax.ShapeDtypeStruct((s, d), BF16),
        scratch_shapes=[pltpu.VMEM((LANES, LANES), F32)],
        compiler_params=_params(("arbitrary", "arbitrary")),
        name="rwkv_scan",
    )(r, lw, k, v, kkraw, a, g, _row(r_k), _row(gn_g), _row(gn_b))


def _sb_qkv_kernel(x_ref, n1_ref, w_ref, q_ref, k_ref, v_ref):
    d = x_ref.shape[1]
    h = _rms(x_ref[...], n1_ref[...]).astype(BF16)
    qkv = _dot(h, w_ref[...])
    q_ref[...] = (qkv[:, :d] * SB_HEAD_DIM ** -0.5).astype(BF16)
    k_ref[...] = qkv[:, d:2 * d].astype(BF16)
    v_ref[...] = qkv[:, 2 * d:].astype(BF16)


def _sb_qkv(x, n1, w_qkv, tm=1024):
    s, d = x.shape
    blk = pl.BlockSpec((tm, d), lambda i: (i, 0))
    out = jax.ShapeDtypeStruct((s, d), BF16)
    return pl.pallas_call(
        _sb_qkv_kernel,
        grid=(s // tm,),
        in_specs=[blk, _full((1, d)), _full((d, 3 * d))],
        out_specs=[blk] * 3,
        out_shape=[out] * 3,
        compiler_params=_params(("arbitrary",)),
        name="sb_qkv",
    )(x, _row(n1), w_qkv.astype(BF16))


def _sb_attn_kernel(q_ref, k_ref, v_ref, o_ref, *, tq):
    qi = pl.program_id(1)
    head0 = lax.broadcasted_iota(jnp.int32, (1, LANES), 1) < SB_HEAD_DIM
    q = q_ref[...]
    zero = jnp.zeros_like(q)
    q2 = jnp.concatenate([jnp.where(head0, q, zero), jnp.where(head0, zero, q)], axis=0)
    row, col = _tri_masks(tq)
    before = jnp.concatenate([col < row, col < row], axis=0)
    later = (row > col).astype(BF16)
    later2 = jnp.concatenate([later, later], axis=0)

    def pair(kb, c, acc, diagonal):
        valid = kb >= 1
        ks = (pl.ds(pl.multiple_of(kb * tq, tq), tq), pl.ds(pl.multiple_of(jnp.maximum(kb - 1, 0) * tq, tq), tq))
        zs = [_dot_nt(q2, k_ref[s, :]) for s in ks]
        sps, excls = [], []
        for j, z in enumerate(zs):
            sp = _softplus(z)
            if diagonal and j == 0:
                sp = jnp.where(before, sp, 0.0)
            hi = sp.astype(BF16)
            sps.append(sp)
            excls.append(_dot(jnp.concatenate([hi, (sp - hi.astype(F32)).astype(BF16)], axis=1), later2))
        tot0 = excls[0][:, 0:1] + sps[0][:, 0:1]
        tot1 = jnp.where(valid, excls[1][:, 0:1] + sps[1][:, 0:1], 0.0)
        w0 = jnp.exp(zs[0] - sps[0] - excls[0] + c)
        if diagonal:
            w0 = jnp.where(before, w0, 0.0)
        w1 = jnp.where(valid, jnp.exp(zs[1] - sps[1] - excls[1] + (c - tot0)), 0.0)
        pv = _dot(jnp.concatenate([w0.astype(BF16), w1.astype(BF16)], axis=1),
                  jnp.concatenate([v_ref[ks[0], :], v_ref[ks[1], :]], axis=0))
        return c - tot0 - tot1, acc + pv

    c, acc = pair(qi, jnp.zeros((2 * tq, 1), F32), jnp.zeros((2 * tq, LANES), F32), True)

    def cond(st):
        kb, c, _ = st
        return jnp.logical_and(kb >= 0, jnp.max(c) > SB_ZERO_LOG)

    def body(st):
        kb, c, acc = st
        c, acc = pair(kb, c, acc, False)
        return kb - 2, c, acc

    _, _, acc = lax.while_loop(cond, body, (qi - 2, c, acc))
    o_ref[...] = jnp.where(head0, acc[:tq], acc[tq:]).astype(BF16)


def _sb_attn(q, k, v, tq=256):
    s, d = q.shape
    return pl.pallas_call(
        functools.partial(_sb_attn_kernel, tq=tq),
        grid=(d // LANES, s // tq),
        in_specs=[pl.BlockSpec((tq, LANES), lambda p, i: (i, p)),
                  pl.BlockSpec((s, LANES), lambda p, i: (0, p)),
                  pl.BlockSpec((s, LANES), lambda p, i: (0, p))],
        out_specs=pl.BlockSpec((tq, LANES), lambda p, i: (i, p)),
        out_shape=jax.ShapeDtypeStruct((s, d), BF16),
        compiler_params=_params(("arbitrary", "arbitrary")),
        name="sb_attn",
    )(q, k, v)


def _stick_breaking(x, n1, w_qkv):
    q, k, v = _sb_qkv(x, n1, w_qkv)
    return _sb_attn(q, k, v)


def _rwkv7(x, n1, mu, w_rkv, w0, w1, w2, a0, a1, a2, g1, g2, k_k, k_a, r_k, gn_g, gn_b):
    r, lw, k, v, kkraw, a, g = _rwkv_proj(x, n1, mu, w_rkv, w0, w1, w2, a0, a1, a2, g1, g2, k_k, k_a)
    return _rwkv_scan(r, lw, k, v, kkraw, a, g, r_k, gn_g, gn_b)


def kernel(x, l0_norm1, l0_gla_w_in, l0_gla_w_alpha_up, l0_gla_b_alpha, l0_gla_out_norm, l0_gla_w_out, l0_norm2, l0_ffn_w_up, l0_ffn_conv_w, l0_ffn_conv_b, l0_ffn_w_down, l1_norm1, l1_rwkv_mu, l1_rwkv_w_rkv, l1_rwkv_w0, l1_rwkv_w1, l1_rwkv_w2, l1_rwkv_a0, l1_rwkv_a1, l1_rwkv_a2, l1_rwkv_g1, l1_rwkv_g2, l1_rwkv_k_k, l1_rwkv_k_a, l1_rwkv_r_k, l1_rwkv_gn_g, l1_rwkv_gn_b, l1_rwkv_w_out, l1_norm2, l1_ffn_w_up, l1_ffn_conv_w, l1_ffn_conv_b, l1_ffn_w_down, l2_norm1, l2_sb_w_qkv, l2_sb_w_out, l2_norm2, l2_ffn_w_up, l2_ffn_conv_w, l2_ffn_conv_b, l2_ffn_w_down, l3_norm1, l3_ml_w_in, l3_ml_b_if, l3_ml_out_norm, l3_ml_w_out, l3_norm2, l3_ffn_w_up, l3_ffn_conv_w, l3_ffn_conv_b, l3_ffn_w_down, final_norm):
    b, s, d = x.shape
    outs = []
    for bi in range(b):
        h = x[bi]
        h = _gla(h, l0_norm1, l0_gla_w_in, l0_gla_w_alpha_up, l0_gla_b_alpha, l0_gla_out_norm, l0_gla_w_out)
        h = _ffn(h, l0_norm2, l0_ffn_w_up, l0_ffn_conv_w, l0_ffn_conv_b, l0_ffn_w_down)
        a = _rwkv7(h, l1_norm1, l1_rwkv_mu, l1_rwkv_w_rkv, l1_rwkv_w0, l1_rwkv_w1, l1_rwkv_w2,
                   l1_rwkv_a0, l1_rwkv_a1, l1_rwkv_a2, l1_rwkv_g1, l1_rwkv_g2,
                   l1_rwkv_k_k, l1_rwkv_k_a, l1_rwkv_r_k, l1_rwkv_gn_g, l1_rwkv_gn_b)
        h = _ffn(h, l1_norm2, l1_ffn_w_up, l1_ffn_conv_w, l1_ffn_conv_b, l1_ffn_w_down, mixer=(a, l1_rwkv_w_out))
        a = _stick_breaking(h, l2_norm1, l2_sb_w_qkv)
        h = _ffn(h, l2_norm2, l2_ffn_w_up, l2_ffn_conv_w, l2_ffn_conv_b, l2_ffn_w_down, mixer=(a, l2_sb_w_out))
        h = _mlstm(h, l3_norm1, l3_ml_w_in, l3_ml_b_if, l3_ml_out_norm, l3_ml_w_out)
        h = _ffn(h, l3_norm2, l3_ffn_w_up, l3_ffn_conv_w, l3_ffn_conv_b, l3_ffn_w_down, final_g=final_norm)
        outs.append(h)
    return jnp.stack(outs, axis=0)
```

```python
import functools

import jax
import jax.numpy as jnp
from jax import lax
from jax.experimental import pallas as pl
from jax.experimental.pallas import tpu as pltpu

F32 = jnp.float32
BF16 = jnp.bfloat16

NORM_EPS = 1e-6
LOG2E = 1.4426950408889634
CHUNK = 64
GLA_HEADS, GLA_DK, GLA_DV, GLA_TAU = 4, 128, 256, 16.0
ML_HEADS, ML_DK, ML_DV = 4, 128, 256
ML_CHUNK = 256
RWKV_HEAD = 64
RWKV_GN_EPS = 64e-5
SB_HEAD_DIM = 64
LANES = 128
V7X_VMEM_BYTES = 64 * 1024 * 1024
VMEM_LIMIT_BYTES = V7X_VMEM_BYTES * 7 // 8
SB_ZERO_LOG = -104.0


def _rms(x, g, eps=NORM_EPS):
    return x * lax.rsqrt(jnp.mean(x * x, axis=-1, keepdims=True) + eps) * g


def _softplus(x):
    return jnp.maximum(x, 0.0) + jnp.log(1.0 + jnp.exp2(jnp.abs(x) * -LOG2E))


def _log_sigmoid(x):
    return -_softplus(-x)


def _sigmoid(x):
    return 1.0 / (1.0 + jnp.exp2(x * -LOG2E))


def _dot(a, b, **kw):
    return jnp.dot(a, b, preferred_element_type=F32, **kw)


def _dot_nt(a, b, **kw):
    return lax.dot_general(a, b, (((1,), (1,)), ((), ())), preferred_element_type=F32, **kw)


def _dot_tn(a, b, **kw):
    return lax.dot_general(a, b, (((0,), (0,)), ((), ())), preferred_element_type=F32, **kw)


def _bdot(a, b):
    return _dot(a.astype(BF16), b.astype(BF16))


def _bdot_nt(a, b):
    return _dot_nt(a.astype(BF16), b.astype(BF16))


def _bdot_tn(a, b):
    return _dot_tn(a.astype(BF16), b.astype(BF16))


def _split3(x, axis):
    hi = x.astype(BF16)
    r1 = x - hi.astype(F32)
    mid = r1.astype(BF16)
    lo = (r1 - mid.astype(F32)).astype(BF16)
    return jnp.concatenate([hi, mid, lo], axis=axis)


def _cumsum_rows(tri3, x):
    return _dot(tri3, _split3(x, 0))


def _lockstep(gens):
    out = [None] * len(gens)
    live = list(range(len(gens)))
    while live:
        for i in list(live):
            try:
                next(gens[i])
            except StopIteration as stop:
                out[i] = stop.value
                live.remove(i)
    return out


def _tri_masks(n):
    row = lax.broadcasted_iota(jnp.int32, (n, n), 0)
    col = lax.broadcasted_iota(jnp.int32, (n, n), 1)
    return row, col


def _full(shape):
    return pl.BlockSpec(shape, lambda *_: (0,) * len(shape), pipeline_mode=pl.Buffered(1))


def _params(sem):
    return pltpu.CompilerParams(dimension_semantics=sem, vmem_limit_bytes=VMEM_LIMIT_BYTES)


def _row(v, width=None):
    v = v.reshape(1, -1).astype(F32)
    if width is not None and v.shape[1] < width:
        v = jnp.pad(v, ((0, 0), (0, width - v.shape[1])))
    return v


def _ffn_kernel(*refs, final_norm, fc, mixer):
    if mixer:
        a_ref, wo_ref, *refs = refs
    x_ref, g_ref, wup_ref, cw_ref, wd_ref, fg_ref, o_ref, h_ref, act_ref, car_ref = refs
    tm = x_ref.shape[0]
    f = wd_ref.shape[0]

    @pl.when(pl.program_id(0) == 0)
    def _():
        car_ref[...] = jnp.zeros_like(car_ref)

    o_ref[...] = x_ref[...] + _dot(a_ref[...], wo_ref[...]) if mixer else x_ref[...]
    h_ref[...] = _rms(o_ref[...], g_ref[...]).astype(BF16)
    rowi = lax.broadcasted_iota(jnp.int32, (tm, fc), 0)

    def conv_up(cols):
        u = _dot(h_ref[...], wup_ref[:, cols])
        c2, c1 = car_ref[6:7, cols], car_ref[7:8, cols]
        s1 = jnp.where(rowi == 0, c1, pltpu.roll(u, 1, 0))
        s2 = jnp.where(rowi == 0, c2, jnp.where(rowi == 1, c1, pltpu.roll(u, 2, 0)))
        car_ref[:, cols] = u[tm - 8:, :]
        return s2 * cw_ref[0:1, cols] + s1 * cw_ref[1:2, cols] + u * cw_ref[2:3, cols] + cw_ref[3:4, cols]

    for c in range(f // fc):
        gate = conv_up(slice(c * fc, (c + 1) * fc))
        up = conv_up(slice(f + c * fc, f + (c + 1) * fc))
        act_ref[:, c * fc:(c + 1) * fc] = (gate * _sigmoid(gate) * up).astype(BF16)
    y = o_ref[...] + _dot(act_ref[...], wd_ref[...])
    if final_norm:
        y = _rms(y, fg_ref[...])
    o_ref[...] = y


def _ffn(x, norm_g, w_up, conv_w, conv_b, w_down, final_g=None, mixer=None, tm=1024, fc=256):
    s, d = x.shape
    f = w_down.shape[0]
    taps = jnp.concatenate([conv_w, conv_b[None, :], jnp.zeros((4, 2 * f), F32)], axis=0)
    fg = _row(final_g if final_g is not None else jnp.ones((d,), F32))
    blk = pl.BlockSpec((tm, d), lambda i: (i, 0))
    pre_specs, pre_args = ([blk, _full((d, d))], (mixer[0], mixer[1].astype(BF16))) if mixer else ([], ())
    return pl.pallas_call(
        functools.partial(_ffn_kernel, final_norm=final_g is not None, fc=fc, mixer=mixer is not None),
        grid=(s // tm,),
        in_specs=pre_specs + [blk, _full((1, d)), _full((d, 2 * f)), _full((8, 2 * f)), _full((f, d)), _full((1, d))],
        out_specs=blk,
        out_shape=jax.ShapeDtypeStruct((s, d), F32),
        scratch_shapes=[pltpu.VMEM((tm, d), BF16), pltpu.VMEM((tm, f), BF16), pltpu.VMEM((8, 2 * f), F32)],
        compiler_params=_params(("arbitrary",)),
        name="conv_ffn",
    )(*pre_args, x, _row(norm_g), w_up.astype(BF16), taps, w_down.astype(BF16), fg)


def _gla_kernel(x_ref, n1_ref, wm_ref, wa_ref, wup_ref, ba_ref, on_ref, wo_ref, o_ref,
                proj_ref, la_ref, ob_ref, st_ref):
    tb = x_ref.shape[0]
    hk, hv = GLA_HEADS * GLA_DK, GLA_HEADS * GLA_DV

    @pl.when(pl.program_id(0) == 0)
    def _():
        st_ref[...] = jnp.zeros_like(st_ref)

    x = x_ref[...]
    h = _rms(x, n1_ref[...]).astype(BF16)
    proj_ref[...] = _dot(h, wm_ref[...])
    a_low = _dot(h, wa_ref[...])
    la_ref[...] = _log_sigmoid(_bdot(a_low, wup_ref[...]) + ba_ref[...]) * (1.0 / GLA_TAU)
    grp = 4 * CHUNK
    row, col = _tri_masks(grp)
    causal = ((row // CHUNK) == (col // CHUNK)) & (row >= col)
    tri = causal.astype(BF16)
    tri3 = jnp.concatenate([tri, tri, tri], axis=1)
    rowc = lax.broadcasted_iota(jnp.int32, (grp, hk), 0) // CHUNK
    head_of_lane = lax.broadcasted_iota(jnp.int32, (1, hk), 1) // GLA_DK
    scale = GLA_DK ** -0.5

    def stack(t):
        return jnp.concatenate([jnp.where(head_of_lane == hh, t, 0.0) for hh in range(GLA_HEADS)], axis=0)

    for g in range(tb // grp):
        r0 = g * grp
        rows = slice(r0, r0 + grp)
        b = _cumsum_rows(tri3, la_ref[rows, :])
        b_last = b[CHUNK - 1:CHUNK, :]
        for c in range(1, 4):
            b_last = jnp.where(rowc == c, b[(c + 1) * CHUNK - 1:(c + 1) * CHUNK, :], b_last)
        q_dec = proj_ref[rows, 0:hk] * scale * jnp.exp(b)
        k = proj_ref[rows, hk:2 * hk]
        k_inv = k * jnp.exp(-b)
        k_end = k * jnp.exp(b_last - b)
        scores = [jnp.where(causal, _bdot_nt(q_dec[:, hh * GLA_DK:(hh + 1) * GLA_DK],
                                             k_inv[:, hh * GLA_DK:(hh + 1) * GLA_DK]), 0.0)
                  for hh in range(GLA_HEADS)]
        upd = []
        for c in range(4):
            cr = slice(r0 + c * CHUNK, r0 + (c + 1) * CHUNK)
            v_s = jnp.concatenate([proj_ref[cr, 2 * hk + hh * GLA_DV:2 * hk + (hh + 1) * GLA_DV]
                                   for hh in range(GLA_HEADS)], axis=0)
            upd.append(_bdot_tn(v_s, stack(k_end[c * CHUNK:(c + 1) * CHUNK, :])))
        intra = [_bdot(scores[hh], proj_ref[rows, 2 * hk + hh * GLA_DV:2 * hk + (hh + 1) * GLA_DV])
                 for hh in range(GLA_HEADS)]
        st = st_ref[...]
        inter = []
        for c in range(4):
            inter.append(_bdot_nt(stack(q_dec[c * CHUNK:(c + 1) * CHUNK, :]), st))
            st = st * jnp.exp(b[(c + 1) * CHUNK - 1:(c + 1) * CHUNK, :]) + upd[c]
        st_ref[...] = st
        for hh in range(GLA_HEADS):
            ob_ref[rows, hh * GLA_DV:(hh + 1) * GLA_DV] = intra[hh] + jnp.concatenate(
                [inter[c][hh * CHUNK:(hh + 1) * CHUNK, :] for c in range(4)], axis=0)
    normed = [_rms(ob_ref[:, hh * GLA_DV:(hh + 1) * GLA_DV], on_ref[:, hh * GLA_DV:(hh + 1) * GLA_DV])
              for hh in range(GLA_HEADS)]
    r = proj_ref[:, 2 * hk + hv:2 * hk + 2 * hv]
    y = jnp.concatenate(normed, axis=-1) * (r * _sigmoid(r))
    o_ref[...] = x + _bdot(y, wo_ref[...])


def _gla(x, n1, w_in, w_alpha_up, b_alpha, out_norm, w_out, tb=1024):
    s, d = x.shape
    hk, hv = GLA_HEADS * GLA_DK, GLA_HEADS * GLA_DV
    nm = 2 * hk + 2 * hv
    rank = w_alpha_up.shape[0]
    wm = w_in[:, :nm].astype(BF16)
    wa = jnp.pad(w_in[:, nm:], ((0, 0), (0, LANES - rank))).astype(BF16)
    wup = jnp.pad(w_alpha_up, ((0, LANES - rank), (0, 0))).astype(BF16)
    return pl.pallas_call(
        _gla_kernel,
        grid=(s // tb,),
        in_specs=[pl.BlockSpec((tb, d), lambda i: (i, 0)), _full((1, d)), _full((d, nm)),
                  _full((d, LANES)), _full((LANES, hk)), _full((1, hk)), _full((1, hv)), _full((hv, d))],
        out_specs=pl.BlockSpec((tb, d), lambda i: (i, 0)),
        out_shape=jax.ShapeDtypeStruct((s, d), F32),
        scratch_shapes=[pltpu.VMEM((tb, nm), F32), pltpu.VMEM((tb, hk), F32), pltpu.VMEM((tb, hv), F32),
                        pltpu.VMEM((GLA_DV, hk), F32)],
        compiler_params=_params(("arbitrary",)),
        name="gla_mixer",
    )(x, _row(n1), wm, wa, wup, _row(b_alpha), _row(out_norm), w_out.astype(BF16))


def _mlstm_kernel(x_ref, n1_ref, wm_ref, wif_ref, wift_ref, bif_ref, bift_ref, on_ref, wo_ref, o_ref,
                  h_ref, proj_ref, gc_ref, ob_ref, c_ref, n_ref, m_ref):
    tb = x_ref.shape[0]
    nh = ML_HEADS
    hk, hv = nh * ML_DK, nh * ML_DV

    @pl.when(pl.program_id(0) == 0)
    def _():
        c_ref[...] = jnp.zeros_like(c_ref)
        n_ref[...] = jnp.zeros_like(n_ref)
        m_ref[...] = jnp.zeros_like(m_ref)

    x = x_ref[...]
    h = _rms(x, n1_ref[...]).astype(BF16)
    h_ref[...] = h
    proj_ref[...] = _dot(h, wm_ref[...])
    gc_ref[...] = _dot(h, wif_ref[...]) + bif_ref[...]
    row, col = _tri_masks(ML_CHUNK)
    causal = row >= col
    tri = causal.astype(BF16)
    tri3 = jnp.concatenate([tri, tri, tri], axis=1)
    scale = ML_DK ** -0.5

    for c in range(tb // ML_CHUNK):
        rows = slice(c * ML_CHUNK, (c + 1) * ML_CHUNK)
        g_col = gc_ref[rows, :]
        g_row = _dot_nt(wift_ref[...], h_ref[rows, :]) + bift_ref[...]
        b_col = _cumsum_rows(tri3, _log_sigmoid(g_col))
        b_row = _dot_nt(_split3(_log_sigmoid(g_row), 1), tri3)
        def head(hh):
            bc, ic = b_col[:, nh + hh:nh + hh + 1], g_col[:, hh:hh + 1]
            br, ir = b_row[nh + hh:nh + hh + 1, :], g_row[hh:hh + 1, :]
            q = proj_ref[rows, hh * ML_DK:(hh + 1) * ML_DK]
            k = proj_ref[rows, hk + hh * ML_DK:hk + (hh + 1) * ML_DK] * scale
            v = proj_ref[rows, 2 * hk + hh * ML_DV:2 * hk + (hh + 1) * ML_DV]
            m_prev = m_ref[0:1, hh:hh + 1]
            d_log = jnp.where(causal, bc - br + ir, -jnp.inf)
            inter = bc + m_prev
            m_t = jnp.maximum(inter, jnp.max(d_log, axis=-1, keepdims=True))
            w_intra = jnp.exp(d_log - m_t)
            sc = jnp.exp(inter - m_t)
            c_st, n_st = c_ref[hh], n_ref[hh]
            qk = _bdot_nt(q, k)
            qc = _bdot(q, c_st)
            yield
            qk = qk * w_intra
            b_last = bc[ML_CHUNK - 1:ML_CHUNK, :]
            g_end = b_last - bc + ic
            m_new = jnp.maximum(b_last + m_prev, jnp.max(g_end, axis=0, keepdims=True))
            kw = k * jnp.exp(g_end - m_new)
            carry_scale = jnp.exp(b_last + m_prev - m_new)
            num = _bdot(qk, v) + sc * qc
            c_ref[hh] = carry_scale * c_st + _bdot_tn(kw, v)
            yield
            den = jnp.sum(qk, axis=-1, keepdims=True) + sc * jnp.sum(q * n_st, axis=-1, keepdims=True)
            ob_ref[rows, hh * ML_DV:(hh + 1) * ML_DV] = num / jnp.maximum(jnp.abs(den), jnp.exp(-m_t))
            n_ref[hh] = carry_scale * n_st + jnp.sum(kw, axis=0, keepdims=True)
            m_ref[0:1, hh:hh + 1] = m_new

        _lockstep([head(hh) for hh in range(nh)])
    normed = [_rms(ob_ref[:, hh * ML_DV:(hh + 1) * ML_DV], on_ref[:, hh * ML_DV:(hh + 1) * ML_DV])
              for hh in range(nh)]
    o_pre = proj_ref[:, 2 * hk + hv:2 * hk + 2 * hv]
    y = jnp.concatenate(normed, axis=-1) * _sigmoid(o_pre)
    o_ref[...] = x + _bdot(y, wo_ref[...])


def _mlstm(x, n1, w_in, b_if, out_norm, w_out, tb=1024):
    s, d = x.shape
    nh = ML_HEADS
    hk, hv = nh * ML_DK, nh * ML_DV
    nm = 2 * hk + 2 * hv
    wm = w_in[:, :nm].astype(BF16)
    w_if = w_in[:, nm:]
    wif = jnp.pad(w_if, ((0, 0), (0, LANES - 2 * nh))).astype(BF16)
    wift = w_if.T.astype(BF16)
    return pl.pallas_call(
        _mlstm_kernel,
        grid=(s // tb,),
        in_specs=[pl.BlockSpec((tb, d), lambda i: (i, 0)), _full((1, d)), _full((d, nm)),
                  _full((d, LANES)), _full((2 * nh, d)), _full((1, LANES)), _full((2 * nh, 1)),
                  _full((1, hv)), _full((hv, d))],
        out_specs=pl.BlockSpec((tb, d), lambda i: (i, 0)),
        out_shape=jax.ShapeDtypeStruct((s, d), F32),
        scratch_shapes=[pltpu.VMEM((tb, d), BF16), pltpu.VMEM((tb, nm), F32), pltpu.VMEM((tb, LANES), F32),
                        pltpu.VMEM((tb, hv), F32), pltpu.VMEM((nh, ML_DK, ML_DV), F32),
                        pltpu.VMEM((nh, 1, ML_DK), F32), pltpu.VMEM((8, LANES), F32)],
        compiler_params=_params(("arbitrary",)),
        name="mlstm_mixer",
    )(x, _row(n1), wm, wif, wift, _row(b_if, LANES), b_if.reshape(2 * nh, 1).astype(F32),
      _row(out_norm), w_out.astype(BF16))


def _rwkv_proj_kernel(x_ref, n1_ref, mu_ref, wr_ref, wk_ref, wv_ref, w0_ref, w1_ref, w2_ref,
                      a0_ref, a1_ref, a2_ref, g1_ref, g2_ref, kk_ref, ka_ref,
                      r_ref, lw_ref, k_ref, v_ref, kkraw_ref, a_ref, g_ref, car_ref):
    tm = x_ref.shape[0]

    @pl.when(pl.program_id(0) == 0)
    def _():
        car_ref[...] = jnp.zeros_like(car_ref)

    h = _rms(x_ref[...], n1_ref[...])
    rowi = lax.broadcasted_iota(jnp.int32, h.shape, 0)
    h_prev = jnp.where(rowi == 0, car_ref[7:8, :], pltpu.roll(h, 1, 0))
    car_ref[...] = h[tm - 8:, :]
    xx = h_prev - h
    xr, xw, xk, xv, xa, xg = (h + xx * mu_ref[j:j + 1, :] for j in range(6))
    r = _bdot(xr, wr_ref[...])
    k = _bdot(xk, wk_ref[...])
    v = _bdot(xv, wv_ref[...])
    w_log = -_softplus(-(w0_ref[...] + _bdot(jnp.tanh(_bdot(xw, w1_ref[...])), w2_ref[...]))) - 0.5
    a = _sigmoid(a0_ref[...] + _bdot(_bdot(xa, a1_ref[...]), a2_ref[...]))
    g = _bdot(_sigmoid(_bdot(xg, g1_ref[...])), g2_ref[...])
    r_ref[...] = r
    lw_ref[...] = -jnp.exp(w_log)
    k_ref[...] = k * (1.0 + (a - 1.0) * ka_ref[...])
    v_ref[...] = v
    kkraw_ref[...] = k * kk_ref[...]
    a_ref[...] = a
    g_ref[...] = g


def _rwkv_proj(x, n1, mu, w_rkv, w0, w1, w2, a0, a1, a2, g1, g2, k_k, k_a, tm=512):
    s, d = x.shape

    def padc(w):
        return jnp.pad(w, ((0, 0), (0, LANES - w.shape[1]))).astype(BF16) if w.shape[1] < LANES else w.astype(BF16)

    def padr(w):
        return jnp.pad(w, ((0, LANES - w.shape[0]), (0, 0))).astype(BF16) if w.shape[0] < LANES else w.astype(BF16)

    mu8 = jnp.pad(mu, ((0, 8 - mu.shape[0]), (0, 0)))
    blk = pl.BlockSpec((tm, d), lambda i: (i, 0))
    out = jax.ShapeDtypeStruct((s, d), F32)
    return pl.pallas_call(
        _rwkv_proj_kernel,
        grid=(s // tm,),
        in_specs=[blk, _full((1, d)), _full((8, d)), _full((d, d)), _full((d, d)), _full((d, d)),
                  _full((1, d)), _full((d, LANES)), _full((LANES, d)),
                  _full((1, d)), _full((d, LANES)), _full((LANES, d)),
                  _full((d, LANES)), _full((LANES, d)), _full((1, d)), _full((1, d))],
        out_specs=[blk] * 7,
        out_shape=[out] * 7,
        scratch_shapes=[pltpu.VMEM((8, d), F32)],
        compiler_params=_params(("arbitrary",)),
        name="rwkv_proj",
    )(x, _row(n1), mu8, w_rkv[0].astype(BF16), w_rkv[1].astype(BF16), w_rkv[2].astype(BF16),
      _row(w0), padc(w1), padr(w2), _row(a0), padc(a1), padr(a2), padc(g1), padr(g2), _row(k_k), _row(k_a))


def _unit_lower_inverse(a_strict, row, col):
    n = a_strict.shape[0]
    eye = (row == col).astype(F32)
    blk16 = (row // 16) == (col // 16)
    p = jnp.where(blk16, a_strict, 0.0)
    inv = eye + p
    p = _bdot(p, p)
    yield
    for _ in range(2):
        both = _bdot(jnp.concatenate([p, inv], axis=0), p)
        yield
        p, inv = both[:n], inv + both[n:]
    inv16 = inv + _bdot(inv, p)
    yield
    blk32 = (row // 32) == (col // 32)
    off32 = jnp.where(blk32 & ~blk16, a_strict, 0.0)
    off64 = jnp.where(blk32, 0.0, a_strict)
    x = _bdot(inv16, jnp.concatenate([off32, off64], axis=1))
    yield
    x32, x64 = x[:, :n], x[:, n:]
    y = _bdot(x32, jnp.concatenate([inv16, x64], axis=1))
    yield
    inv32 = inv16 + y[:, :n]
    return inv32 + _bdot(x64 + y[:, n:], inv32)


def _rwkv_scan_kernel(r_ref, lw_ref, k_ref, v_ref, kkraw_ref, a_ref, g_ref, rk_ref, gg_ref, gb_ref, o_ref, s_ref):
    tb = r_ref.shape[0]
    n = RWKV_HEAD
    grp = 2 * CHUNK
    big = 2 * grp

    @pl.when(pl.program_id(1) == 0)
    def _():
        s_ref[...] = jnp.zeros_like(s_ref)

    rown = lax.broadcasted_iota(jnp.int32, (grp, LANES), 0)
    head0 = lax.broadcasted_iota(jnp.int32, (1, LANES), 1) < n
    chunk0 = rown < CHUNK
    row, col = _tri_masks(big)
    same = (row // CHUNK) == (col // CHUNK)
    incl = same & (row >= col)
    strict = same & (row > col)
    tri = incl[:grp, :grp].astype(BF16)
    tri3 = jnp.concatenate([tri, tri, tri], axis=1)
    srow = lax.broadcasted_iota(jnp.int32, (big, LANES), 0)
    slane = lax.broadcasted_iota(jnp.int32, (big, LANES), 1)
    srow_chunk0 = (srow // CHUNK) % 2 == 0
    diag = (srow % grp) == slane
    own = ((rown < n) == head0)

    def stack(t):
        return jnp.concatenate([jnp.where(head0, t, 0.0), jnp.where(head0, 0.0, t)], axis=0)

    def blockdiag(t):
        s = stack(t)
        return jnp.concatenate([jnp.where(srow_chunk0, s, 0.0), jnp.where(srow_chunk0, 0.0, s)], axis=1)

    def spread(t):
        return jnp.concatenate([jnp.where(srow < grp, t, 0.0), jnp.where(srow < grp, 0.0, t)], axis=1)

    def head_sum(t):
        return jnp.where(head0, jnp.sum(jnp.where(head0, t, 0.0), axis=-1, keepdims=True),
                         jnp.sum(jnp.where(head0, 0.0, t), axis=-1, keepdims=True))

    def per_chunk(t, c):
        return jnp.concatenate([t[c * CHUNK:(c + 1) * CHUNK], t[grp + c * CHUNK:grp + (c + 1) * CHUNK]], axis=0)

    def phase1(g):
        rows = slice(g * grp, (g + 1) * grp)
        r, lw, k, v = r_ref[rows, :], lw_ref[rows, :], k_ref[rows, :], v_ref[rows, :]
        kk, a = kkraw_ref[rows, :], a_ref[rows, :]
        kk = kk / jnp.maximum(jnp.sqrt(head_sum(kk * kk)), 1e-12)
        cum = _cumsum_rows(tri3, lw)
        yield
        cum_last = jnp.where(chunk0, cum[CHUNK - 1:CHUNK, :], cum[grp - 1:grp, :])
        e_neg, e_end = jnp.exp(-cum), jnp.exp(cum_last - cum)
        b_vec = kk * a
        a_t = stack(-kk * jnp.exp(cum - lw))
        r_t = stack(r * jnp.exp(cum))
        v_s = stack(v)
        a4 = _bdot_nt(jnp.concatenate([a_t, r_t], axis=0),
                      jnp.concatenate([b_vec * e_neg, k * e_neg], axis=0))
        yield
        a_ab = jnp.where(strict, spread(a4[:big, :grp]), 0.0)
        a_ak = jnp.where(strict, spread(a4[:big, grp:]), 0.0)
        a_rb = jnp.where(incl, spread(a4[big:, :grp]), 0.0)
        a_rk = jnp.where(incl, spread(a4[big:, grp:]), 0.0)
        akv = _bdot(a_ak, v_s)
        t_inv = yield from _unit_lower_inverse(a_ab, row, col)
        yield
        z = _bdot(t_inv, jnp.concatenate([a_t, akv], axis=1))
        yield
        zv = jnp.concatenate([z, jnp.concatenate([jnp.zeros_like(v_s), v_s], axis=1)], axis=0)
        gh = _bdot_tn(jnp.concatenate([blockdiag(b_vec * e_end), blockdiag(k * e_end)], axis=0), zv)
        ry = _bdot(jnp.concatenate([a_rb, a_rk], axis=1), zv)
        p_end = jnp.exp(jnp.concatenate([cum[CHUNK - 1:CHUNK, :], cum[grp - 1:grp, :]], axis=0))
        g_all = gh[:, :LANES] + jnp.where(diag, jnp.where(srow < grp, p_end[0:1, :], p_end[1:2, :]), 0.0)
        h_all = gh[:, LANES:]
        rp_all = r_t + ry[:, :LANES]
        y0_all = ry[:, LANES:]
        return g_all, h_all, rp_all, y0_all

    def emit(g, c, y):
        crow = slice(g * grp + c * CHUNK, g * grp + (c + 1) * CHUNK)
        mean = jnp.sum(y, axis=-1, keepdims=True) * (1.0 / n)
        dev = jnp.where(own, y - mean, 0.0)
        var = jnp.sum(dev * dev, axis=-1, keepdims=True) * (1.0 / n)
        yn = dev * lax.rsqrt(var + RWKV_GN_EPS)
        yn = yn[:CHUNK] + yn[CHUNK:]
        rc, kc, vc = r_ref[crow, :], k_ref[crow, :], v_ref[crow, :]
        bonus = head_sum(rc * kc * rk_ref[...])[:CHUNK] * vc
        o_ref[crow, :] = ((yn * gg_ref[...] + gb_ref[...] + bonus) * g_ref[crow, :]).astype(BF16)

    def phase2(g0, staged):
        for g, (g_all, h_all, rp_all, y0_all) in enumerate(staged, g0):
            for c in range(2):
                lhs = jnp.concatenate([g_all[c * grp:(c + 1) * grp], per_chunk(rp_all, c)], axis=0)
                both = _bdot(lhs, s_ref[...])
                yield
                s_ref[...] = both[:grp] + h_all[c * grp:(c + 1) * grp]
                emit(g, c, both[grp:] + per_chunk(y0_all, c))

    n_groups, wave = tb // grp, 4
    pending = None
    for g0 in range(0, n_groups, wave):
        gens = [phase1(g) for g in range(g0, min(g0 + wave, n_groups))]
        done = _lockstep(gens + ([pending] if pending is not None else []))
        pending = phase2(g0, done[:len(gens)])
    _lockstep([pending])


def _rwkv_scan(r, lw, k, v, kkraw, a, g, r_k, gn_g, gn_b, tb=2048):
    s, d = r.shape
    tb = min(tb, s)
    blk = pl.BlockSpec((tb, LANES), lambda p, i: (i, p))
    vec = pl.BlockSpec((1, LANES), lambda p, i: (0, p))
    return pl.pallas_call(
        _rwkv_scan_kernel,
        grid=(d // LANES, s // tb),
        in_specs=[blk] * 7 + [vec] * 3,
        out_specs=blk,
        out_shape=jax.ShapeDtypeStruct((s, d), BF16),
        scratch_shapes=[pltpu.VMEM((LANES, LANES), F32)],
        compiler_params=_params(("arbitrary", "arbitrary")),
        name="rwkv_scan",
    )(r, lw, k, v, kkraw, a, g, _row(r_k), _row(gn_g), _row(gn_b))


def _sb_qkv_kernel(x_ref, n1_ref, w_ref, q_ref, k_ref, v_ref):
    d = x_ref.shape[1]
    h = _rms(x_ref[...], n1_ref[...]).astype(BF16)
    qkv = _dot(h, w_ref[...])
    q_ref[...] = (qkv[:, :d] * SB_HEAD_DIM ** -0.5).astype(BF16)
    k_ref[...] = qkv[:, d:2 * d].astype(BF16)
    v_ref[...] = qkv[:, 2 * d:].astype(BF16)


def _sb_qkv(x, n1, w_qkv, tm=1024):
    s, d = x.shape
    blk = pl.BlockSpec((tm, d), lambda i: (i, 0))
    out = jax.ShapeDtypeStruct((s, d), BF16)
    return pl.pallas_call(
        _sb_qkv_kernel,
        grid=(s // tm,),
        in_specs=[blk, _full((1, d)), _full((d, 3 * d))],
        out_specs=[blk] * 3,
        out_shape=[out] * 3,
        compiler_params=_params(("arbitrary",)),
        name="sb_qkv",
    )(x, _row(n1), w_qkv.astype(BF16))


def _sb_attn_kernel(q_ref, k_ref, v_ref, o_ref, *, tq):
    qi = pl.program_id(1)
    head0 = lax.broadcasted_iota(jnp.int32, (1, LANES), 1) < SB_HEAD_DIM
    q = q_ref[...]
    zero = jnp.zeros_like(q)
    q2 = jnp.concatenate([jnp.where(head0, q, zero), jnp.where(head0, zero, q)], axis=0)
    row, col = _tri_masks(tq)
    before = jnp.concatenate([col < row, col < row], axis=0)
    later = (row > col).astype(BF16)
    later2 = jnp.concatenate([later, later], axis=0)

    def pair(kb, c, acc, diagonal):
        valid = kb >= 1
        ks = (pl.ds(pl.multiple_of(kb * tq, tq), tq), pl.ds(pl.multiple_of(jnp.maximum(kb - 1, 0) * tq, tq), tq))
        zs = [_dot_nt(q2, k_ref[s, :]) for s in ks]
        sps, excls = [], []
        for j, z in enumerate(zs):
            sp = _softplus(z)
            if diagonal and j == 0:
                sp = jnp.where(before, sp, 0.0)
            hi = sp.astype(BF16)
            sps.append(sp)
            excls.append(_dot(jnp.concatenate([hi, (sp - hi.astype(F32)).astype(BF16)], axis=1), later2))
        tot0 = excls[0][:, 0:1] + sps[0][:, 0:1]
        tot1 = jnp.where(valid, excls[1][:, 0:1] + sps[1][:, 0:1], 0.0)
        w0 = jnp.exp(zs[0] - sps[0] - excls[0] + c)
        if diagonal:
            w0 = jnp.where(before, w0, 0.0)
        w1 = jnp.where(valid, jnp.exp(zs[1] - sps[1] - excls[1] + (c - tot0)), 0.0)
        pv = _dot(jnp.concatenate([w0.astype(BF16), w1.astype(BF16)], axis=1),
                  jnp.concatenate([v_ref[ks[0], :], v_ref[ks[1], :]], axis=0))
        return c - tot0 - tot1, acc + pv

    c, acc = pair(qi, jnp.zeros((2 * tq, 1), F32), jnp.zeros((2 * tq, LANES), F32), True)

    def cond(st):
        kb, c, _ = st
        return jnp.logical_and(kb >= 0, jnp.max(c) > SB_ZERO_LOG)

    def body(st):
        kb, c, acc = st
        c, acc = pair(kb, c, acc, False)
        return kb - 2, c, acc

    _, _, acc = lax.while_loop(cond, body, (qi - 2, c, acc))
    o_ref[...] = jnp.where(head0, acc[:tq], acc[tq:]).astype(BF16)


def _sb_attn(q, k, v, tq=256):
    s, d = q.shape
    return pl.pallas_call(
        functools.partial(_sb_attn_kernel, tq=tq),
        grid=(d // LANES, s // tq),
        in_specs=[pl.BlockSpec((tq, LANES), lambda p, i: (i, p)),
                  pl.BlockSpec((s, LANES), lambda p, i: (0, p)),
                  pl.BlockSpec((s, LANES), lambda p, i: (0, p))],
        out_specs=pl.BlockSpec((tq, LANES), lambda p, i: (i, p)),
        out_shape=jax.ShapeDtypeStruct((s, d), BF16),
        compiler_params=_params(("arbitrary", "arbitrary")),
        name="sb_attn",
    )(q, k, v)


def _stick_breaking(x, n1, w_qkv):
    q, k, v = _sb_qkv(x, n1, w_qkv)
    return _sb_attn(q, k, v)


def _rwkv7(x, n1, mu, w_rkv, w0, w1, w2, a0, a1, a2, g1, g2, k_k, k_a, r_k, gn_g, gn_b):
    r, lw, k, v, kkraw, a, g = _rwkv_proj(x, n1, mu, w_rkv, w0, w1, w2, a0, a1, a2, g1, g2, k_k, k_a)
    return _rwkv_scan(r, lw, k, v, kkraw, a, g, r_k, gn_g, gn_b)


def kernel(x, l0_norm1, l0_gla_w_in, l0_gla_w_alpha_up, l0_gla_b_alpha, l0_gla_out_norm, l0_gla_w_out, l0_norm2, l0_ffn_w_up, l0_ffn_conv_w, l0_ffn_conv_b, l0_ffn_w_down, l1_norm1, l1_rwkv_mu, l1_rwkv_w_rkv, l1_rwkv_w0, l1_rwkv_w1, l1_rwkv_w2, l1_rwkv_a0, l1_rwkv_a1, l1_rwkv_a2, l1_rwkv_g1, l1_rwkv_g2, l1_rwkv_k_k, l1_rwkv_k_a, l1_rwkv_r_k, l1_rwkv_gn_g, l1_rwkv_gn_b, l1_rwkv_w_out, l1_norm2, l1_ffn_w_up, l1_ffn_conv_w, l1_ffn_conv_b, l1_ffn_w_down, l2_norm1, l2_sb_w_qkv, l2_sb_w_out, l2_norm2, l2_ffn_w_up, l2_ffn_conv_w, l2_ffn_conv_b, l2_ffn_w_down, l3_norm1, l3_ml_w_in, l3_ml_b_if, l3_ml_out_norm, l3_ml_w_out, l3_norm2, l3_ffn_w_up, l3_ffn_conv_w, l3_ffn_conv_b, l3_ffn_w_down, final_norm):
    b, s, d = x.shape
    outs = []
    for bi in range(b):
        h = x[bi]
        h = _gla(h, l0_norm1, l0_gla_w_in, l0_gla_w_alpha_up, l0_gla_b_alpha, l0_gla_out_norm, l0_gla_w_out)
        h = _ffn(h, l0_norm2, l0_ffn_w_up, l0_ffn_conv_w, l0_ffn_conv_b, l0_ffn_w_down)
        a = _rwkv7(h, l1_norm1, l1_rwkv_mu, l1_rwkv_w_rkv, l1_rwkv_w0, l1_rwkv_w1, l1_rwkv_w2,
                   l1_rwkv_a0, l1_rwkv_a1, l1_rwkv_a2, l1_rwkv_g1, l1_rwkv_g2,
                   l1_rwkv_k_k, l1_rwkv_k_a, l1_rwkv_r_k, l1_rwkv_gn_g, l1_rwkv_gn_b)
        h = _ffn(h, l1_norm2, l1_ffn_w_up, l1_ffn_conv_w, l1_ffn_conv_b, l1_ffn_w_down, mixer=(a, l1_rwkv_w_out))
        a = _stick_breaking(h, l2_norm1, l2_sb_w_qkv)
        h = _ffn(h, l2_norm2, l2_ffn_w_up, l2_ffn_conv_w, l2_ffn_conv_b, l2_ffn_w_down, mixer=(a, l2_sb_w_out))
        h = _mlstm(h, l3_norm1, l3_ml_w_in, l3_ml_b_if, l3_ml_out_norm, l3_ml_w_out)
        h = _ffn(h, l3_norm2, l3_ffn_w_up, l3_ffn_conv_w, l3_ffn_conv_b, l3_ffn_w_down, final_g=final_norm)
        outs.append(h)
    return jnp.stack(outs, axis=0)
```
